```python
import jax, jax.numpy as jnp
from jax import lax
import numpy as np

D_MODEL = 1024
BATCH = 32
SEQ = 2048
DEPTH = 4

MEM_LEN = 256
MEM_HEADS = 4
MEM_HEAD_DIM = D_MODEL // 8
RET_HEADS = 4
RET_QK_DIM = D_MODEL // 8
RET_V_DIM = D_MODEL // 4
RET_CHUNK = 128
ROPE_BASE = 10000.0
POOL_WINDOWS = (2, 4, 8, 16)
POOL_GROUP = D_MODEL // 8
POOL_WIDTH = POOL_GROUP * len(POOL_WINDOWS)
FFN_HIDDEN = 4 * D_MODEL
N_BRANCHES = 3
EPS = 1e-6

RET_QK_W = RET_HEADS * RET_QK_DIM
RET_V_W = RET_HEADS * RET_V_DIM
MEM_Q_W = MEM_HEADS * MEM_HEAD_DIM
COL_SIZES = (RET_QK_W, RET_QK_W, RET_V_W, RET_V_W, POOL_WIDTH, MEM_Q_W, N_BRANCHES * D_MODEL)
COL_SPLITS = tuple(int(s) for s in np.cumsum(COL_SIZES)[:-1])
IN_PROJ_W = int(sum(COL_SIZES))

kernel_name = "hybrid_retention_pool_memory_encoder"


def rms_norm(x, g):
    xf = x.astype(jnp.float32)
    y = xf * lax.rsqrt(jnp.mean(jnp.square(xf), axis=-1, keepdims=True) + EPS)
    return (y * g.astype(jnp.float32)).astype(x.dtype)


def rotary(x, pos):
    d = x.shape[-1]
    inv = ROPE_BASE ** (-jnp.arange(0, d, 2, dtype=jnp.float32) / d)
    ang = pos.astype(jnp.float32)[:, None] * inv[None, :]
    cos = jnp.cos(ang)[None, :, None, :].astype(x.dtype)
    sin = jnp.sin(ang)[None, :, None, :].astype(x.dtype)
    x1, x2 = jnp.split(x, 2, axis=-1)
    return jnp.concatenate([x1 * cos - x2 * sin, x1 * sin + x2 * cos], axis=-1)


def retention_one_direction(q, k, v, log_g, strict):
    B, S, H, dk = q.shape
    dv = v.shape[-1]
    C = RET_CHUNK
    N = S // C
    q = q.reshape(B, N, C, H, dk)
    k = k.reshape(B, N, C, H, dk)
    v = v.reshape(B, N, C, H, dv)
    idx = jnp.arange(C, dtype=jnp.float32)
    diff = idx[:, None] - idx[None, :]
    mask = (diff > 0) if strict else (diff >= 0)
    dmat = jnp.where(mask[None], jnp.exp(log_g[:, None, None] * jnp.where(mask, diff, 0.0)[None]), 0.0)
    s = jnp.einsum('bnihd,bnjhd->bnhij', q, k) * dmat
    o_intra = jnp.einsum('bnhij,bnjhe->bnihe', s, v)
    zeta = jnp.exp(log_g[None, :] * (C - 1 - idx)[:, None])
    xi = jnp.exp(log_g[None, :] * (idx + 1)[:, None])
    chunk_decay = jnp.exp(log_g * C)
    qx = q * xi[:, :, None]
    kz = k * zeta[:, :, None]

    def step(state, inp):
        qc, kc, vc = inp
        out = jnp.einsum('bihd,bhde->bihe', qc, state)
        state = state * chunk_decay[None, :, None, None] + jnp.einsum('bjhd,bjhe->bhde', kc, vc)
        return state, out

    state0 = jnp.zeros((B, H, dk, dv), jnp.float32)
    _, o_cross = lax.scan(step, state0, (jnp.moveaxis(qx, 1, 0), jnp.moveaxis(kz, 1, 0), jnp.moveaxis(v, 1, 0)))
    o = o_intra.astype(jnp.float32) + jnp.moveaxis(o_cross, 0, 1)
    return o.reshape(B, S, H, dv)


def retention_branch(q, k, v, g, decay_logit, w_o, pos):
    B, S, _ = q.shape
    q = rotary(q.reshape(B, S, RET_HEADS, RET_QK_DIM), pos)
    k = rotary(k.reshape(B, S, RET_HEADS, RET_QK_DIM), pos) * (RET_QK_DIM ** -0.5)
    v = v.reshape(B, S, RET_HEADS, RET_V_DIM)
    log_g = jax.nn.log_sigmoid(decay_logit.astype(jnp.float32))
    o_fwd = retention_one_direction(q, k, v, log_g[0], strict=False)
    o_bwd = jnp.flip(retention_one_direction(jnp.flip(q, 1), jnp.flip(k, 1), jnp.flip(v, 1), log_g[1], strict=True), 1)
    o = o_fwd + o_bwd
    mu = jnp.mean(o, axis=-1, keepdims=True)
    var = jnp.mean(jnp.square(o - mu), axis=-1, keepdims=True)
    o = ((o - mu) * lax.rsqrt(var + EPS)).reshape(B, S, RET_V_W).astype(g.dtype)
    return (o * jax.nn.silu(g)) @ w_o


def pool_branch(p, w_grp, scale, w_o):
    B, S, _ = p.shape
    pf = p.astype(jnp.float32)
    cs = jnp.concatenate([jnp.zeros((B, 1, POOL_WIDTH), jnp.float32), jnp.cumsum(pf, axis=1)], axis=1)
    n = jnp.arange(S)
    groups = []
    for gi, w in enumerate(POOL_WINDOWS):
        lo = jnp.clip(n - w // 2, 0, S)
        hi = jnp.clip(n + w // 2, 0, S)
        sl = slice(gi * POOL_GROUP, (gi + 1) * POOL_GROUP)
        csg = cs[..., sl]
        mean = (csg[:, hi] - csg[:, lo]) / (hi - lo).astype(jnp.float32)[None, :, None]
        groups.append(mean - pf[..., sl])
    mixed = jnp.stack(groups, axis=2)
    y = jnp.einsum('bsgc,gcd->bsgd', mixed, w_grp.astype(jnp.float32)).reshape(B, S, POOL_WIDTH)
    y = (y * scale.astype(jnp.float32)).astype(p.dtype)
    return y @ w_o


def memory_branch(q, mem_n, w_kv, w_o):
    B, S, _ = q.shape
    M = mem_n.shape[1]
    q = q.reshape(B, S, MEM_HEADS, MEM_HEAD_DIM)
    kv = mem_n @ w_kv
    k, v = jnp.split(kv, 2, axis=-1)
    k = k.reshape(B, M, MEM_HEADS, MEM_HEAD_DIM)
    v = v.reshape(B, M, MEM_HEADS, MEM_HEAD_DIM)
    s = jnp.einsum('bshd,bmhd->bhsm', q, k).astype(jnp.float32) * (MEM_HEAD_DIM ** -0.5)
    a = jax.nn.softmax(s, axis=-1).astype(v.dtype)
    o = jnp.einsum('bhsm,bmhd->bshd', a, v).reshape(B, S, MEM_Q_W)
    return o @ w_o


def setup_inputs(seed: int = 0) -> dict:
    key = jax.random.key(seed)
    ks = jax.random.split(key, 20)
    f32 = jnp.float32

    def dense(k, shape, fan_in):
        return jax.random.normal(k, shape, f32) * (fan_in ** -0.5)

    base = 1.0 - 2.0 ** (-5.0 - jnp.arange(RET_HEADS, dtype=f32))
    base_logit = jnp.log(base) - jnp.log1p(-base)
    decay_logit = base_logit[None, None, :] + 0.05 * jax.random.normal(ks[3], (DEPTH, 2, RET_HEADS), f32)
    return {
        "x": jax.random.normal(ks[0], (BATCH, SEQ, D_MODEL), f32),
        "mem": jax.random.normal(ks[1], (BATCH, MEM_LEN, D_MODEL), f32),
        "w_in": dense(ks[2], (DEPTH, D_MODEL, IN_PROJ_W), D_MODEL),
        "ret_decay_logit": decay_logit,
        "w_ret_o": dense(ks[4], (DEPTH, RET_V_W, D_MODEL), RET_V_W),
        "w_pool_grp": dense(ks[5], (DEPTH, len(POOL_WINDOWS), POOL_GROUP, POOL_GROUP), POOL_GROUP),
        "pool_scale": 1.0 + 0.1 * jax.random.normal(ks[6], (DEPTH, POOL_WIDTH), f32),
        "w_pool_o": dense(ks[7], (DEPTH, POOL_WIDTH, D_MODEL), POOL_WIDTH),
        "w_mem_kv": dense(ks[8], (DEPTH, D_MODEL, 2 * MEM_Q_W), D_MODEL),
        "w_mem_o": dense(ks[9], (DEPTH, MEM_Q_W, D_MODEL), MEM_Q_W),
        "w_out": dense(ks[10], (DEPTH, D_MODEL, D_MODEL), D_MODEL),
        "w_ff1": dense(ks[11], (DEPTH, D_MODEL, FFN_HIDDEN), D_MODEL),
        "w_ff2": dense(ks[12], (DEPTH, FFN_HIDDEN, D_MODEL), FFN_HIDDEN),
        "norm1_g": 1.0 + 0.05 * jax.random.normal(ks[13], (DEPTH, D_MODEL), f32),
        "norm2_g": 1.0 + 0.05 * jax.random.normal(ks[14], (DEPTH, D_MODEL), f32),
        "mem_norm_g": 1.0 + 0.05 * jax.random.normal(ks[15], (D_MODEL,), f32),
        "final_norm_g": 1.0 + 0.05 * jax.random.normal(ks[16], (D_MODEL,), f32),
    }


def reference(x, mem, w_in, ret_decay_logit, w_ret_o, w_pool_grp, pool_scale, w_pool_o,
              w_mem_kv, w_mem_o, w_out, w_ff1, w_ff2, norm1_g, norm2_g, mem_norm_g, final_norm_g):
    S = x.shape[1]
    pos = jnp.arange(S)
    mem_n = rms_norm(mem, mem_norm_g)
    for l in range(DEPTH):
        h = rms_norm(x, norm1_g[l])
        proj = h @ w_in[l]
        q_r, k_r, v_r, g_r, p_in, q_m, gates = jnp.split(proj, COL_SPLITS, axis=-1)
        o_ret = retention_branch(q_r, k_r, v_r, g_r, ret_decay_logit[l], w_ret_o[l], pos)
        o_pool = pool_branch(p_in, w_pool_grp[l], pool_scale[l], w_pool_o[l])
        o_mem = memory_branch(q_m, mem_n, w_mem_kv[l], w_mem_o[l])
        gate_r, gate_p, gate_m = jnp.split(jax.nn.sigmoid(gates), N_BRANCHES, axis=-1)
        merged = (gate_r * o_ret + gate_p * o_pool + gate_m * o_mem).astype(x.dtype)
        x = x + merged @ w_out[l]
        h = rms_norm(x, norm2_g[l])
        x = x + jnp.square(jax.nn.relu(h @ w_ff1[l])) @ w_ff2[l]
    return rms_norm(x, final_norm_g)
```

```python
import functools

import jax
import jax.numpy as jnp
from jax import lax
from jax.experimental import pallas as pl
from jax.experimental.pallas import tpu as pltpu

F32 = jnp.float32
BF16 = jnp.bfloat16

RET_HEADS = 4
RET_QK_DIM = 128
RET_V_DIM = 256
MEM_HEADS = 4
MEM_HEAD_DIM = 128
POOL_WINDOWS = (2, 4, 8, 16)
POOL_GROUP = 128
N_BRANCHES = 3
ROPE_BASE = 10000.0
EPS = 1e-6

RET_QK_W = RET_HEADS * RET_QK_DIM
RET_V_W = RET_HEADS * RET_V_DIM
MEM_Q_W = MEM_HEADS * MEM_HEAD_DIM
POOL_W = POOL_GROUP * len(POOL_WINDOWS)
POOL_HALO = max(POOL_WINDOWS) // 2

V7X_LANES = 128
V7X_VMEM_BYTES = 64 * 1024 * 1024
VMEM_LIMIT_BYTES = V7X_VMEM_BYTES - 8 * 1024 * 1024

RET_CHUNK = 256
ROW_TILE = 512
MIX_TILE = 512
COL_CHUNK = 512


def _params(*semantics):
    return pltpu.CompilerParams(dimension_semantics=semantics, vmem_limit_bytes=VMEM_LIMIT_BYTES)


def _resident(shape):
    zeros = (0,) * len(shape)
    return pl.BlockSpec(shape, lambda *_: zeros, pipeline_mode=pl.Buffered(1))


def _rms_norm(x, gain):
    return x * lax.rsqrt(jnp.mean(x * x, axis=-1, keepdims=True) + EPS) * gain


def _sigmoid(x):
    return 1.0 / (1.0 + jnp.exp(-x))


def _dot(a, b):
    return jnp.dot(a, b, preferred_element_type=F32)


def _dot_nt(a, b):
    return lax.dot_general(a, b, (((1,), (1,)), ((), ())), preferred_element_type=F32)


def _dot_tn(a, b):
    return lax.dot_general(a, b, (((0,), (0,)), ((), ())), preferred_element_type=F32)


def _decay_tables_kernel(logit_ref, dmat_ref, xi_f_ref, xi_b_ref, zeta_f_ref, zeta_b_ref, dec_ref):
    c = RET_CHUNK
    logit = logit_ref[0]
    log_g = jnp.minimum(logit, 0.0) - jnp.log1p(jnp.exp(-jnp.abs(logit)))
    lg_f, lg_b = log_g[0], log_g[1]
    row = lax.broadcasted_iota(jnp.int32, (c, RET_QK_W), 0).astype(F32)
    xi_f_ref[0] = jnp.exp(lg_f * (row + 1.0))
    xi_b_ref[0] = jnp.exp(lg_b * (c - row))
    zeta_f_ref[0] = jnp.exp(lg_f * (c - 1.0 - row))
    zeta_b_ref[0] = jnp.exp(lg_b * row)
    dec_ref[0, 0] = jnp.exp(lg_f * c)
    dec_ref[0, 1] = jnp.exp(lg_b * c)
    i = lax.broadcasted_iota(jnp.int32, (c, c), 0)
    j = lax.broadcasted_iota(jnp.int32, (c, c), 1)
    diff = (i - j).astype(F32)
    for h in range(RET_HEADS):
        lf = lg_f[:, h * RET_QK_DIM:h * RET_QK_DIM + 1]
        lb = lg_b[:, h * RET_QK_DIM:h * RET_QK_DIM + 1]
        fwd = jnp.exp(lf * jnp.maximum(diff, 0.0))
        bwd = jnp.exp(lb * jnp.maximum(-diff, 0.0))
        dmat_ref[0, h] = jnp.where(diff >= 0.0, fwd, bwd)


def _decay_tables(ret_decay_logit):
    depth = ret_decay_logit.shape[0]
    c = RET_CHUNK
    logit = jnp.repeat(ret_decay_logit.astype(F32), RET_QK_DIM, axis=-1)[:, :, None, :]
    vec = jax.ShapeDtypeStruct((depth, c, RET_QK_W), F32)
    vec_spec = pl.BlockSpec((1, c, RET_QK_W), lambda l: (l, 0, 0))
    return pl.pallas_call(
        _decay_tables_kernel,
        grid=(depth,),
        in_specs=[pl.BlockSpec((1, 2, 1, RET_QK_W), lambda l: (l, 0, 0, 0))],
        out_specs=[pl.BlockSpec((1, RET_HEADS, c, c), lambda l: (l, 0, 0, 0)),
                   vec_spec, vec_spec, vec_spec, vec_spec,
                   pl.BlockSpec((1, 2, 1, RET_QK_W), lambda l: (l, 0, 0, 0))],
        out_shape=[jax.ShapeDtypeStruct((depth, RET_HEADS, c, c), F32), vec, vec, vec, vec,
                   jax.ShapeDtypeStruct((depth, 2, 1, RET_QK_W), F32)],
        compiler_params=_params("arbitrary"),
        name="decay_tables",
    )(logit)


def _mem_kv_kernel(mem_ref, gain_ref, w_ref, kv_ref):
    mem_n = _rms_norm(mem_ref[0], gain_ref[...]).astype(BF16)
    for l in range(w_ref.shape[0]):
        kv_ref[l, 0] = _dot(mem_n, w_ref[l]).astype(BF16)


def _mem_kv(mem, mem_norm_g, w_mem_kv):
    b, m, d = mem.shape
    depth, _, kvw = w_mem_kv.shape
    return pl.pallas_call(
        _mem_kv_kernel,
        grid=(b,),
        in_specs=[pl.BlockSpec((1, m, d), lambda i: (i, 0, 0)),
                  _resident((1, d)),
                  _resident((depth, d, kvw))],
        out_specs=pl.BlockSpec((depth, 1, m, kvw), lambda i: (0, i, 0, 0)),
        out_shape=jax.ShapeDtypeStruct((depth, b, m, kvw), BF16),
        compiler_params=_params("arbitrary"),
        name="mem_kv",
    )(mem, mem_norm_g.reshape(1, d), w_mem_kv)


def _in_proj_kernel(x_ref, gain_ref, w_ref, cos_ref, sin_ref,
                    qk_ref, v_ref, g_ref, p_ref, qm_ref, gates_ref):
    h = _rms_norm(x_ref[...], gain_ref[...]).astype(BF16)
    cos = cos_ref[...]
    sin = sin_ref[...]

    def proj(col, width=COL_CHUNK):
        return _dot(h, w_ref[:, col:col + width])

    def rotary(a):
        heads = []
        for hd in range(RET_HEADS):
            ah = a[:, hd * RET_QK_DIM:(hd + 1) * RET_QK_DIM]
            heads.append(ah * cos + pltpu.roll(ah, RET_QK_DIM // 2, axis=1) * sin)
        return jnp.concatenate(heads, axis=1)

    col = 0
    qk_ref[:, :RET_QK_W] = rotary(proj(col)).astype(BF16)
    col += RET_QK_W
    qk_ref[:, RET_QK_W:] = (rotary(proj(col)) * (RET_QK_DIM ** -0.5)).astype(BF16)
    col += RET_QK_W
    for c in range(0, RET_V_W, COL_CHUNK):
        v_ref[:, c:c + COL_CHUNK] = proj(col + c).astype(BF16)
    col += RET_V_W
    for c in range(0, RET_V_W, COL_CHUNK):
        a = proj(col + c)
        g_ref[:, c:c + COL_CHUNK] = (a * _sigmoid(a)).astype(BF16)
    col += RET_V_W
    p_ref[...] = proj(col, POOL_W)
    col += POOL_W
    qm_ref[...] = proj(col, MEM_Q_W).astype(BF16)
    col += MEM_Q_W
    for c in range(0, gates_ref.shape[1], COL_CHUNK):
        gates_ref[:, c:c + COL_CHUNK] = _sigmoid(proj(col + c)).astype(BF16)


def _in_proj(x2, gain, w_in, cos, sin, seq):
    t, d = x2.shape
    tm = ROW_TILE
    gates_w = N_BRANCHES * d
    pos_tiles = seq // tm

    def rows(width):
        return pl.BlockSpec((tm, width), lambda i: (i, 0))

    pos_spec = pl.BlockSpec((tm, RET_QK_DIM), lambda i: (i % pos_tiles, 0))
    return pl.pallas_call(
        _in_proj_kernel,
        grid=(t // tm,),
        in_specs=[rows(d), _resident((1, d)), _resident(w_in.shape), pos_spec, pos_spec],
        out_specs=[rows(2 * RET_QK_W), rows(RET_V_W), rows(RET_V_W), rows(POOL_W), rows(MEM_Q_W),
                   rows(gates_w)],
        out_shape=[jax.ShapeDtypeStruct((t, 2 * RET_QK_W), BF16),
                   jax.ShapeDtypeStruct((t, RET_V_W), BF16),
                   jax.ShapeDtypeStruct((t, RET_V_W), BF16),
                   jax.ShapeDtypeStruct((t, POOL_W), F32),
                   jax.ShapeDtypeStruct((t, MEM_Q_W), BF16),
                   jax.ShapeDtypeStruct((t, gates_w), BF16)],
        compiler_params=_params("arbitrary"),
        name="in_proj",
    )(x2, gain.reshape(1, d), w_in, cos, sin)


def _ret_state_kernel(kf_ref, vf_ref, kb_ref, vb_ref, zeta_f_ref, zeta_b_ref, dec_ref,
                      sf_ref, sb_ref, acc_f, acc_b):
    @pl.when(pl.program_id(1) == 0)
    def _():
        acc_f[...] = jnp.zeros_like(acc_f)
        acc_b[...] = jnp.zeros_like(acc_b)

    sf_ref[0] = acc_f[...].astype(BF16)
    sb_ref[0] = acc_b[...].astype(BF16)

    def update(acc, k_ref, v_ref, zeta_ref, dec):
        kz = (k_ref[...].astype(F32) * zeta_ref[0]).astype(BF16)
        for h in range(RET_HEADS):
            rows = slice(h * RET_QK_DIM, (h + 1) * RET_QK_DIM)
            outer = _dot_tn(kz[:, rows], v_ref[:, h * RET_V_DIM:(h + 1) * RET_V_DIM])
            acc[rows, :] = acc[rows, :] * dec[:, h * RET_QK_DIM:h * RET_QK_DIM + 1] + outer

    update(acc_f, kf_ref, vf_ref, zeta_f_ref, dec_ref[0, 0])
    update(acc_b, kb_ref, vb_ref, zeta_b_ref, dec_ref[0, 1])


def _ret_states(qk, v, zeta_f, zeta_b, dec, layer, batch, seq):
    c = RET_CHUNK
    n = seq // c
    state_rows = RET_HEADS * RET_QK_DIM

    def fwd(width, col):
        return pl.BlockSpec((c, width), lambda b, i: (b * n + i, col))

    def bwd(width, col):
        return pl.BlockSpec((c, width), lambda b, i: (b * n + n - 1 - i, col))

    table = pl.BlockSpec((1, c, RET_QK_W), lambda b, i: (layer, 0, 0), pipeline_mode=pl.Buffered(1))
    state = jax.ShapeDtypeStruct((batch * n, state_rows, RET_V_DIM), BF16)
    return pl.pallas_call(
        _ret_state_kernel,
        grid=(batch, n),
        in_specs=[fwd(RET_QK_W, 1), fwd(RET_V_W, 0), bwd(RET_QK_W, 1), bwd(RET_V_W, 0), table, table,
                  pl.BlockSpec((1, 2, 1, RET_QK_W), lambda b, i: (layer, 0, 0, 0),
                               pipeline_mode=pl.Buffered(1))],
        out_specs=[pl.BlockSpec((1, state_rows, RET_V_DIM), lambda b, i: (b * n + i, 0, 0)),
                   pl.BlockSpec((1, state_rows, RET_V_DIM), lambda b, i: (b * n + n - 1 - i, 0, 0))],
        out_shape=[state, state],
        scratch_shapes=[pltpu.VMEM((state_rows, RET_V_DIM), F32),
                        pltpu.VMEM((state_rows, RET_V_DIM), F32)],
        compiler_params=_params("arbitrary", "arbitrary"),
        name="ret_states",
    )(qk, v, qk, v, zeta_f, zeta_b, dec)


def _mix_kernel(x_ref, qk_ref, v_ref, g_ref, p_ref, p_prev_ref, p_next_ref, qm_ref, gates_ref,
                sf_ref, sb_ref, kv_ref, dmat_ref, xi_f_ref, xi_b_ref,
                w_ret_o_ref, w_grp_ref, scale_ref, w_pool_o_ref, w_mem_o_ref, w_out_ref,
                out_ref, ret_scr, pad_scr, *, seq):
    ts = x_ref.shape[0]
    d = x_ref.shape[1]
    c = RET_CHUNK
    tile = pl.program_id(1)
    n_tiles = pl.num_programs(1)

    xi_f = xi_f_ref[0]
    xi_b = xi_b_ref[0]
    for ci in range(ts // c):
        rows = slice(ci * c, (ci + 1) * c)
        q = qk_ref[rows, :RET_QK_W]
        q32 = q.astype(F32)
        q_f = (q32 * xi_f).astype(BF16)
        q_b = (q32 * xi_b).astype(BF16)
        for h in range(RET_HEADS):
            qk_cols = slice(h * RET_QK_DIM, (h + 1) * RET_QK_DIM)
            v_cols = slice(h * RET_V_DIM, (h + 1) * RET_V_DIM)
            k_h = qk_ref[rows, RET_QK_W + h * RET_QK_DIM:RET_QK_W + (h + 1) * RET_QK_DIM]
            s = _dot_nt(q[:, qk_cols], k_h) * dmat_ref[0, h]
            o = (_dot(s.astype(BF16), v_ref[rows, v_cols])
                 + _dot(q_f[:, qk_cols], sf_ref[ci, qk_cols, :])
                 + _dot(q_b[:, qk_cols], sb_ref[ci, qk_cols, :]))
            mu = jnp.mean(o, axis=-1, keepdims=True)
            cen = o - mu
            var = jnp.mean(cen * cen, axis=-1, keepdims=True)
            o_n = cen * lax.rsqrt(var + EPS)
            ret_scr[rows, v_cols] = (o_n * g_ref[rows, v_cols].astype(F32)).astype(BF16)
    o_ret = _dot(ret_scr[...], w_ret_o_ref[...])

    halo = POOL_HALO
    p = p_ref[...]
    pad_scr[:halo, :] = jnp.where(tile > 0, p_prev_ref[...], 0.0)
    pad_scr[halo:halo + ts, :] = p
    pad_scr[halo + ts:, :] = jnp.where(tile < n_tiles - 1, p_next_ref[...], 0.0)
    pos = tile * ts + lax.broadcasted_iota(jnp.int32, (ts, 1), 0)
    groups = []
    for gi, w in enumerate(POOL_WINDOWS):
        cols = slice(gi * POOL_GROUP, (gi + 1) * POOL_GROUP)
        half = w // 2
        win = pad_scr[halo - half:halo - half + ts, cols]
        for off in range(1 - half, half):
            win = win + pad_scr[halo + off:halo + off + ts, cols]
        count = (jnp.minimum(pos + half, seq) - jnp.maximum(pos - half, 0)).astype(F32)
        mixed = win / count - p[:, cols]
        y = _dot(mixed.astype(BF16), w_grp_ref[gi]) * scale_ref[:, cols]
        groups.append(y.astype(BF16))
    o_pool = _dot(jnp.concatenate(groups, axis=1), w_pool_o_ref[...])

    heads = []
    for h in range(MEM_HEADS):
        cols = slice(h * MEM_HEAD_DIM, (h + 1) * MEM_HEAD_DIM)
        k_h = kv_ref[0, :, cols]
        v_h = kv_ref[0, :, MEM_Q_W + h * MEM_HEAD_DIM:MEM_Q_W + (h + 1) * MEM_HEAD_DIM]
        s = _dot_nt(qm_ref[:, cols], k_h) * (MEM_HEAD_DIM ** -0.5)
        e = jnp.exp(s - jnp.max(s, axis=-1, keepdims=True))
        a = e / jnp.sum(e, axis=-1, keepdims=True)
        heads.append(_dot(a.astype(BF16), v_h).astype(BF16))
    o_mem = _dot(jnp.concatenate(heads, axis=1), w_mem_o_ref[...])

    merged = (gates_ref[:, :d].astype(F32) * o_ret
              + gates_ref[:, d:2 * d].astype(F32) * o_pool
              + gates_ref[:, 2 * d:].astype(F32) * o_mem)
    out_ref[...] = x_ref[...] + _dot(merged.astype(BF16), w_out_ref[...])


def _mix(x2, qk, v, g, p, qm, gates, sf, sb, kv, dmat, xi_f, xi_b, w_ret_o, w_grp, scale,
         w_pool_o, w_mem_o, w_out, layer, batch, seq):
    t, d = x2.shape
    ts = MIX_TILE
    c = RET_CHUNK
    nt = seq // ts
    halo = POOL_HALO
    halo_per_tile = ts // halo
    last_halo = t // halo - 1

    def rows(width):
        return pl.BlockSpec((ts, width), lambda b, i: (b * nt + i, 0))

    p_prev = pl.BlockSpec((halo, POOL_W), lambda b, i: (jnp.maximum((b * nt + i) * halo_per_tile - 1, 0), 0))
    p_next = pl.BlockSpec((halo, POOL_W),
                          lambda b, i: (jnp.minimum((b * nt + i + 1) * halo_per_tile, last_halo), 0))
    state = pl.BlockSpec((ts // c, RET_HEADS * RET_QK_DIM, RET_V_DIM), lambda b, i: (b * nt + i, 0, 0))
    kv_spec = pl.BlockSpec((1,) + kv.shape[2:], lambda b, i: (layer * batch + b, 0, 0))
    dmat_spec = pl.BlockSpec((1, RET_HEADS, c, c), lambda b, i: (layer, 0, 0, 0), pipeline_mode=pl.Buffered(1))
    xi_spec = pl.BlockSpec((1, c, RET_QK_W), lambda b, i: (layer, 0, 0), pipeline_mode=pl.Buffered(1))
    return pl.pallas_call(
        functools.partial(_mix_kernel, seq=seq),
        grid=(batch, nt),
        in_specs=[rows(d), rows(2 * RET_QK_W), rows(RET_V_W), rows(RET_V_W), rows(POOL_W), p_prev, p_next,
                  rows(MEM_Q_W), rows(N_BRANCHES * d), state, state, kv_spec, dmat_spec, xi_spec, xi_spec,
                  _resident(w_ret_o.shape), _resident(w_grp.shape), _resident(scale.shape),
                  _resident(w_pool_o.shape), _resident(w_mem_o.shape), _resident(w_out.shape)],
        out_specs=rows(d),
        out_shape=jax.ShapeDtypeStruct((t, d), F32),
        scratch_shapes=[pltpu.VMEM((ts, RET_V_W), BF16),
                        pltpu.VMEM((ts + 2 * halo, POOL_W), F32)],
        compiler_params=_params("arbitrary", "arbitrary"),
        name="mix",
    )(x2, qk, v, g, p, p, p, qm, gates, sf, sb, kv.reshape((-1,) + kv.shape[2:]), dmat, xi_f, xi_b,
      w_ret_o, w_grp, scale, w_pool_o, w_mem_o, w_out)


def _mlp_kernel(x_ref, gain_ref, w1_ref, w2_ref, final_gain_ref, out_ref, *, final_norm):
    x = x_ref[...]
    h = _rms_norm(x, gain_ref[...]).astype(BF16)
    acc = x
    for col in range(0, w1_ref.shape[1], COL_CHUNK):
        hid = jnp.maximum(_dot(h, w1_ref[:, col:col + COL_CHUNK]), 0.0)
        acc = acc + _dot((hid * hid).astype(BF16), w2_ref[col:col + COL_CHUNK, :])
    out_ref[...] = _rms_norm(acc, final_gain_ref[...]) if final_norm else acc


def _mlp(x2, gain, w1, w2, final_gain, final_norm):
    t, d = x2.shape
    tm = ROW_TILE
    rows = pl.BlockSpec((tm, d), lambda i: (i, 0))
    return pl.pallas_call(
        functools.partial(_mlp_kernel, final_norm=final_norm),
        grid=(t // tm,),
        in_specs=[rows, _resident((1, d)), _resident(w1.shape), _resident(w2.shape), _resident((1, d))],
        out_specs=rows,
        out_shape=jax.ShapeDtypeStruct((t, d), F32),
        compiler_params=_params("arbitrary"),
        name="mlp",
    )(x2, gain.reshape(1, d), w1, w2, final_gain.reshape(1, d))


def _rotary_tables(seq):
    inv = ROPE_BASE ** (-jnp.arange(0, RET_QK_DIM, 2, dtype=F32) / RET_QK_DIM)
    ang = jnp.arange(seq, dtype=F32)[:, None] * inv[None, :]
    cos, sin = jnp.cos(ang), jnp.sin(ang)
    return jnp.concatenate([cos, cos], axis=1), jnp.concatenate([-sin, sin], axis=1)


def kernel(x, mem, w_in, ret_decay_logit, w_ret_o, w_pool_grp, pool_scale, w_pool_o, w_mem_kv, w_mem_o,
           w_out, w_ff1, w_ff2, norm1_g, norm2_g, mem_norm_g, final_norm_g):
    batch, seq, d = x.shape
    depth = w_in.shape[0]
    assert seq % MIX_TILE == 0 and seq % ROW_TILE == 0 and MIX_TILE % RET_CHUNK == 0
    assert d % COL_CHUNK == 0 and POOL_HALO % 8 == 0

    cos, sin = _rotary_tables(seq)
    dmat, xi_f, xi_b, zeta_f, zeta_b, dec = _decay_tables(ret_decay_logit)
    kv = _mem_kv(mem, mem_norm_g, w_mem_kv.astype(BF16))

    w_in_b = w_in.astype(BF16)
    w_ret_o_b = w_ret_o.astype(BF16)
    w_grp_b = w_pool_grp.astype(BF16)
    w_pool_o_b = w_pool_o.astype(BF16)
    w_mem_o_b = w_mem_o.astype(BF16)
    w_out_b = w_out.astype(BF16)
    w_ff1_b = w_ff1.astype(BF16)
    w_ff2_b = w_ff2.astype(BF16)

    x2 = x.reshape(batch * seq, d)
    for l in range(depth):
        qk, v, g, p, qm, gates = _in_proj(x2, norm1_g[l], w_in_b[l], cos, sin, seq)
        sf, sb = _ret_states(qk, v, zeta_f, zeta_b, dec, l, batch, seq)
        x2 = _mix(x2, qk, v, g, p, qm, gates, sf, sb, kv, dmat, xi_f, xi_b, w_ret_o_b[l], w_grp_b[l],
                  pool_scale[l].reshape(1, POOL_W), w_pool_o_b[l], w_mem_o_b[l], w_out_b[l], l, batch, seq)
        x2 = _mlp(x2, norm2_g[l], w_ff1_b[l], w_ff2_b[l], final_norm_g, l == depth - 1)
    return x2.reshape(batch, seq, d)
```

```python
import functools

import jax
import jax.numpy as jnp
from jax import lax
from jax.experimental import pallas as pl
from jax.experimental.pallas import tpu as pltpu

F32 = jnp.float32
BF16 = jnp.bfloat16

RET_HEADS = 4
RET_QK_DIM = 128
RET_V_DIM = 256
MEM_HEADS = 4
MEM_HEAD_DIM = 128
POOL_WINDOWS = (2, 4, 8, 16)
POOL_GROUP = 128
N_BRANCHES = 3
ROPE_BASE = 10000.0
EPS = 1e-6
LOG2_E = 1.4426950408889634

RET_QK_W = RET_HEADS * RET_QK_DIM
RET_V_W = RET_HEADS * RET_V_DIM
MEM_Q_W = MEM_HEADS * MEM_HEAD_DIM
POOL_W = POOL_GROUP * len(POOL_WINDOWS)
POOL_HALO = max(POOL_WINDOWS) // 2

V7X_LANES = 128
V7X_VMEM_BYTES = 64 * 1024 * 1024
VMEM_LIMIT_BYTES = V7X_VMEM_BYTES - 8 * 1024 * 1024

RET_CHUNK = 256
ROW_TILE = 512
MIX_TILE = 512
COL_CHUNK = 512
IN_PROJ_CHUNK = 256


def _params(*semantics):
    return pltpu.CompilerParams(dimension_semantics=semantics, vmem_limit_bytes=VMEM_LIMIT_BYTES)


def _resident(shape):
    zeros = (0,) * len(shape)
    return pl.BlockSpec(shape, lambda *_: zeros, pipeline_mode=pl.Buffered(1))


def _rms_norm(x, gain):
    return x * lax.rsqrt(jnp.mean(x * x, axis=-1, keepdims=True) + EPS) * gain


def _sigmoid(x):
    return 1.0 / (1.0 + jnp.exp(-x))


def _dot(a, b):
    return jnp.dot(a, b, preferred_element_type=F32)


def _dot_nt(a, b):
    return lax.dot_general(a, b, (((1,), (1,)), ((), ())), preferred_element_type=F32)


def _dot_tn(a, b):
    return lax.dot_general(a, b, (((0,), (0,)), ((), ())), preferred_element_type=F32)


def _decay_tables_kernel(logit_ref, dmat_ref, xi_f_ref, xi_b_ref, zeta_f_ref, zeta_b_ref, dec_ref):
    c = RET_CHUNK
    logit = logit_ref[0]
    log_g = jnp.minimum(logit, 0.0) - jnp.log1p(jnp.exp(-jnp.abs(logit)))
    lg_f, lg_b = log_g[0], log_g[1]
    row = lax.broadcasted_iota(jnp.int32, (c, RET_QK_W), 0).astype(F32)
    xi_f_ref[0] = jnp.exp(lg_f * (row + 1.0))
    xi_b_ref[0] = jnp.exp(lg_b * (c - row))
    zeta_f_ref[0] = jnp.exp(lg_f * (c - 1.0 - row))
    zeta_b_ref[0] = jnp.exp(lg_b * row)
    dec_ref[0, 0] = jnp.exp(lg_f * c)
    dec_ref[0, 1] = jnp.exp(lg_b * c)
    i = lax.broadcasted_iota(jnp.int32, (c, c), 0)
    j = lax.broadcasted_iota(jnp.int32, (c, c), 1)
    diff = (i - j).astype(F32)
    for h in range(RET_HEADS):
        lf = lg_f[:, h * RET_QK_DIM:h * RET_QK_DIM + 1]
        lb = lg_b[:, h * RET_QK_DIM:h * RET_QK_DIM + 1]
        fwd = jnp.exp(lf * jnp.maximum(diff, 0.0))
        bwd = jnp.exp(lb * jnp.maximum(-diff, 0.0))
        dmat_ref[0, h] = jnp.where(diff >= 0.0, fwd, bwd)


def _decay_tables(ret_decay_logit):
    depth = ret_decay_logit.shape[0]
    c = RET_CHUNK
    logit = jnp.repeat(ret_decay_logit.astype(F32), RET_QK_DIM, axis=-1)[:, :, None, :]
    vec = jax.ShapeDtypeStruct((depth, c, RET_QK_W), F32)
    vec_spec = pl.BlockSpec((1, c, RET_QK_W), lambda l: (l, 0, 0))
    return pl.pallas_call(
        _decay_tables_kernel,
        grid=(depth,),
        in_specs=[pl.BlockSpec((1, 2, 1, RET_QK_W), lambda l: (l, 0, 0, 0))],
        out_specs=[pl.BlockSpec((1, RET_HEADS, c, c), lambda l: (l, 0, 0, 0)),
                   vec_spec, vec_spec, vec_spec, vec_spec,
                   pl.BlockSpec((1, 2, 1, RET_QK_W), lambda l: (l, 0, 0, 0))],
        out_shape=[jax.ShapeDtypeStruct((depth, RET_HEADS, c, c), F32), vec, vec, vec, vec,
                   jax.ShapeDtypeStruct((depth, 2, 1, RET_QK_W), F32)],
        compiler_params=_params("arbitrary"),
        name="decay_tables",
    )(logit)


def _mem_kv_kernel(mem_ref, gain_ref, w_ref, kv_ref):
    mem_n = _rms_norm(mem_ref[0], gain_ref[...]).astype(BF16)
    for l in range(w_ref.shape[0]):
        kv_ref[l, 0] = _dot(mem_n, w_ref[l]).astype(BF16)


def _mem_kv(mem, mem_norm_g, w_mem_kv):
    b, m, d = mem.shape
    depth, _, kvw = w_mem_kv.shape
    return pl.pallas_call(
        _mem_kv_kernel,
        grid=(b,),
        in_specs=[pl.BlockSpec((1, m, d), lambda i: (i, 0, 0)),
                  _resident((1, d)),
                  _resident((depth, d, kvw))],
        out_specs=pl.BlockSpec((depth, 1, m, kvw), lambda i: (0, i, 0, 0)),
        out_shape=jax.ShapeDtypeStruct((depth, b, m, kvw), BF16),
        compiler_params=_params("arbitrary"),
        name="mem_kv",
    )(mem, mem_norm_g.reshape(1, d), w_mem_kv)


def _in_proj_kernel(x_ref, gain_ref, w_ref, cos_ref, sin_ref,
                    qk_ref, v_ref, g_ref, p_ref, qm_ref, gates_ref):
    x = x_ref[...]
    h = (x * gain_ref[...]).astype(BF16)
    inv_rms = lax.rsqrt(jnp.mean(x * x, axis=-1, keepdims=True) + EPS)
    cos = cos_ref[...]
    sin = sin_ref[...]

    def rotary(a, scale):
        heads = []
        for hd in range(a.shape[1] // RET_QK_DIM):
            ah = a[:, hd * RET_QK_DIM:(hd + 1) * RET_QK_DIM]
            heads.append(ah * cos + pltpu.roll(ah, RET_QK_DIM // 2, axis=1) * sin)
        rotated = jnp.concatenate(heads, axis=1)
        return rotated if scale is None else rotated * scale

    groups = [
        (qk_ref, 0, RET_QK_W, lambda a: rotary(a, None)),
        (qk_ref, RET_QK_W, RET_QK_W, lambda a: rotary(a, RET_QK_DIM ** -0.5)),
        (v_ref, 0, RET_V_W, lambda a: a),
        (g_ref, 0, RET_V_W, lambda a: a * _sigmoid(a)),
        (p_ref, 0, POOL_W, lambda a: a),
        (qm_ref, 0, MEM_Q_W, lambda a: a),
        (gates_ref, 0, gates_ref.shape[1], _sigmoid),
    ]
    w_col = 0
    for out_ref, out_col, width, epilogue in groups:
        for c in range(0, width, IN_PROJ_CHUNK):
            a = _dot(h, w_ref[:, w_col + c:w_col + c + IN_PROJ_CHUNK]) * inv_rms
            out_ref[:, out_col + c:out_col + c + IN_PROJ_CHUNK] = epilogue(a).astype(out_ref.dtype)
        w_col += width


def _in_proj(x2, gain, w_in, cos, sin, seq):
    t, d = x2.shape
    tm = ROW_TILE
    gates_w = N_BRANCHES * d
    pos_tiles = seq // tm

    def rows(width):
        return pl.BlockSpec((tm, width), lambda i: (i, 0))

    pos_spec = pl.BlockSpec((tm, RET_QK_DIM), lambda i: (i % pos_tiles, 0))
    return pl.pallas_call(
        _in_proj_kernel,
        grid=(t // tm,),
        in_specs=[rows(d), _resident((1, d)), _resident(w_in.shape), pos_spec, pos_spec],
        out_specs=[rows(2 * RET_QK_W), rows(RET_V_W), rows(RET_V_W), rows(POOL_W), rows(MEM_Q_W),
                   rows(gates_w)],
        out_shape=[jax.ShapeDtypeStruct((t, 2 * RET_QK_W), BF16),
                   jax.ShapeDtypeStruct((t, RET_V_W), BF16),
                   jax.ShapeDtypeStruct((t, RET_V_W), BF16),
                   jax.ShapeDtypeStruct((t, POOL_W), F32),
                   jax.ShapeDtypeStruct((t, MEM_Q_W), BF16),
                   jax.ShapeDtypeStruct((t, gates_w), BF16)],
        compiler_params=_params("arbitrary"),
        name="in_proj",
    )(x2, gain.reshape(1, d), w_in, cos, sin)


def _ret_state_kernel(k_ref, v_ref, zeta_f_ref, zeta_b_ref, dec_ref, state_ref, acc):
    c = RET_CHUNK
    n = k_ref.shape[0] // c

    def scan(row0, zeta_ref, dec, order):
        acc[...] = jnp.zeros_like(acc)
        for ci in order:
            for h in range(RET_HEADS):
                state_ref[ci, h, row0:row0 + RET_QK_DIM, :] = acc[h].astype(BF16)
            if ci == order[-1]:
                break
            rows = slice(ci * c, (ci + 1) * c)
            kz = (k_ref[rows, :].astype(F32) * zeta_ref[0]).astype(BF16)
            for h in range(RET_HEADS):
                cols = slice(h * RET_QK_DIM, (h + 1) * RET_QK_DIM)
                outer = _dot_tn(kz[:, cols], v_ref[rows, h * RET_V_DIM:(h + 1) * RET_V_DIM])
                acc[h] = acc[h] * dec[:, h * RET_QK_DIM:h * RET_QK_DIM + 1] + outer

    scan(0, zeta_f_ref, dec_ref[0, 0], list(range(n)))
    scan(RET_QK_DIM, zeta_b_ref, dec_ref[0, 1], list(range(n - 1, -1, -1)))


def _ret_states(qk, v, zeta_f, zeta_b, dec, layer, batch, seq):
    c = RET_CHUNK
    n = seq // c
    table = pl.BlockSpec((1, c, RET_QK_W), lambda b: (layer, 0, 0), pipeline_mode=pl.Buffered(1))
    return pl.pallas_call(
        _ret_state_kernel,
        grid=(batch,),
        in_specs=[pl.BlockSpec((seq, RET_QK_W), lambda b: (b, 1)),
                  pl.BlockSpec((seq, RET_V_W), lambda b: (b, 0)),
                  table, table,
                  pl.BlockSpec((1, 2, 1, RET_QK_W), lambda b: (layer, 0, 0, 0), pipeline_mode=pl.Buffered(1))],
        out_specs=pl.BlockSpec((n, RET_HEADS, 2 * RET_QK_DIM, RET_V_DIM), lambda b: (b, 0, 0, 0)),
        out_shape=jax.ShapeDtypeStruct((batch * n, RET_HEADS, 2 * RET_QK_DIM, RET_V_DIM), BF16),
        scratch_shapes=[pltpu.VMEM((RET_HEADS, RET_QK_DIM, RET_V_DIM), F32)],
        compiler_params=_params("arbitrary"),
        name="ret_states",
    )(qk, v, zeta_f, zeta_b, dec)


def _window_sums(padded):
    length = padded.shape[0]
    halo = POOL_HALO
    ts = length - 2 * halo

    def ahead(a, k):
        return pltpu.roll(a, length - k, axis=0)

    def behind(a, k):
        return pltpu.roll(a, k, axis=0)

    sums = []
    for gi, w in enumerate(POOL_WINDOWS):
        a = padded[:, gi * POOL_GROUP:(gi + 1) * POOL_GROUP]
        span = 1
        while 2 * span < w:
            a = a + ahead(a, span)
            span *= 2
        assert 2 * span == w and span <= halo
        sums.append((a + behind(a, span))[halo:halo + ts, :])
    return sums


def _mix_kernel(x_ref, qk_ref, v_ref, g_ref, p_ref, p_prev_ref, p_next_ref, inv_count_ref, qm_ref, gates_ref,
                state_ref, kv_ref, dmat_ref, xi_f_ref, xi_b_ref,
                w_ret_o_ref, w_grp_ref, scale_ref, w_pool_o_ref, w_mem_o_ref, w_out_ref,
                out_ref, ret_scr):
    ts = x_ref.shape[0]
    d = x_ref.shape[1]
    c = RET_CHUNK
    tile = pl.program_id(1)
    n_tiles = pl.num_programs(1)

    xi_f = xi_f_ref[0]
    xi_b = xi_b_ref[0]
    for ci in range(ts // c):
        rows = slice(ci * c, (ci + 1) * c)
        q = qk_ref[rows, :RET_QK_W]
        q32 = q.astype(F32)
        q_f = (q32 * xi_f).astype(BF16)
        q_b = (q32 * xi_b).astype(BF16)
        for h in range(RET_HEADS):
            qk_cols = slice(h * RET_QK_DIM, (h + 1) * RET_QK_DIM)
            v_cols = slice(h * RET_V_DIM, (h + 1) * RET_V_DIM)
            k_h = qk_ref[rows, RET_QK_W + h * RET_QK_DIM:RET_QK_W + (h + 1) * RET_QK_DIM]
            s = _dot_nt(q[:, qk_cols], k_h) * dmat_ref[0, h]
            q_fb = jnp.concatenate([q_f[:, qk_cols], q_b[:, qk_cols]], axis=1)
            o = _dot(s.astype(BF16), v_ref[rows, v_cols]) + _dot(q_fb, state_ref[ci, h])
            mu = jnp.mean(o, axis=-1, keepdims=True)
            cen = o - mu
            var = jnp.mean(cen * cen, axis=-1, keepdims=True)
            o_n = cen * lax.rsqrt(var + EPS)
            ret_scr[rows, v_cols] = (o_n * g_ref[rows, v_cols].astype(F32)).astype(BF16)
    o_ret = _dot(ret_scr[...], w_ret_o_ref[...]).astype(BF16)

    p = p_ref[...]
    padded = jnp.concatenate([jnp.where(tile > 0, p_prev_ref[...], 0.0), p,
                              jnp.where(tile < n_tiles - 1, p_next_ref[...], 0.0)], axis=0)
    groups = []
    for gi, win in enumerate(_window_sums(padded)):
        cols = slice(gi * POOL_GROUP, (gi + 1) * POOL_GROUP)
        mixed = win * inv_count_ref[:, cols] - p[:, cols]
        y = _dot(mixed.astype(BF16), w_grp_ref[gi]) * scale_ref[:, cols]
        groups.append(y.astype(BF16))
    o_pool = _dot(jnp.concatenate(groups, axis=1), w_pool_o_ref[...]).astype(BF16)

    exp2_scale = (MEM_HEAD_DIM ** -0.5) * LOG2_E
    heads = []
    for h in range(MEM_HEADS):
        cols = slice(h * MEM_HEAD_DIM, (h + 1) * MEM_HEAD_DIM)
        k_h = kv_ref[0, :, cols]
        v_h = kv_ref[0, :, MEM_Q_W + h * MEM_HEAD_DIM:MEM_Q_W + (h + 1) * MEM_HEAD_DIM]
        s = _dot_nt(qm_ref[:, cols], k_h)
        e = jnp.exp2((s - jnp.max(s, axis=-1, keepdims=True)) * exp2_scale)
        o = _dot(e.astype(BF16), v_h) / jnp.sum(e, axis=-1, keepdims=True)
        heads.append(o.astype(BF16))
    o_mem = _dot(jnp.concatenate(heads, axis=1), w_mem_o_ref[...]).astype(BF16)

    merged = gates_ref[:, :d] * o_ret + gates_ref[:, d:2 * d] * o_pool + gates_ref[:, 2 * d:] * o_mem
    out_ref[...] = x_ref[...] + _dot(merged, w_out_ref[...])


def _mix(x2, qk, v, g, p, inv_count, qm, gates, states, kv, dmat, xi_f, xi_b, w_ret_o, w_grp, scale,
         w_pool_o, w_mem_o, w_out, layer, batch, seq):
    t, d = x2.shape
    ts = MIX_TILE
    c = RET_CHUNK
    nt = seq // ts
    halo = POOL_HALO
    halo_per_tile = ts // halo
    last_halo = t // halo - 1

    def rows(width):
        return pl.BlockSpec((ts, width), lambda b, i: (b * nt + i, 0))

    p_prev = pl.BlockSpec((halo, POOL_W), lambda b, i: (jnp.maximum((b * nt + i) * halo_per_tile - 1, 0), 0))
    p_next = pl.BlockSpec((halo, POOL_W),
                          lambda b, i: (jnp.minimum((b * nt + i + 1) * halo_per_tile, last_halo), 0))
    state = pl.BlockSpec((ts // c,) + states.shape[1:], lambda b, i: (b * nt + i, 0, 0, 0))
    kv_spec = pl.BlockSpec((1,) + kv.shape[2:], lambda b, i: (layer * batch + b, 0, 0))
    dmat_spec = pl.BlockSpec((1, RET_HEADS, c, c), lambda b, i: (layer, 0, 0, 0), pipeline_mode=pl.Buffered(1))
    xi_spec = pl.BlockSpec((1, c, RET_QK_W), lambda b, i: (layer, 0, 0), pipeline_mode=pl.Buffered(1))
    return pl.pallas_call(
        _mix_kernel,
        grid=(batch, nt),
        in_specs=[rows(d), rows(2 * RET_QK_W), rows(RET_V_W), rows(RET_V_W), rows(POOL_W), p_prev, p_next,
                  pl.BlockSpec((ts, POOL_W), lambda b, i: (i, 0)),
                  rows(MEM_Q_W), rows(N_BRANCHES * d), state, kv_spec, dmat_spec, xi_spec, xi_spec,
                  _resident(w_ret_o.shape), _resident(w_grp.shape), _resident(scale.shape),
                  _resident(w_pool_o.shape), _resident(w_mem_o.shape), _resident(w_out.shape)],
        out_specs=rows(d),
        out_shape=jax.ShapeDtypeStruct((t, d), F32),
        scratch_shapes=[pltpu.VMEM((ts, RET_V_W), BF16)],
        compiler_params=_params("arbitrary", "arbitrary"),
        name="mix",
    )(x2, qk, v, g, p, p, p, inv_count, qm, gates, states, kv.reshape((-1,) + kv.shape[2:]), dmat, xi_f, xi_b,
      w_ret_o, w_grp, scale, w_pool_o, w_mem_o, w_out)


def _mlp_kernel(x_ref, gain_ref, w1_ref, w2_ref, final_gain_ref, out_ref, *, final_norm):
    x = x_ref[...]
    h = _rms_norm(x, gain_ref[...]).astype(BF16)
    acc = x
    for col in range(0, w1_ref.shape[1], COL_CHUNK):
        hid = jnp.maximum(_dot(h, w1_ref[:, col:col + COL_CHUNK]), 0.0)
        acc = acc + _dot((hid * hid).astype(BF16), w2_ref[col:col + COL_CHUNK, :])
    out_ref[...] = _rms_norm(acc, final_gain_ref[...]) if final_norm else acc


def _mlp(x2, gain, w1, w2, final_gain, final_norm):
    t, d = x2.shape
    tm = ROW_TILE
    rows = pl.BlockSpec((tm, d), lambda i: (i, 0))
    return pl.pallas_call(
        functools.partial(_mlp_kernel, final_norm=final_norm),
        grid=(t // tm,),
        in_specs=[rows, _resident((1, d)), _resident(w1.shape), _resident(w2.shape), _resident((1, d))],
        out_specs=rows,
        out_shape=jax.ShapeDtypeStruct((t, d), F32),
        compiler_params=_params("arbitrary"),
        name="mlp",
    )(x2, gain.reshape(1, d), w1, w2, final_gain.reshape(1, d))


def _rotary_tables(seq):
    inv = ROPE_BASE ** (-jnp.arange(0, RET_QK_DIM, 2, dtype=F32) / RET_QK_DIM)
    ang = jnp.arange(seq, dtype=F32)[:, None] * inv[None, :]
    cos, sin = jnp.cos(ang), jnp.sin(ang)
    return jnp.concatenate([cos, cos], axis=1), jnp.concatenate([-sin, sin], axis=1)


def _pool_inv_counts(seq):
    pos = jnp.arange(seq)
    cols = []
    for w in POOL_WINDOWS:
        count = jnp.minimum(pos + w // 2, seq) - jnp.maximum(pos - w // 2, 0)
        cols.append(jnp.broadcast_to((1.0 / count.astype(F32))[:, None], (seq, POOL_GROUP)))
    return jnp.concatenate(cols, axis=1)


def kernel(x, mem, w_in, ret_decay_logit, w_ret_o, w_pool_grp, pool_scale, w_pool_o, w_mem_kv, w_mem_o,
           w_out, w_ff1, w_ff2, norm1_g, norm2_g, mem_norm_g, final_norm_g):
    batch, seq, d = x.shape
    depth = w_in.shape[0]
    assert seq % MIX_TILE == 0 and seq % ROW_TILE == 0 and MIX_TILE % RET_CHUNK == 0
    assert d % COL_CHUNK == 0 and POOL_HALO % 8 == 0

    cos, sin = _rotary_tables(seq)
    inv_count = _pool_inv_counts(seq)
    dmat, xi_f, xi_b, zeta_f, zeta_b, dec = _decay_tables(ret_decay_logit)
    kv = _mem_kv(mem, mem_norm_g, w_mem_kv.astype(BF16))

    w_in_b = w_in.astype(BF16)
    w_ret_o_b = w_ret_o.astype(BF16)
    w_grp_b = w_pool_grp.astype(BF16)
    w_pool_o_b = w_pool_o.astype(BF16)
    w_mem_o_b = w_mem_o.astype(BF16)
    w_out_b = w_out.astype(BF16)
    w_ff1_b = w_ff1.astype(BF16)
    w_ff2_b = w_ff2.astype(BF16)

    x2 = x.reshape(batch * seq, d)
    for l in range(depth):
        qk, v, g, p, qm, gates = _in_proj(x2, norm1_g[l], w_in_b[l], cos, sin, seq)
        states = _ret_states(qk, v, zeta_f, zeta_b, dec, l, batch, seq)
        x2 = _mix(x2, qk, v, g, p, inv_count, qm, gates, states, kv, dmat, xi_f, xi_b, w_ret_o_b[l], w_grp_b[l],
                  pool_scale[l].reshape(1, POOL_W), w_pool_o_b[l], w_mem_o_b[l], w_out_b[l], l, batch, seq)
        x2 = _mlp(x2, norm2_g[l], w_ff1_b[l], w_ff2_b[l], final_norm_g, l == depth - 1)
    return x2.reshape(batch, seq, d)
```

```python
import functools

import jax
import jax.numpy as jnp
from jax import lax
from jax.experimental import pallas as pl
from jax.experimental.pallas import tpu as pltpu

F32 = jnp.float32
BF16 = jnp.bfloat16

RET_HEADS = 4
RET_QK_DIM = 128
RET_V_DIM = 256
MEM_HEADS = 4
MEM_HEAD_DIM = 128
POOL_WINDOWS = (2, 4, 8, 16)
POOL_GROUP = 128
N_BRANCHES = 3
ROPE_BASE = 10000.0
EPS = 1e-6
LOG2_E = 1.4426950408889634

RET_QK_W = RET_HEADS * RET_QK_DIM
RET_V_W = RET_HEADS * RET_V_DIM
MEM_Q_W = MEM_HEADS * MEM_HEAD_DIM
POOL_W = POOL_GROUP * len(POOL_WINDOWS)
POOL_HALO = max(POOL_WINDOWS) // 2

V7X_LANES = 128
V7X_VMEM_BYTES = 64 * 1024 * 1024
VMEM_LIMIT_BYTES = V7X_VMEM_BYTES - 8 * 1024 * 1024

RET_CHUNK = 256
ROW_TILE = 512
MLP_TILE = 1024
MIX_TILE = 512
COL_CHUNK = 512
IN_PROJ_CHUNK = 256


def _params(*semantics):
    return pltpu.CompilerParams(dimension_semantics=semantics, vmem_limit_bytes=VMEM_LIMIT_BYTES)


def _resident(shape):
    zeros = (0,) * len(shape)
    return pl.BlockSpec(shape, lambda *_: zeros, pipeline_mode=pl.Buffered(1))


def _layer_resident(stacked_shape, layer):
    index = (layer,) + (0,) * (len(stacked_shape) - 1)
    return pl.BlockSpec((None,) + tuple(stacked_shape[1:]), lambda *_: index, pipeline_mode=pl.Buffered(1))


def _rms_norm(x, gain):
    return x * lax.rsqrt(jnp.mean(x * x, axis=-1, keepdims=True) + EPS) * gain


def _sigmoid(x):
    return 1.0 / (1.0 + jnp.exp(-x))


def _dot(a, b):
    return jnp.dot(a, b, preferred_element_type=F32)


def _dot_nt(a, b):
    return lax.dot_general(a, b, (((1,), (1,)), ((), ())), preferred_element_type=F32)


def _dot_tn(a, b):
    return lax.dot_general(a, b, (((0,), (0,)), ((), ())), preferred_element_type=F32)


def _decay_tables_kernel(logit_ref, dmat_ref, xi_f_ref, xi_b_ref, zeta_f_ref, zeta_b_ref, dec_ref):
    c = RET_CHUNK
    logit = logit_ref[0]
    log_g = jnp.minimum(logit, 0.0) - jnp.log1p(jnp.exp(-jnp.abs(logit)))
    lg_f, lg_b = log_g[0], log_g[1]
    row = lax.broadcasted_iota(jnp.int32, (c, RET_QK_W), 0).astype(F32)
    xi_f_ref[0] = jnp.exp(lg_f * (row + 1.0))
    xi_b_ref[0] = jnp.exp(lg_b * (c - row))
    zeta_f_ref[0] = jnp.exp(lg_f * (c - 1.0 - row))
    zeta_b_ref[0] = jnp.exp(lg_b * row)
    dec_ref[0, 0] = jnp.exp(lg_f * c)
    dec_ref[0, 1] = jnp.exp(lg_b * c)
    i = lax.broadcasted_iota(jnp.int32, (c, c), 0)
    j = lax.broadcasted_iota(jnp.int32, (c, c), 1)
    diff = (i - j).astype(F32)
    for h in range(RET_HEADS):
        lf = lg_f[:, h * RET_QK_DIM:h * RET_QK_DIM + 1]
        lb = lg_b[:, h * RET_QK_DIM:h * RET_QK_DIM + 1]
        fwd = jnp.exp(lf * jnp.maximum(diff, 0.0))
        bwd = jnp.exp(lb * jnp.maximum(-diff, 0.0))
        dmat_ref[0, h] = jnp.where(diff >= 0.0, fwd, bwd)


def _decay_tables(ret_decay_logit):
    depth = ret_decay_logit.shape[0]
    c = RET_CHUNK
    logit = jnp.repeat(ret_decay_logit.astype(F32), RET_QK_DIM, axis=-1)[:, :, None, :]
    vec = jax.ShapeDtypeStruct((depth, c, RET_QK_W), F32)
    vec_spec = pl.BlockSpec((1, c, RET_QK_W), lambda l: (l, 0, 0))
    return pl.pallas_call(
        _decay_tables_kernel,
        grid=(depth,),
        in_specs=[pl.BlockSpec((1, 2, 1, RET_QK_W), lambda l: (l, 0, 0, 0))],
        out_specs=[pl.BlockSpec((1, RET_HEADS, c, c), lambda l: (l, 0, 0, 0)),
                   vec_spec, vec_spec, vec_spec, vec_spec,
                   pl.BlockSpec((1, 2, 1, RET_QK_W), lambda l: (l, 0, 0, 0))],
        out_shape=[jax.ShapeDtypeStruct((depth, RET_HEADS, c, c), F32), vec, vec, vec, vec,
                   jax.ShapeDtypeStruct((depth, 2, 1, RET_QK_W), F32)],
        compiler_params=_params("arbitrary"),
        name="decay_tables",
    )(logit)


def _mem_kv_kernel(mem_ref, gain_ref, w_ref, kv_ref):
    mem_n = _rms_norm(mem_ref[0], gain_ref[...]).astype(BF16)
    for l in range(w_ref.shape[0]):
        kv_ref[l, 0] = _dot(mem_n, w_ref[l]).astype(BF16)


def _mem_kv(mem, mem_norm_g, w_mem_kv):
    b, m, d = mem.shape
    depth, _, kvw = w_mem_kv.shape
    return pl.pallas_call(
        _mem_kv_kernel,
        grid=(b,),
        in_specs=[pl.BlockSpec((1, m, d), lambda i: (i, 0, 0)),
                  _resident((1, d)),
                  _resident((depth, d, kvw))],
        out_specs=pl.BlockSpec((depth, 1, m, kvw), lambda i: (0, i, 0, 0)),
        out_shape=jax.ShapeDtypeStruct((depth, b, m, kvw), BF16),
        compiler_params=_params("arbitrary"),
        name="mem_kv",
    )(mem, mem_norm_g.reshape(1, d), w_mem_kv)


def _in_proj_kernel(x_ref, gain_ref, w_ref, cos_ref, sin_ref,
                    qk_ref, v_ref, g_ref, p_ref, qm_ref, gates_ref):
    x = x_ref[...]
    h = (x * gain_ref[...]).astype(BF16)
    inv_rms = lax.rsqrt(jnp.mean(x * x, axis=-1, keepdims=True) + EPS)
    cos = cos_ref[...]
    sin = sin_ref[...]

    def rotary(a, scale):
        heads = []
        for hd in range(a.shape[1] // RET_QK_DIM):
            ah = a[:, hd * RET_QK_DIM:(hd + 1) * RET_QK_DIM]
            heads.append(ah * cos + pltpu.roll(ah, RET_QK_DIM // 2, axis=1) * sin)
        rotated = jnp.concatenate(heads, axis=1)
        return rotated if scale is None else rotated * scale

    groups = [
        (qk_ref, 0, RET_QK_W, lambda a: rotary(a, None)),
        (qk_ref, RET_QK_W, RET_QK_W, lambda a: rotary(a, RET_QK_DIM ** -0.5)),
        (v_ref, 0, RET_V_W, lambda a: a),
        (g_ref, 0, RET_V_W, lambda a: a * _sigmoid(a)),
        (p_ref, 0, POOL_W, lambda a: a),
        (qm_ref, 0, MEM_Q_W, lambda a: a),
        (gates_ref, 0, gates_ref.shape[1], _sigmoid),
    ]
    w_col = 0
    for out_ref, out_col, width, epilogue in groups:
        for c in range(0, width, IN_PROJ_CHUNK):
            a = _dot(h, w_ref[:, w_col + c:w_col + c + IN_PROJ_CHUNK]) * inv_rms
            out_ref[:, out_col + c:out_col + c + IN_PROJ_CHUNK] = epilogue(a).astype(out_ref.dtype)
        w_col += width


def _in_proj(x2, gains, w_in, cos, sin, layer, seq):
    t, d = x2.shape
    tm = ROW_TILE
    gates_w = N_BRANCHES * d
    pos_tiles = seq // tm

    def rows(width):
        return pl.BlockSpec((tm, width), lambda i: (i, 0))

    pos_spec = pl.BlockSpec((tm, RET_QK_DIM), lambda i: (i % pos_tiles, 0))
    return pl.pallas_call(
        _in_proj_kernel,
        grid=(t // tm,),
        in_specs=[rows(d), _layer_resident(gains.shape, layer), _layer_resident(w_in.shape, layer),
                  pos_spec, pos_spec],
        out_specs=[rows(2 * RET_QK_W), rows(RET_V_W), rows(RET_V_W), rows(POOL_W), rows(MEM_Q_W),
                   rows(gates_w)],
        out_shape=[jax.ShapeDtypeStruct((t, 2 * RET_QK_W), BF16),
                   jax.ShapeDtypeStruct((t, RET_V_W), BF16),
                   jax.ShapeDtypeStruct((t, RET_V_W), BF16),
                   jax.ShapeDtypeStruct((t, POOL_W), F32),
                   jax.ShapeDtypeStruct((t, MEM_Q_W), BF16),
                   jax.ShapeDtypeStruct((t, gates_w), BF16)],
        compiler_params=_params("arbitrary"),
        name="in_proj",
    )(x2, gains, w_in, cos, sin)


def _ret_state_kernel(k_ref, v_ref, zeta_f_ref, zeta_b_ref, dec_ref, state_ref, acc):
    c = RET_CHUNK
    n = k_ref.shape[0] // c

    def scan(row0, zeta_ref, dec, order):
        acc[...] = jnp.zeros_like(acc)
        for ci in order:
            for h in range(RET_HEADS):
                state_ref[ci, h, row0:row0 + RET_QK_DIM, :] = acc[h].astype(BF16)
            if ci == order[-1]:
                break
            rows = slice(ci * c, (ci + 1) * c)
            kz = (k_ref[rows, :].astype(F32) * zeta_ref[0]).astype(BF16)
            for h in range(RET_HEADS):
                cols = slice(h * RET_QK_DIM, (h + 1) * RET_QK_DIM)
                outer = _dot_tn(kz[:, cols], v_ref[rows, h * RET_V_DIM:(h + 1) * RET_V_DIM])
                acc[h] = acc[h] * dec[:, h * RET_QK_DIM:h * RET_QK_DIM + 1] + outer

    scan(0, zeta_f_ref, dec_ref[0, 0], list(range(n)))
    scan(RET_QK_DIM, zeta_b_ref, dec_ref[0, 1], list(range(n - 1, -1, -1)))


def _ret_states(qk, v, zeta_f, zeta_b, dec, layer, batch, seq):
    c = RET_CHUNK
    n = seq // c
    table = pl.BlockSpec((1, c, RET_QK_W), lambda b: (layer, 0, 0), pipeline_mode=pl.Buffered(1))
    return pl.pallas_call(
        _ret_state_kernel,
        grid=(batch,),
        in_specs=[pl.BlockSpec((seq, RET_QK_W), lambda b: (b, 1)),
                  pl.BlockSpec((seq, RET_V_W), lambda b: (b, 0)),
                  table, table,
                  pl.BlockSpec((1, 2, 1, RET_QK_W), lambda b: (layer, 0, 0, 0), pipeline_mode=pl.Buffered(1))],
        out_specs=pl.BlockSpec((n, RET_HEADS, 2 * RET_QK_DIM, RET_V_DIM), lambda b: (b, 0, 0, 0)),
        out_shape=jax.ShapeDtypeStruct((batch * n, RET_HEADS, 2 * RET_QK_DIM, RET_V_DIM), BF16),
        scratch_shapes=[pltpu.VMEM((RET_HEADS, RET_QK_DIM, RET_V_DIM), F32)],
        compiler_params=_params("arbitrary"),
        name="ret_states",
    )(qk, v, zeta_f, zeta_b, dec)


def _window_sums(padded):
    length = padded.shape[0]
    halo = POOL_HALO
    ts = length - 2 * halo

    def ahead(a, k):
        return pltpu.roll(a, length - k, axis=0)

    def behind(a, k):
        return pltpu.roll(a, k, axis=0)

    sums = []
    for gi, w in enumerate(POOL_WINDOWS):
        a = padded[:, gi * POOL_GROUP:(gi + 1) * POOL_GROUP]
        span = 1
        while 2 * span < w:
            a = a + ahead(a, span)
            span *= 2
        assert 2 * span == w and span <= halo
        sums.append((a + behind(a, span))[halo:halo + ts, :])
    return sums


def _mix_kernel(x_ref, qk_ref, v_ref, g_ref, p_ref, p_prev_ref, p_next_ref, inv_count_ref, qm_ref, gates_ref,
                state_ref, kv_ref, dmat_ref, xi_f_ref, xi_b_ref,
                w_ret_o_ref, w_pool_ref, w_mem_o_ref, w_out_ref,
                out_ref, ret_scr):
    ts = x_ref.shape[0]
    d = x_ref.shape[1]
    c = RET_CHUNK
    tile = pl.program_id(1)
    n_tiles = pl.num_programs(1)

    xi_f = xi_f_ref[0]
    xi_b = xi_b_ref[0]
    for ci in range(ts // c):
        rows = slice(ci * c, (ci + 1) * c)
        q = qk_ref[rows, :RET_QK_W]
        q32 = q.astype(F32)
        q_f = (q32 * xi_f).astype(BF16)
        q_b = (q32 * xi_b).astype(BF16)
        for h in range(RET_HEADS):
            qk_cols = slice(h * RET_QK_DIM, (h + 1) * RET_QK_DIM)
            v_cols = slice(h * RET_V_DIM, (h + 1) * RET_V_DIM)
            k_h = qk_ref[rows, RET_QK_W + h * RET_QK_DIM:RET_QK_W + (h + 1) * RET_QK_DIM]
            s = _dot_nt(q[:, qk_cols], k_h) * dmat_ref[0, h]
            q_fb = jnp.concatenate([q_f[:, qk_cols], q_b[:, qk_cols]], axis=1)
            o = _dot(s.astype(BF16), v_ref[rows, v_cols]) + _dot(q_fb, state_ref[ci, h])
            mu = jnp.mean(o, axis=-1, keepdims=True)
            cen = o - mu
            var = jnp.mean(cen * cen, axis=-1, keepdims=True)
            o_n = cen * lax.rsqrt(var + EPS)
            ret_scr[rows, v_cols] = (o_n * g_ref[rows, v_cols].astype(F32)).astype(BF16)
    o_ret = _dot(ret_scr[...], w_ret_o_ref[...]).astype(BF16)

    p = p_ref[...]
    padded = jnp.concatenate([jnp.where(tile > 0, p_prev_ref[...], 0.0), p,
                              jnp.where(tile < n_tiles - 1, p_next_ref[...], 0.0)], axis=0)
    groups = []
    for gi, win in enumerate(_window_sums(padded)):
        cols = slice(gi * POOL_GROUP, (gi + 1) * POOL_GROUP)
        groups.append((win * inv_count_ref[:, cols] - p[:, cols]).astype(BF16))
    o_pool = _dot(jnp.concatenate(groups, axis=1), w_pool_ref[...]).astype(BF16)

    exp2_scale = (MEM_HEAD_DIM ** -0.5) * LOG2_E
    heads = []
    for h in range(MEM_HEADS):
        cols = slice(h * MEM_HEAD_DIM, (h + 1) * MEM_HEAD_DIM)
        k_h = kv_ref[0, :, cols]
        v_h = kv_ref[0, :, MEM_Q_W + h * MEM_HEAD_DIM:MEM_Q_W + (h + 1) * MEM_HEAD_DIM]
        s = _dot_nt(qm_ref[:, cols], k_h)
        e = jnp.exp2((s - jnp.max(s, axis=-1, keepdims=True)) * exp2_scale)
        o = _dot(e.astype(BF16), v_h) / jnp.sum(e, axis=-1, keepdims=True)
        heads.append(o.astype(BF16))
    o_mem = _dot(jnp.concatenate(heads, axis=1), w_mem_o_ref[...]).astype(BF16)

    merged = gates_ref[:, :d] * o_ret + gates_ref[:, d:2 * d] * o_pool + gates_ref[:, 2 * d:] * o_mem
    out_ref[...] = x_ref[...] + _dot(merged, w_out_ref[...])


def _mix(x2, qk, v, g, p, inv_count, qm, gates, states, kv, dmat, xi_f, xi_b, w_ret_o, w_pool, w_mem_o, w_out,
         layer, batch, seq):
    t, d = x2.shape
    ts = MIX_TILE
    c = RET_CHUNK
    nt = seq // ts
    halo = POOL_HALO
    halo_per_tile = ts // halo
    last_halo = t // halo - 1

    def rows(width):
        return pl.BlockSpec((ts, width), lambda b, i: (b * nt + i, 0))

    p_prev = pl.BlockSpec((halo, POOL_W), lambda b, i: (jnp.maximum((b * nt + i) * halo_per_tile - 1, 0), 0))
    p_next = pl.BlockSpec((halo, POOL_W),
                          lambda b, i: (jnp.minimum((b * nt + i + 1) * halo_per_tile, last_halo), 0))
    state = pl.BlockSpec((ts // c,) + states.shape[1:], lambda b, i: (b * nt + i, 0, 0, 0))
    kv_spec = pl.BlockSpec((1,) + kv.shape[2:], lambda b, i: (layer * batch + b, 0, 0))
    dmat_spec = pl.BlockSpec((1, RET_HEADS, c, c), lambda b, i: (layer, 0, 0, 0), pipeline_mode=pl.Buffered(1))
    xi_spec = pl.BlockSpec((1, c, RET_QK_W), lambda b, i: (layer, 0, 0), pipeline_mode=pl.Buffered(1))
    return pl.pallas_call(
        _mix_kernel,
        grid=(batch, nt),
        in_specs=[rows(d), rows(2 * RET_QK_W), rows(RET_V_W), rows(RET_V_W), rows(POOL_W), p_prev, p_next,
                  pl.BlockSpec((ts, POOL_W), lambda b, i: (i, 0)),
                  rows(MEM_Q_W), rows(N_BRANCHES * d), state, kv_spec, dmat_spec, xi_spec, xi_spec,
                  _layer_resident(w_ret_o.shape, layer), _layer_resident(w_pool.shape, layer),
                  _layer_resident(w_mem_o.shape, layer), _layer_resident(w_out.shape, layer)],
        out_specs=rows(d),
        out_shape=jax.ShapeDtypeStruct((t, d), F32),
        scratch_shapes=[pltpu.VMEM((ts, RET_V_W), BF16)],
        compiler_params=_params("arbitrary", "arbitrary"),
        name="mix",
    )(x2, qk, v, g, p, p, p, inv_count, qm, gates, states, kv.reshape((-1,) + kv.shape[2:]), dmat, xi_f, xi_b,
      w_ret_o, w_pool, w_mem_o, w_out)


def _mlp_kernel(x_ref, gain_ref, w1_ref, w2_ref, final_gain_ref, out_ref, *, final_norm):
    x = x_ref[...]
    h = (x * gain_ref[...]).astype(BF16)
    acc = None
    for col in range(0, w1_ref.shape[1], COL_CHUNK):
        hid = jnp.maximum(_dot(h, w1_ref[:, col:col + COL_CHUNK]), 0.0)
        part = _dot((hid * hid).astype(BF16), w2_ref[col:col + COL_CHUNK, :])
        acc = part if acc is None else acc + part
    out = x + acc / (jnp.mean(x * x, axis=-1, keepdims=True) + EPS)
    out_ref[...] = _rms_norm(out, final_gain_ref[...]) if final_norm else out


def _mlp(x2, gains, w1, w2, final_gain, layer, final_norm):
    t, d = x2.shape
    tm = MLP_TILE
    rows = pl.BlockSpec((tm, d), lambda i: (i, 0))
    return pl.pallas_call(
        functools.partial(_mlp_kernel, final_norm=final_norm),
        grid=(t // tm,),
        in_specs=[rows, _layer_resident(gains.shape, layer), _layer_resident(w1.shape, layer),
                  _layer_resident(w2.shape, layer), _resident((1, d))],
        out_specs=rows,
        out_shape=jax.ShapeDtypeStruct((t, d), F32),
        compiler_params=_params("arbitrary"),
        name="mlp",
    )(x2, gains, w1, w2, final_gain.reshape(1, d))


def _pool_weight_kernel(w_grp_ref, scale_ref, w_o_ref, out_ref):
    for gi in range(len(POOL_WINDOWS)):
        rows = slice(gi * POOL_GROUP, (gi + 1) * POOL_GROUP)
        scaled = w_grp_ref[0, gi] * scale_ref[0, :, rows]
        out_ref[0, rows, :] = jnp.dot(scaled, w_o_ref[0, rows, :], preferred_element_type=F32,
                                      precision=lax.Precision.HIGHEST).astype(BF16)


def _pool_weights(w_pool_grp, pool_scale, w_pool_o):
    depth, groups, group_w, _ = w_pool_grp.shape
    d = w_pool_o.shape[2]
    return pl.pallas_call(
        _pool_weight_kernel,
        grid=(depth,),
        in_specs=[pl.BlockSpec((1, groups, group_w, group_w), lambda l: (l, 0, 0, 0)),
                  pl.BlockSpec((1, 1, POOL_W), lambda l: (l, 0, 0)),
                  pl.BlockSpec((1, POOL_W, d), lambda l: (l, 0, 0))],
        out_specs=pl.BlockSpec((1, POOL_W, d), lambda l: (l, 0, 0)),
        out_shape=jax.ShapeDtypeStruct((depth, POOL_W, d), BF16),
        compiler_params=_params("arbitrary"),
        name="pool_weights",
    )(w_pool_grp, pool_scale.reshape(depth, 1, POOL_W), w_pool_o)


def _rotary_tables(seq):
    inv = ROPE_BASE ** (-jnp.arange(0, RET_QK_DIM, 2, dtype=F32) / RET_QK_DIM)
    ang = jnp.arange(seq, dtype=F32)[:, None] * inv[None, :]
    cos, sin = jnp.cos(ang), jnp.sin(ang)
    return jnp.concatenate([cos, cos], axis=1), jnp.concatenate([-sin, sin], axis=1)


def _pool_inv_counts(seq):
    pos = jnp.arange(seq)
    cols = []
    for w in POOL_WINDOWS:
        count = jnp.minimum(pos + w // 2, seq) - jnp.maximum(pos - w // 2, 0)
        cols.append(jnp.broadcast_to((1.0 / count.astype(F32))[:, None], (seq, POOL_GROUP)))
    return jnp.concatenate(cols, axis=1)


def kernel(x, mem, w_in, ret_decay_logit, w_ret_o, w_pool_grp, pool_scale, w_pool_o, w_mem_kv, w_mem_o,
           w_out, w_ff1, w_ff2, norm1_g, norm2_g, mem_norm_g, final_norm_g):
    batch, seq, d = x.shape
    depth = w_in.shape[0]
    assert seq % MIX_TILE == 0 and seq % ROW_TILE == 0 and MIX_TILE % RET_CHUNK == 0
    assert d % COL_CHUNK == 0 and POOL_HALO % 8 == 0

    cos, sin = _rotary_tables(seq)
    inv_count = _pool_inv_counts(seq)
    dmat, xi_f, xi_b, zeta_f, zeta_b, dec = _decay_tables(ret_decay_logit)
    kv = _mem_kv(mem, mem_norm_g, w_mem_kv.astype(BF16))

    w_pool = _pool_weights(w_pool_grp, pool_scale, w_pool_o)
    w_in_b = w_in.astype(BF16)
    w_ret_o_b = w_ret_o.astype(BF16)
    w_mem_o_b = w_mem_o.astype(BF16)
    w_out_b = w_out.astype(BF16)
    w_ff1_b = w_ff1.astype(BF16)
    w_ff2_b = w_ff2.astype(BF16)
    gains1 = norm1_g.reshape(depth, 1, d)
    gains2 = norm2_g.reshape(depth, 1, d)

    x2 = x.reshape(batch * seq, d)
    for l in range(depth):
        qk, v, g, p, qm, gates = _in_proj(x2, gains1, w_in_b, cos, sin, l, seq)
        states = _ret_states(qk, v, zeta_f, zeta_b, dec, l, batch, seq)
        x2 = _mix(x2, qk, v, g, p, inv_count, qm, gates, states, kv, dmat, xi_f, xi_b, w_ret_o_b, w_pool,
                  w_mem_o_b, w_out_b, l, batch, seq)
        x2 = _mlp(x2, gains2, w_ff1_b, w_ff2_b, final_norm_g, l, l == depth - 1)
    return x2.reshape(batch, seq, d)
```

```python
import functools

import jax
import jax.numpy as jnp
from jax import lax
from jax.experimental import pallas as pl
from jax.experimental.pallas import tpu as pltpu

F32 = jnp.float32
BF16 = jnp.bfloat16

RET_HEADS = 4
RET_QK_DIM = 128
RET_V_DIM = 256
MEM_HEADS = 4
MEM_HEAD_DIM = 128
POOL_WINDOWS = (2, 4, 8, 16)
POOL_GROUP = 128
N_BRANCHES = 3
ROPE_BASE = 10000.0
EPS = 1e-6
LOG2_E = 1.4426950408889634

RET_QK_W = RET_HEADS * RET_QK_DIM
RET_V_W = RET_HEADS * RET_V_DIM
MEM_Q_W = MEM_HEADS * MEM_HEAD_DIM
POOL_W = POOL_GROUP * len(POOL_WINDOWS)
POOL_HALO = max(POOL_WINDOWS) // 2

V7X_LANES = 128
V7X_VMEM_BYTES = 64 * 1024 * 1024
VMEM_LIMIT_BYTES = V7X_VMEM_BYTES - 4 * 1024 * 1024

RET_CHUNK = 256
ROW_TILE = 1024
MLP_TILE = 1024
MIX_TILE = 512
COL_CHUNK = 512
IN_PROJ_CHUNK = 256


def _params(*semantics):
    return pltpu.CompilerParams(dimension_semantics=semantics, vmem_limit_bytes=VMEM_LIMIT_BYTES)


def _resident(shape):
    zeros = (0,) * len(shape)
    return pl.BlockSpec(shape, lambda *_: zeros, pipeline_mode=pl.Buffered(1))


def _layer_resident(stacked_shape, layer):
    index = (layer,) + (0,) * (len(stacked_shape) - 1)
    return pl.BlockSpec((None,) + tuple(stacked_shape[1:]), lambda *_: index, pipeline_mode=pl.Buffered(1))


def _rms_norm(x, gain):
    return x * lax.rsqrt(jnp.mean(x * x, axis=-1, keepdims=True) + EPS) * gain


def _sigmoid(x):
    return 1.0 / (1.0 + jnp.exp(-x))


_dot = functools.partial(jnp.dot, preferred_element_type=F32)
_dot_nt = functools.partial(lax.dot_general, dimension_numbers=(((1,), (1,)), ((), ())), preferred_element_type=F32)
_dot_tn = functools.partial(lax.dot_general, dimension_numbers=(((0,), (0,)), ((), ())), preferred_element_type=F32)


def _decay_tables_kernel(logit_ref, dmat_ref, xi_f_ref, xi_b_ref, zeta_f_ref, zeta_b_ref, dec_ref):
    c = RET_CHUNK
    logit = logit_ref[0]
    log_g = jnp.minimum(logit, 0.0) - jnp.log1p(jnp.exp(-jnp.abs(logit)))
    lg_f, lg_b = log_g[0], log_g[1]
    row = lax.broadcasted_iota(jnp.int32, (c, RET_QK_W), 0).astype(F32)
    xi_f_ref[0] = jnp.exp(lg_f * (row + 1.0))
    xi_b_ref[0] = jnp.exp(lg_b * (c - row))
    zeta_f_ref[0] = jnp.exp(lg_f * (c - 1.0 - row))
    zeta_b_ref[0] = jnp.exp(lg_b * row)
    dec_ref[0, 0] = jnp.exp(lg_f * c)
    dec_ref[0, 1] = jnp.exp(lg_b * c)
    i = lax.broadcasted_iota(jnp.int32, (c, c), 0)
    j = lax.broadcasted_iota(jnp.int32, (c, c), 1)
    diff = (i - j).astype(F32)
    for h in range(RET_HEADS):
        lf = lg_f[:, h * RET_QK_DIM:h * RET_QK_DIM + 1]
        lb = lg_b[:, h * RET_QK_DIM:h * RET_QK_DIM + 1]
        fwd = jnp.exp(lf * jnp.maximum(diff, 0.0))
        bwd = jnp.exp(lb * jnp.maximum(-diff, 0.0))
        dmat_ref[0, h] = jnp.where(diff >= 0.0, fwd, bwd)


def _decay_tables(ret_decay_logit):
    depth = ret_decay_logit.shape[0]
    c = RET_CHUNK
    logit = jnp.repeat(ret_decay_logit.astype(F32), RET_QK_DIM, axis=-1)[:, :, None, :]
    vec = jax.ShapeDtypeStruct((depth, c, RET_QK_W), F32)
    vec_spec = pl.BlockSpec((1, c, RET_QK_W), lambda l: (l, 0, 0))
    return pl.pallas_call(
        _decay_tables_kernel,
        grid=(depth,),
        in_specs=[pl.BlockSpec((1, 2, 1, RET_QK_W), lambda l: (l, 0, 0, 0))],
        out_specs=[pl.BlockSpec((1, RET_HEADS, c, c), lambda l: (l, 0, 0, 0)),
                   vec_spec, vec_spec, vec_spec, vec_spec,
                   pl.BlockSpec((1, 2, 1, RET_QK_W), lambda l: (l, 0, 0, 0))],
        out_shape=[jax.ShapeDtypeStruct((depth, RET_HEADS, c, c), F32), vec, vec, vec, vec,
                   jax.ShapeDtypeStruct((depth, 2, 1, RET_QK_W), F32)],
        compiler_params=_params("arbitrary"),
        name="decay_tables",
    )(logit)


def _mem_kv_kernel(mem_ref, gain_ref, w_ref, kv_ref):
    mem_n = _rms_norm(mem_ref[0], gain_ref[...]).astype(BF16)
    for l in range(w_ref.shape[0]):
        kv_ref[l, 0] = _dot(mem_n, w_ref[l]).astype(BF16)


def _mem_kv(mem, mem_norm_g, w_mem_kv):
    b, m, d = mem.shape
    depth, _, kvw = w_mem_kv.shape
    return pl.pallas_call(
        _mem_kv_kernel,
        grid=(b,),
        in_specs=[pl.BlockSpec((1, m, d), lambda i: (i, 0, 0)),
                  _resident((1, d)),
                  _resident((depth, d, kvw))],
        out_specs=pl.BlockSpec((depth, 1, m, kvw), lambda i: (0, i, 0, 0)),
        out_shape=jax.ShapeDtypeStruct((depth, b, m, kvw), BF16),
        compiler_params=_params("arbitrary"),
        name="mem_kv",
    )(mem, mem_norm_g.reshape(1, d), w_mem_kv)


def _in_proj_kernel(x_ref, gain_ref, w_ref, cos_ref, sin_ref,
                    qk_ref, v_ref, g_ref, p_ref, qm_ref, gates_ref):
    x = x_ref[...]
    h = (x * gain_ref[...]).astype(BF16)
    inv_rms = lax.rsqrt(jnp.mean(x * x, axis=-1, keepdims=True) + EPS)
    cos = cos_ref[...]
    sin = sin_ref[...]

    def rotary(a, scale):
        heads = []
        for hd in range(a.shape[1] // RET_QK_DIM):
            ah = a[:, hd * RET_QK_DIM:(hd + 1) * RET_QK_DIM]
            heads.append(ah * cos + pltpu.roll(ah, RET_QK_DIM // 2, axis=1) * sin)
        rotated = jnp.concatenate(heads, axis=1)
        return rotated if scale is None else rotated * scale

    groups = [
        (qk_ref, 0, RET_QK_W, lambda a: rotary(a, None)),
        (qk_ref, RET_QK_W, RET_QK_W, lambda a: rotary(a, RET_QK_DIM ** -0.5)),
        (v_ref, 0, RET_V_W, lambda a: a),
        (g_ref, 0, RET_V_W, lambda a: a * _sigmoid(a)),
        (p_ref, 0, POOL_W, lambda a: a),
        (qm_ref, 0, MEM_Q_W, lambda a: a),
        (gates_ref, 0, gates_ref.shape[1], _sigmoid),
    ]
    w_col = 0
    for out_ref, out_col, width, epilogue in groups:
        for c in range(0, width, IN_PROJ_CHUNK):
            a = _dot(h, w_ref[:, w_col + c:w_col + c + IN_PROJ_CHUNK]) * inv_rms
            out_ref[:, out_col + c:out_col + c + IN_PROJ_CHUNK] = epilogue(a).astype(out_ref.dtype)
        w_col += width


def _in_proj(x2, gains, w_in, cos, sin, layer, seq):
    t, d = x2.shape
    tm = ROW_TILE
    gates_w = N_BRANCHES * d
    pos_tiles = seq // tm

    def rows(width):
        return pl.BlockSpec((tm, width), lambda i: (i, 0))

    pos_spec = pl.BlockSpec((tm, RET_QK_DIM), lambda i: (i % pos_tiles, 0))
    return pl.pallas_call(
        _in_proj_kernel,
        grid=(t // tm,),
        in_specs=[rows(d), _layer_resident(gains.shape, layer), _layer_resident(w_in.shape, layer),
                  pos_spec, pos_spec],
        out_specs=[rows(2 * RET_QK_W), rows(RET_V_W), rows(RET_V_W), rows(POOL_W), rows(MEM_Q_W),
                   rows(gates_w)],
        out_shape=[jax.ShapeDtypeStruct((t, 2 * RET_QK_W), BF16),
                   jax.ShapeDtypeStruct((t, RET_V_W), BF16),
                   jax.ShapeDtypeStruct((t, RET_V_W), BF16),
                   jax.ShapeDtypeStruct((t, POOL_W), F32),
                   jax.ShapeDtypeStruct((t, MEM_Q_W), BF16),
                   jax.ShapeDtypeStruct((t, gates_w), BF16)],
        compiler_params=_params("arbitrary"),
        name="in_proj",
    )(x2, gains, w_in, cos, sin)


def _ret_state_kernel(k_ref, v_ref, zeta_f_ref, zeta_b_ref, dec_ref, state_ref, acc):
    c = RET_CHUNK
    n = k_ref.shape[0] // c

    def scan(row0, zeta_ref, dec, order):
        acc[...] = jnp.zeros_like(acc)
        for ci in order:
            for h in range(RET_HEADS):
                state_ref[ci, h, row0:row0 + RET_QK_DIM, :] = acc[h].astype(BF16)
            if ci == order[-1]:
                break
            rows = slice(ci * c, (ci + 1) * c)
            kz = (k_ref[rows, :].astype(F32) * zeta_ref[0]).astype(BF16)
            for h in range(RET_HEADS):
                cols = slice(h * RET_QK_DIM, (h + 1) * RET_QK_DIM)
                outer = _dot_tn(kz[:, cols], v_ref[rows, h * RET_V_DIM:(h + 1) * RET_V_DIM])
                acc[h] = acc[h] * dec[:, h * RET_QK_DIM:h * RET_QK_DIM + 1] + outer

    scan(0, zeta_f_ref, dec_ref[0, 0], list(range(n)))
    scan(RET_QK_DIM, zeta_b_ref, dec_ref[0, 1], list(range(n - 1, -1, -1)))


def _ret_states(qk, v, zeta_f, zeta_b, dec, layer, batch, seq):
    c = RET_CHUNK
    n = seq // c
    table = pl.BlockSpec((1, c, RET_QK_W), lambda b: (layer, 0, 0), pipeline_mode=pl.Buffered(1))
    return pl.pallas_call(
        _ret_state_kernel,
        grid=(batch,),
        in_specs=[pl.BlockSpec((seq, RET_QK_W), lambda b: (b, 1)),
                  pl.BlockSpec((seq, RET_V_W), lambda b: (b, 0)),
                  table, table,
                  pl.BlockSpec((1, 2, 1, RET_QK_W), lambda b: (layer, 0, 0, 0), pipeline_mode=pl.Buffered(1))],
        out_specs=pl.BlockSpec((n, RET_HEADS, 2 * RET_QK_DIM, RET_V_DIM), lambda b: (b, 0, 0, 0)),
        out_shape=jax.ShapeDtypeStruct((batch * n, RET_HEADS, 2 * RET_QK_DIM, RET_V_DIM), BF16),
        scratch_shapes=[pltpu.VMEM((RET_HEADS, RET_QK_DIM, RET_V_DIM), F32)],
        compiler_params=_params("arbitrary"),
        name="ret_states",
    )(qk, v, zeta_f, zeta_b, dec)


def _window_sums(padded):
    length = padded.shape[0]
    halo = POOL_HALO
    ts = length - 2 * halo

    def ahead(a, k):
        return pltpu.roll(a, length - k, axis=0)

    def behind(a, k):
        return pltpu.roll(a, k, axis=0)

    sums = []
    for gi, w in enumerate(POOL_WINDOWS):
        a = padded[:, gi * POOL_GROUP:(gi + 1) * POOL_GROUP]
        span = 1
        while 2 * span < w:
            a = a + ahead(a, span)
            span *= 2
        assert 2 * span == w and span <= halo
        sums.append((a + behind(a, span))[halo:halo + ts, :])
    return sums


def _mix_kernel(x_ref, qk_ref, v_ref, g_ref, p_ref, p_prev_ref, p_next_ref, inv_count_ref, qm_ref, gates_ref,
                state_ref, kv_ref, dmat_ref, xi_f_ref, xi_b_ref,
                w_ret_o_ref, w_pool_ref, w_mem_o_ref, w_out_ref,
                out_ref, ret_scr):
    ts = x_ref.shape[0]
    d = x_ref.shape[1]
    c = RET_CHUNK
    tile = pl.program_id(1)
    n_tiles = pl.num_programs(1)

    xi_f = xi_f_ref[0]
    xi_b = xi_b_ref[0]
    for ci in range(ts // c):
        rows = slice(ci * c, (ci + 1) * c)
        q = qk_ref[rows, :RET_QK_W]
        q32 = q.astype(F32)
        q_f = (q32 * xi_f).astype(BF16)
        q_b = (q32 * xi_b).astype(BF16)
        for h in range(RET_HEADS):
            qk_cols = slice(h * RET_QK_DIM, (h + 1) * RET_QK_DIM)
            v_cols = slice(h * RET_V_DIM, (h + 1) * RET_V_DIM)
            k_h = qk_ref[rows, RET_QK_W + h * RET_QK_DIM:RET_QK_W + (h + 1) * RET_QK_DIM]
            s = _dot_nt(q[:, qk_cols], k_h) * dmat_ref[0, h]
            q_fb = jnp.concatenate([q_f[:, qk_cols], q_b[:, qk_cols]], axis=1)
            o = _dot(s.astype(BF16), v_ref[rows, v_cols]) + _dot(q_fb, state_ref[ci, h])
            mu = jnp.mean(o, axis=-1, keepdims=True)
            cen = o - mu
            var = jnp.mean(cen * cen, axis=-1, keepdims=True)
            o_n = cen * lax.rsqrt(var + EPS)
            ret_scr[rows, v_cols] = (o_n * g_ref[rows, v_cols].astype(F32)).astype(BF16)
    o_ret = _dot(ret_scr[...], w_ret_o_ref[...]).astype(BF16)

    p = p_ref[...]
    padded = jnp.concatenate([jnp.where(tile > 0, p_prev_ref[...], 0.0), p,
                              jnp.where(tile < n_tiles - 1, p_next_ref[...], 0.0)], axis=0)
    groups = []
    for gi, win in enumerate(_window_sums(padded)):
        cols = slice(gi * POOL_GROUP, (gi + 1) * POOL_GROUP)
        groups.append((win * inv_count_ref[:, cols] - p[:, cols]).astype(BF16))
    o_pool = _dot(jnp.concatenate(groups, axis=1), w_pool_ref[...]).astype(BF16)

    exp2_scale = (MEM_HEAD_DIM ** -0.5) * LOG2_E
    heads = []
    for h in range(MEM_HEADS):
        cols = slice(h * MEM_HEAD_DIM, (h + 1) * MEM_HEAD_DIM)
        k_h = kv_ref[0, :, cols]
        v_h = kv_ref[0, :, MEM_Q_W + h * MEM_HEAD_DIM:MEM_Q_W + (h + 1) * MEM_HEAD_DIM]
        s = _dot_nt(qm_ref[:, cols], k_h)
        e = jnp.exp2((s - jnp.max(s, axis=-1, keepdims=True)) * exp2_scale)
        o = _dot(e.astype(BF16), v_h) / jnp.sum(e, axis=-1, keepdims=True)
        heads.append(o.astype(BF16))
    o_mem = _dot(jnp.concatenate(heads, axis=1), w_mem_o_ref[...]).astype(BF16)

    merged = gates_ref[:, :d] * o_ret + gates_ref[:, d:2 * d] * o_pool + gates_ref[:, 2 * d:] * o_mem
    out_ref[...] = x_ref[...] + _dot(merged, w_out_ref[...])


def _mix(x2, qk, v, g, p, inv_count, qm, gates, states, kv, dmat, xi_f, xi_b, w_ret_o, w_pool, w_mem_o, w_out,
         layer, batch, seq):
    t, d = x2.shape
    ts = MIX_TILE
    c = RET_CHUNK
    nt = seq // ts
    halo = POOL_HALO
    halo_per_tile = ts // halo
    last_halo = t // halo - 1

    def rows(width):
        return pl.BlockSpec((ts, width), lambda b, i: (b * nt + i, 0))

    p_prev = pl.BlockSpec((halo, POOL_W), lambda b, i: (jnp.maximum((b * nt + i) * halo_per_tile - 1, 0), 0))
    p_next = pl.BlockSpec((halo, POOL_W),
                          lambda b, i: (jnp.minimum((b * nt + i + 1) * halo_per_tile, last_halo), 0))
    state = pl.BlockSpec((ts // c,) + states.shape[1:], lambda b, i: (b * nt + i, 0, 0, 0))
    kv_spec = pl.BlockSpec((1,) + kv.shape[2:], lambda b, i: (layer * batch + b, 0, 0))
    dmat_spec = pl.BlockSpec((1, RET_HEADS, c, c), lambda b, i: (layer, 0, 0, 0), pipeline_mode=pl.Buffered(1))
    xi_spec = pl.BlockSpec((1, c, RET_QK_W), lambda b, i: (layer, 0, 0), pipeline_mode=pl.Buffered(1))
    return pl.pallas_call(
        _mix_kernel,
        grid=(batch, nt),
        in_specs=[rows(d), rows(2 * RET_QK_W), rows(RET_V_W), rows(RET_V_W), rows(POOL_W), p_prev, p_next,
                  pl.BlockSpec((ts, POOL_W), lambda b, i: (i, 0)),
                  rows(MEM_Q_W), rows(N_BRANCHES * d), state, kv_spec, dmat_spec, xi_spec, xi_spec,
                  _layer_resident(w_ret_o.shape, layer), _layer_resident(w_pool.shape, layer),
                  _layer_resident(w_mem_o.shape, layer), _layer_resident(w_out.shape, layer)],
        out_specs=rows(d),
        out_shape=jax.ShapeDtypeStruct((t, d), F32),
        scratch_shapes=[pltpu.VMEM((ts, RET_V_W), BF16)],
        compiler_params=_params("arbitrary", "arbitrary"),
        name="mix",
    )(x2, qk, v, g, p, p, p, inv_count, qm, gates, states, kv.reshape((-1,) + kv.shape[2:]), dmat, xi_f, xi_b,
      w_ret_o, w_pool, w_mem_o, w_out)


def _mlp_kernel(x_ref, gain_ref, w1_ref, w2_ref, final_gain_ref, out_ref, *, final_norm):
    x = x_ref[...]
    h = (x * gain_ref[...]).astype(BF16)
    acc = None
    for col in range(0, w1_ref.shape[1], COL_CHUNK):
        hid = jnp.maximum(_dot(h, w1_ref[:, col:col + COL_CHUNK]), 0.0)
        part = _dot((hid * hid).astype(BF16), w2_ref[col:col + COL_CHUNK, :])
        acc = part if acc is None else acc + part
    out = x + acc / (jnp.mean(x * x, axis=-1, keepdims=True) + EPS)
    out_ref[...] = _rms_norm(out, final_gain_ref[...]) if final_norm else out


def _mlp(x2, gains, w1, w2, final_gain, layer, final_norm):
    t, d = x2.shape
    tm = MLP_TILE
    rows = pl.BlockSpec((tm, d), lambda i: (i, 0))
    return pl.pallas_call(
        functools.partial(_mlp_kernel, final_norm=final_norm),
        grid=(t // tm,),
        in_specs=[rows, _layer_resident(gains.shape, layer), _layer_resident(w1.shape, layer),
                  _layer_resident(w2.shape, layer), _resident((1, d))],
        out_specs=rows,
        out_shape=jax.ShapeDtypeStruct((t, d), F32),
        compiler_params=_params("arbitrary"),
        name="mlp",
    )(x2, gains, w1, w2, final_gain.reshape(1, d))


def _pool_weight_kernel(w_grp_ref, scale_ref, w_o_ref, out_ref):
    for gi in range(len(POOL_WINDOWS)):
        rows = slice(gi * POOL_GROUP, (gi + 1) * POOL_GROUP)
        scaled = w_grp_ref[0, gi] * scale_ref[0, :, rows]
        out_ref[0, rows, :] = jnp.dot(scaled, w_o_ref[0, rows, :], preferred_element_type=F32,
                                      precision=lax.Precision.HIGHEST).astype(BF16)


def _pool_weights(w_pool_grp, pool_scale, w_pool_o):
    depth, groups, group_w, _ = w_pool_grp.shape
    d = w_pool_o.shape[2]
    return pl.pallas_call(
        _pool_weight_kernel,
        grid=(depth,),
        in_specs=[pl.BlockSpec((1, groups, group_w, group_w), lambda l: (l, 0, 0, 0)),
                  pl.BlockSpec((1, 1, POOL_W), lambda l: (l, 0, 0)),
                  pl.BlockSpec((1, POOL_W, d), lambda l: (l, 0, 0))],
        out_specs=pl.BlockSpec((1, POOL_W, d), lambda l: (l, 0, 0)),
        out_shape=jax.ShapeDtypeStruct((depth, POOL_W, d), BF16),
        compiler_params=_params("arbitrary"),
        name="pool_weights",
    )(w_pool_grp, pool_scale.reshape(depth, 1, POOL_W), w_pool_o)


def _rotary_tables(seq):
    inv = ROPE_BASE ** (-jnp.arange(0, RET_QK_DIM, 2, dtype=F32) / RET_QK_DIM)
    ang = jnp.arange(seq, dtype=F32)[:, None] * inv[None, :]
    cos, sin = jnp.cos(ang), jnp.sin(ang)
    return jnp.concatenate([cos, cos], axis=1), jnp.concatenate([-sin, sin], axis=1)


def _pool_inv_counts(seq):
    pos = jnp.arange(seq)
    cols = []
    for w in POOL_WINDOWS:
        count = jnp.minimum(pos + w // 2, seq) - jnp.maximum(pos - w // 2, 0)
        cols.append(jnp.broadcast_to((1.0 / count.astype(F32))[:, None], (seq, POOL_GROUP)))
    return jnp.concatenate(cols, axis=1)


def kernel(x, mem, w_in, ret_decay_logit, w_ret_o, w_pool_grp, pool_scale, w_pool_o, w_mem_kv, w_mem_o,
           w_out, w_ff1, w_ff2, norm1_g, norm2_g, mem_norm_g, final_norm_g):
    batch, seq, d = x.shape
    depth = w_in.shape[0]
    assert seq % MIX_TILE == 0 and seq % ROW_TILE == 0 and MIX_TILE % RET_CHUNK == 0
    assert d % COL_CHUNK == 0 and POOL_HALO % 8 == 0

    cos, sin = _rotary_tables(seq)
    inv_count = _pool_inv_counts(seq)
    dmat, xi_f, xi_b, zeta_f, zeta_b, dec = _decay_tables(ret_decay_logit)
    kv = _mem_kv(mem, mem_norm_g, w_mem_kv.astype(BF16))

    w_pool = _pool_weights(w_pool_grp, pool_scale, w_pool_o)
    w_in_b = w_in.astype(BF16)
    w_ret_o_b = w_ret_o.astype(BF16)
    w_mem_o_b = w_mem_o.astype(BF16)
    w_out_b = w_out.astype(BF16)
    w_ff1_b = w_ff1.astype(BF16)
    w_ff2_b = w_ff2.astype(BF16)
    gains1 = norm1_g.reshape(depth, 1, d)
    gains2 = norm2_g.reshape(depth, 1, d)

    x2 = x.reshape(batch * seq, d)
    for l in range(depth):
        qk, v, g, p, qm, gates = _in_proj(x2, gains1, w_in_b, cos, sin, l, seq)
        states = _ret_states(qk, v, zeta_f, zeta_b, dec, l, batch, seq)
        x2 = _mix(x2, qk, v, g, p, inv_count, qm, gates, states, kv, dmat, xi_f, xi_b, w_ret_o_b, w_pool,
                  w_mem_o_b, w_out_b, l, batch, seq)
        x2 = _mlp(x2, gains2, w_ff1_b, w_ff2_b, final_norm_g, l, l == depth - 1)
    return x2.reshape(batch, seq, d)
```

```python
import functools

import jax
import jax.numpy as jnp
from jax import lax
from jax.experimental import pallas as pl
from jax.experimental.pallas import tpu as pltpu

F32 = jnp.float32
BF16 = jnp.bfloat16

RET_HEADS = 4
RET_QK_DIM = 128
RET_V_DIM = 256
MEM_HEADS = 4
MEM_HEAD_DIM = 128
POOL_WINDOWS = (2, 4, 8, 16)
POOL_GROUP = 128
N_BRANCHES = 3
ROPE_BASE = 10000.0
EPS = 1e-6
LOG2_E = 1.4426950408889634

RET_QK_W = RET_HEADS * RET_QK_DIM
RET_V_W = RET_HEADS * RET_V_DIM
MEM_Q_W = MEM_HEADS * MEM_HEAD_DIM
POOL_W = POOL_GROUP * len(POOL_WINDOWS)
POOL_HALO = max(POOL_WINDOWS) // 2
GATE_COL0 = 2 * RET_QK_W + 2 * RET_V_W + POOL_W + MEM_Q_W

V7X_LANES = 128
V7X_VMEM_BYTES = 64 * 1024 * 1024
VMEM_LIMIT_BYTES = V7X_VMEM_BYTES - 4 * 1024 * 1024

RET_CHUNK = 256
ROW_TILE = 1024
MLP_TILE = 1024
MIX_TILE = 512
COL_CHUNK = 512
IN_PROJ_CHUNK = 256
MIX_FILL_CHUNK = 256


def _params(*semantics):
    return pltpu.CompilerParams(dimension_semantics=semantics, vmem_limit_bytes=VMEM_LIMIT_BYTES)


def _resident(shape):
    zeros = (0,) * len(shape)
    return pl.BlockSpec(shape, lambda *_: zeros, pipeline_mode=pl.Buffered(1))


def _layer_resident(stacked_shape, layer):
    index = (layer,) + (0,) * (len(stacked_shape) - 1)
    return pl.BlockSpec((None,) + tuple(stacked_shape[1:]), lambda *_: index, pipeline_mode=pl.Buffered(1))


def _rms_norm(x, gain):
    return x * lax.rsqrt(jnp.mean(x * x, axis=-1, keepdims=True) + EPS) * gain


def _sigmoid(x):
    return 1.0 / (1.0 + jnp.exp(-x))


_dot = functools.partial(jnp.dot, preferred_element_type=F32)
_dot_nt = functools.partial(lax.dot_general, dimension_numbers=(((1,), (1,)), ((), ())), preferred_element_type=F32)
_dot_tn = functools.partial(lax.dot_general, dimension_numbers=(((0,), (0,)), ((), ())), preferred_element_type=F32)


def _decay_tables_kernel(logit_ref, dmat_ref, xi_f_ref, xi_b_ref, zeta_f_ref, zeta_b_ref, dec_ref):
    c = RET_CHUNK
    logit = logit_ref[0]
    log_g = jnp.minimum(logit, 0.0) - jnp.log1p(jnp.exp(-jnp.abs(logit)))
    lg_f, lg_b = log_g[0], log_g[1]
    row = lax.broadcasted_iota(jnp.int32, (c, RET_QK_W), 0).astype(F32)
    xi_f_ref[0] = jnp.exp(lg_f * (row + 1.0))
    xi_b_ref[0] = jnp.exp(lg_b * (c - row))
    zeta_f_ref[0] = jnp.exp(lg_f * (c - 1.0 - row))
    zeta_b_ref[0] = jnp.exp(lg_b * row)
    dec_ref[0, 0] = jnp.exp(lg_f * c)
    dec_ref[0, 1] = jnp.exp(lg_b * c)
    i = lax.broadcasted_iota(jnp.int32, (c, c), 0)
    j = lax.broadcasted_iota(jnp.int32, (c, c), 1)
    diff = (i - j).astype(F32)
    for h in range(RET_HEADS):
        lf = lg_f[:, h * RET_QK_DIM:h * RET_QK_DIM + 1]
        lb = lg_b[:, h * RET_QK_DIM:h * RET_QK_DIM + 1]
        fwd = jnp.exp(lf * jnp.maximum(diff, 0.0))
        bwd = jnp.exp(lb * jnp.maximum(-diff, 0.0))
        dmat_ref[0, h] = jnp.where(diff >= 0.0, fwd, bwd)


def _decay_tables(ret_decay_logit):
    depth = ret_decay_logit.shape[0]
    c = RET_CHUNK
    logit = jnp.repeat(ret_decay_logit.astype(F32), RET_QK_DIM, axis=-1)[:, :, None, :]
    vec = jax.ShapeDtypeStruct((depth, c, RET_QK_W), F32)
    vec_spec = pl.BlockSpec((1, c, RET_QK_W), lambda l: (l, 0, 0))
    return pl.pallas_call(
        _decay_tables_kernel,
        grid=(depth,),
        in_specs=[pl.BlockSpec((1, 2, 1, RET_QK_W), lambda l: (l, 0, 0, 0))],
        out_specs=[pl.BlockSpec((1, RET_HEADS, c, c), lambda l: (l, 0, 0, 0)),
                   vec_spec, vec_spec, vec_spec, vec_spec,
                   pl.BlockSpec((1, 2, 1, RET_QK_W), lambda l: (l, 0, 0, 0))],
        out_shape=[jax.ShapeDtypeStruct((depth, RET_HEADS, c, c), F32), vec, vec, vec, vec,
                   jax.ShapeDtypeStruct((depth, 2, 1, RET_QK_W), F32)],
        compiler_params=_params("arbitrary"),
        name="decay_tables",
    )(logit)


def _mem_kv_kernel(mem_ref, gain_ref, w_ref, kv_ref):
    mem_n = _rms_norm(mem_ref[0], gain_ref[...]).astype(BF16)
    for l in range(w_ref.shape[0]):
        kv_ref[l, 0] = _dot(mem_n, w_ref[l]).astype(BF16)


def _mem_kv(mem, mem_norm_g, w_mem_kv):
    b, m, d = mem.shape
    depth, _, kvw = w_mem_kv.shape
    return pl.pallas_call(
        _mem_kv_kernel,
        grid=(b,),
        in_specs=[pl.BlockSpec((1, m, d), lambda i: (i, 0, 0)),
                  _resident((1, d)),
                  _resident((depth, d, kvw))],
        out_specs=pl.BlockSpec((depth, 1, m, kvw), lambda i: (0, i, 0, 0)),
        out_shape=jax.ShapeDtypeStruct((depth, b, m, kvw), BF16),
        compiler_params=_params("arbitrary"),
        name="mem_kv",
    )(mem, mem_norm_g.reshape(1, d), w_mem_kv)


def _in_proj_kernel(x_ref, gain_ref, w_ref, cos_ref, sin_ref,
                    qk_ref, v_ref, g_ref, p_ref, qm_ref):
    x = x_ref[...]
    h = (x * gain_ref[...]).astype(BF16)
    inv_rms = lax.rsqrt(jnp.mean(x * x, axis=-1, keepdims=True) + EPS)
    cos = cos_ref[...]
    sin = sin_ref[...]

    def rotary(a, scale):
        heads = []
        for hd in range(a.shape[1] // RET_QK_DIM):
            ah = a[:, hd * RET_QK_DIM:(hd + 1) * RET_QK_DIM]
            heads.append(ah * cos + pltpu.roll(ah, RET_QK_DIM // 2, axis=1) * sin)
        rotated = jnp.concatenate(heads, axis=1)
        return rotated if scale is None else rotated * scale

    groups = [
        (qk_ref, 0, RET_QK_W, lambda a: rotary(a, None)),
        (qk_ref, RET_QK_W, RET_QK_W, lambda a: rotary(a, RET_QK_DIM ** -0.5)),
        (v_ref, 0, RET_V_W, lambda a: a),
        (g_ref, 0, RET_V_W, lambda a: a * _sigmoid(a)),
        (p_ref, 0, POOL_W, lambda a: a),
        (qm_ref, 0, MEM_Q_W, lambda a: a),
    ]
    w_col = 0
    for out_ref, out_col, width, epilogue in groups:
        for c in range(0, width, IN_PROJ_CHUNK):
            a = _dot(h, w_ref[:, w_col + c:w_col + c + IN_PROJ_CHUNK]) * inv_rms
            out_ref[:, out_col + c:out_col + c + IN_PROJ_CHUNK] = epilogue(a).astype(out_ref.dtype)
        w_col += width


def _in_proj(x2, gains, w_in, cos, sin, layer, seq):
    t, d = x2.shape
    tm = ROW_TILE
    pos_tiles = seq // tm
    w_cols = GATE_COL0

    def rows(width):
        return pl.BlockSpec((tm, width), lambda i: (i, 0))

    pos_spec = pl.BlockSpec((tm, RET_QK_DIM), lambda i: (i % pos_tiles, 0))
    w_spec = pl.BlockSpec((None, d, w_cols), lambda i: (layer, 0, 0), pipeline_mode=pl.Buffered(1))
    return pl.pallas_call(
        _in_proj_kernel,
        grid=(t // tm,),
        in_specs=[rows(d), _layer_resident(gains.shape, layer), w_spec, pos_spec, pos_spec],
        out_specs=[rows(2 * RET_QK_W), rows(RET_V_W), rows(RET_V_W), rows(POOL_W), rows(MEM_Q_W)],
        out_shape=[jax.ShapeDtypeStruct((t, 2 * RET_QK_W), BF16),
                   jax.ShapeDtypeStruct((t, RET_V_W), BF16),
                   jax.ShapeDtypeStruct((t, RET_V_W), BF16),
                   jax.ShapeDtypeStruct((t, POOL_W), F32),
                   jax.ShapeDtypeStruct((t, MEM_Q_W), BF16)],
        compiler_params=_params("arbitrary"),
        name="in_proj",
    )(x2, gains, w_in, cos, sin)


def _ret_state_kernel(k_ref, v_ref, zeta_f_ref, zeta_b_ref, dec_ref, state_ref, acc):
    c = RET_CHUNK
    n = k_ref.shape[0] // c

    def scan(row0, zeta_ref, dec, order):
        acc[...] = jnp.zeros_like(acc)
        for ci in order:
            for h in range(RET_HEADS):
                state_ref[ci, h, row0:row0 + RET_QK_DIM, :] = acc[h].astype(BF16)
            if ci == order[-1]:
                break
            rows = slice(ci * c, (ci + 1) * c)
            kz = (k_ref[rows, :].astype(F32) * zeta_ref[0]).astype(BF16)
            for h in range(RET_HEADS):
                cols = slice(h * RET_QK_DIM, (h + 1) * RET_QK_DIM)
                outer = _dot_tn(kz[:, cols], v_ref[rows, h * RET_V_DIM:(h + 1) * RET_V_DIM])
                acc[h] = acc[h] * dec[:, h * RET_QK_DIM:h * RET_QK_DIM + 1] + outer

    scan(0, zeta_f_ref, dec_ref[0, 0], list(range(n)))
    scan(RET_QK_DIM, zeta_b_ref, dec_ref[0, 1], list(range(n - 1, -1, -1)))


def _ret_states(qk, v, zeta_f, zeta_b, dec, layer, batch, seq):
    c = RET_CHUNK
    n = seq // c
    table = pl.BlockSpec((1, c, RET_QK_W), lambda b: (layer, 0, 0), pipeline_mode=pl.Buffered(1))
    return pl.pallas_call(
        _ret_state_kernel,
        grid=(batch,),
        in_specs=[pl.BlockSpec((seq, RET_QK_W), lambda b: (b, 1)),
                  pl.BlockSpec((seq, RET_V_W), lambda b: (b, 0)),
                  table, table,
                  pl.BlockSpec((1, 2, 1, RET_QK_W), lambda b: (layer, 0, 0, 0), pipeline_mode=pl.Buffered(1))],
        out_specs=pl.BlockSpec((n, RET_HEADS, 2 * RET_QK_DIM, RET_V_DIM), lambda b: (b, 0, 0, 0)),
        out_shape=jax.ShapeDtypeStruct((batch * n, RET_HEADS, 2 * RET_QK_DIM, RET_V_DIM), BF16),
        scratch_shapes=[pltpu.VMEM((RET_HEADS, RET_QK_DIM, RET_V_DIM), F32)],
        compiler_params=_params("arbitrary"),
        name="ret_states",
    )(qk, v, zeta_f, zeta_b, dec)


def _window_sums(padded):
    length = padded.shape[0]
    halo = POOL_HALO
    ts = length - 2 * halo

    def ahead(a, k):
        return pltpu.roll(a, length - k, axis=0)

    def behind(a, k):
        return pltpu.roll(a, k, axis=0)

    sums = []
    for gi, w in enumerate(POOL_WINDOWS):
        a = padded[:, gi * POOL_GROUP:(gi + 1) * POOL_GROUP]
        span = 1
        while 2 * span < w:
            a = a + ahead(a, span)
            span *= 2
        assert 2 * span == w and span <= halo
        sums.append((a + behind(a, span))[halo:halo + ts, :])
    return sums


def _mix_kernel(x_ref, gain_ref, qk_ref, v_ref, g_ref, p_ref, p_prev_ref, p_next_ref, inv_count_ref, qm_ref,
                state_ref, kv_ref, dmat_ref, xi_f_ref, xi_b_ref,
                w_gate_ret_ref, w_gate_pool_ref, w_gate_mem_ref,
                w_ret_o_ref, w_pool_ref, w_mem_o_ref, w_out_ref,
                out_ref, ret_scr, gate_scr):
    ts = x_ref.shape[0]
    c = RET_CHUNK
    tile = pl.program_id(1)
    n_tiles = pl.num_programs(1)

    d = x_ref.shape[1]
    fill_cols = list(range(0, d, MIX_FILL_CHUNK))

    def out_proj_chunk(lhs, w_ref, col):
        return _dot(lhs, w_ref[:, col:col + MIX_FILL_CHUNK]).astype(BF16)

    x = x_ref[...]
    x_gained = (x * gain_ref[...]).astype(BF16)
    half_inv_rms = 0.5 * lax.rsqrt(jnp.mean(x * x, axis=-1, keepdims=True) + EPS)
    gate_w_refs = (w_gate_ret_ref, w_gate_pool_ref, w_gate_mem_ref)
    gate_jobs = [(b, col) for b in range(N_BRANCHES) for col in fill_cols]

    def gate_job():
        b, col = gate_jobs.pop(0)
        z_half = _dot(x_gained, gate_w_refs[b][:, col:col + MIX_FILL_CHUNK]) * half_inv_rms
        gate_scr[:, b * d + col:b * d + col + MIX_FILL_CHUNK] = jnp.tanh(z_half).astype(BF16) * 0.5 + 0.5

    xi_f = xi_f_ref[0]
    xi_b = xi_b_ref[0]
    for ci in range(ts // c):
        rows = slice(ci * c, (ci + 1) * c)
        q = qk_ref[rows, :RET_QK_W]
        q32 = q.astype(F32)
        q_f = (q32 * xi_f).astype(BF16)
        q_b = (q32 * xi_b).astype(BF16)
        for h in range(RET_HEADS):
            qk_cols = slice(h * RET_QK_DIM, (h + 1) * RET_QK_DIM)
            v_cols = slice(h * RET_V_DIM, (h + 1) * RET_V_DIM)
            k_h = qk_ref[rows, RET_QK_W + h * RET_QK_DIM:RET_QK_W + (h + 1) * RET_QK_DIM]
            s = _dot_nt(q[:, qk_cols], k_h) * dmat_ref[0, h]
            q_fb = jnp.concatenate([q_f[:, qk_cols], q_b[:, qk_cols]], axis=1)
            o = _dot(s.astype(BF16), v_ref[rows, v_cols]) + _dot(q_fb, state_ref[ci, h])
            mu = jnp.mean(o, axis=-1, keepdims=True)
            cen = o - mu
            var = jnp.mean(cen * cen, axis=-1, keepdims=True)
            o_n = cen * lax.rsqrt(var + EPS)
            ret_scr[rows, v_cols] = (o_n * g_ref[rows, v_cols].astype(F32)).astype(BF16)
            if gate_jobs:
                gate_job()

    p = p_ref[...]
    padded = jnp.concatenate([jnp.where(tile > 0, p_prev_ref[...], 0.0), p,
                              jnp.where(tile < n_tiles - 1, p_next_ref[...], 0.0)], axis=0)
    ret_lhs = ret_scr[...]
    groups, o_ret = [], []
    for gi, win in enumerate(_window_sums(padded)):
        cols = slice(gi * POOL_GROUP, (gi + 1) * POOL_GROUP)
        groups.append((win * inv_count_ref[:, cols] - p[:, cols]).astype(BF16))
        if gi < len(fill_cols):
            o_ret.append(out_proj_chunk(ret_lhs, w_ret_o_ref, fill_cols[gi]))
    o_ret += [out_proj_chunk(ret_lhs, w_ret_o_ref, col) for col in fill_cols[len(o_ret):]]
    pool_lhs = jnp.concatenate(groups, axis=1)

    exp2_scale = (MEM_HEAD_DIM ** -0.5) * LOG2_E
    heads, o_pool = [], []
    for h in range(MEM_HEADS):
        cols = slice(h * MEM_HEAD_DIM, (h + 1) * MEM_HEAD_DIM)
        k_h = kv_ref[0, :, cols]
        v_h = kv_ref[0, :, MEM_Q_W + h * MEM_HEAD_DIM:MEM_Q_W + (h + 1) * MEM_HEAD_DIM]
        s = _dot_nt(qm_ref[:, cols], k_h)
        e = jnp.exp2((s - jnp.max(s, axis=-1, keepdims=True)) * exp2_scale)
        o = _dot(e.astype(BF16), v_h) / jnp.sum(e, axis=-1, keepdims=True)
        heads.append(o.astype(BF16))
        if gate_jobs:
            gate_job()
        if h < len(fill_cols):
            o_pool.append(out_proj_chunk(pool_lhs, w_pool_ref, fill_cols[h]))
    while gate_jobs:
        gate_job()
    o_pool += [out_proj_chunk(pool_lhs, w_pool_ref, col) for col in fill_cols[len(o_pool):]]
    mem_lhs = jnp.concatenate(heads, axis=1)

    merged = []
    for j, col in enumerate(fill_cols):
        o_mem = out_proj_chunk(mem_lhs, w_mem_o_ref, col)
        gates = [gate_scr[:, b * d + col:b * d + col + MIX_FILL_CHUNK] for b in range(N_BRANCHES)]
        merged.append(gates[0] * o_ret[j] + gates[1] * o_pool[j] + gates[2] * o_mem)
    out_ref[...] = x + _dot(jnp.concatenate(merged, axis=1), w_out_ref[...])


def _mix(x2, gains, qk, v, g, p, inv_count, qm, states, kv, dmat, xi_f, xi_b, w_in, w_ret_o, w_pool, w_mem_o,
         w_out, layer, batch, seq):
    t, d = x2.shape
    assert GATE_COL0 % d == 0

    def gate_weight(branch):
        return pl.BlockSpec((None, d, d), lambda b, i: (layer, 0, GATE_COL0 // d + branch),
                            pipeline_mode=pl.Buffered(1))

    ts = MIX_TILE
    c = RET_CHUNK
    nt = seq // ts
    halo = POOL_HALO
    halo_per_tile = ts // halo
    last_halo = t // halo - 1

    def rows(width):
        return pl.BlockSpec((ts, width), lambda b, i: (b * nt + i, 0))

    p_prev = pl.BlockSpec((halo, POOL_W), lambda b, i: (jnp.maximum((b * nt + i) * halo_per_tile - 1, 0), 0))
    p_next = pl.BlockSpec((halo, POOL_W),
                          lambda b, i: (jnp.minimum((b * nt + i + 1) * halo_per_tile, last_halo), 0))
    state = pl.BlockSpec((ts // c,) + states.shape[1:], lambda b, i: (b * nt + i, 0, 0, 0))
    kv_spec = pl.BlockSpec((1,) + kv.shape[2:], lambda b, i: (layer * batch + b, 0, 0))
    dmat_spec = pl.BlockSpec((1, RET_HEADS, c, c), lambda b, i: (layer, 0, 0, 0), pipeline_mode=pl.Buffered(1))
    xi_spec = pl.BlockSpec((1, c, RET_QK_W), lambda b, i: (layer, 0, 0), pipeline_mode=pl.Buffered(1))
    return pl.pallas_call(
        _mix_kernel,
        grid=(batch, nt),
        in_specs=[rows(d), _layer_resident(gains.shape, layer), rows(2 * RET_QK_W), rows(RET_V_W), rows(RET_V_W),
                  rows(POOL_W), p_prev, p_next, pl.BlockSpec((ts, POOL_W), lambda b, i: (i, 0)),
                  rows(MEM_Q_W), state, kv_spec, dmat_spec, xi_spec, xi_spec,
                  gate_weight(0), gate_weight(1), gate_weight(2),
                  _layer_resident(w_ret_o.shape, layer), _layer_resident(w_pool.shape, layer),
                  _layer_resident(w_mem_o.shape, layer), _layer_resident(w_out.shape, layer)],
        out_specs=rows(d),
        out_shape=jax.ShapeDtypeStruct((t, d), F32),
        scratch_shapes=[pltpu.VMEM((ts, RET_V_W), BF16), pltpu.VMEM((ts, N_BRANCHES * d), BF16)],
        compiler_params=_params("arbitrary", "arbitrary"),
        name="mix",
    )(x2, gains, qk, v, g, p, p, p, inv_count, qm, states, kv.reshape((-1,) + kv.shape[2:]), dmat, xi_f, xi_b,
      w_in, w_in, w_in, w_ret_o, w_pool, w_mem_o, w_out)


def _mlp_kernel(x_ref, gain_ref, w1_ref, w2_ref, final_gain_ref, out_ref, *, final_norm):
    x = x_ref[...]
    h = (x * gain_ref[...]).astype(BF16)
    acc = None
    for col in range(0, w1_ref.shape[1], COL_CHUNK):
        hid = jnp.maximum(_dot(h, w1_ref[:, col:col + COL_CHUNK]), 0.0)
        part = _dot((hid * hid).astype(BF16), w2_ref[col:col + COL_CHUNK, :])
        acc = part if acc is None else acc + part
    out = x + acc / (jnp.mean(x * x, axis=-1, keepdims=True) + EPS)
    out_ref[...] = _rms_norm(out, final_gain_ref[...]) if final_norm else out


def _mlp(x2, gains, w1, w2, final_gain, layer, final_norm):
    t, d = x2.shape
    tm = MLP_TILE
    rows = pl.BlockSpec((tm, d), lambda i: (i, 0))
    return pl.pallas_call(
        functools.partial(_mlp_kernel, final_norm=final_norm),
        grid=(t // tm,),
        in_specs=[rows, _layer_resident(gains.shape, layer), _layer_resident(w1.shape, layer),
                  _layer_resident(w2.shape, layer), _resident((1, d))],
        out_specs=rows,
        out_shape=jax.ShapeDtypeStruct((t, d), F32),
        compiler_params=_params("arbitrary"),
        name="mlp",
    )(x2, gains, w1, w2, final_gain.reshape(1, d))


def _pool_weight_kernel(w_grp_ref, scale_ref, w_o_ref, out_ref):
    for gi in range(len(POOL_WINDOWS)):
        rows = slice(gi * POOL_GROUP, (gi + 1) * POOL_GROUP)
        scaled = w_grp_ref[0, gi] * scale_ref[0, :, rows]
        out_ref[0, rows, :] = jnp.dot(scaled, w_o_ref[0, rows, :], preferred_element_type=F32,
                                      precision=lax.Precision.HIGHEST).astype(BF16)


def _pool_weights(w_pool_grp, pool_scale, w_pool_o):
    depth, groups, group_w, _ = w_pool_grp.shape
    d = w_pool_o.shape[2]
    return pl.pallas_call(
        _pool_weight_kernel,
        grid=(depth,),
        in_specs=[pl.BlockSpec((1, groups, group_w, group_w), lambda l: (l, 0, 0, 0)),
                  pl.BlockSpec((1, 1, POOL_W), lambda l: (l, 0, 0)),
                  pl.BlockSpec((1, POOL_W, d), lambda l: (l, 0, 0))],
        out_specs=pl.BlockSpec((1, POOL_W, d), lambda l: (l, 0, 0)),
        out_shape=jax.ShapeDtypeStruct((depth, POOL_W, d), BF16),
        compiler_params=_params("arbitrary"),
        name="pool_weights",
    )(w_pool_grp, pool_scale.reshape(depth, 1, POOL_W), w_pool_o)


def _rotary_tables(seq):
    inv = ROPE_BASE ** (-jnp.arange(0, RET_QK_DIM, 2, dtype=F32) / RET_QK_DIM)
    ang = jnp.arange(seq, dtype=F32)[:, None] * inv[None, :]
    cos, sin = jnp.cos(ang), jnp.sin(ang)
    return jnp.concatenate([cos, cos], axis=1), jnp.concatenate([-sin, sin], axis=1)


def _pool_inv_counts(seq):
    pos = jnp.arange(seq)
    cols = []
    for w in POOL_WINDOWS:
        count = jnp.minimum(pos + w // 2, seq) - jnp.maximum(pos - w // 2, 0)
        cols.append(jnp.broadcast_to((1.0 / count.astype(F32))[:, None], (seq, POOL_GROUP)))
    return jnp.concatenate(cols, axis=1)


def kernel(x, mem, w_in, ret_decay_logit, w_ret_o, w_pool_grp, pool_scale, w_pool_o, w_mem_kv, w_mem_o,
           w_out, w_ff1, w_ff2, norm1_g, norm2_g, mem_norm_g, final_norm_g):
    batch, seq, d = x.shape
    depth = w_in.shape[0]
    assert seq % MIX_TILE == 0 and seq % ROW_TILE == 0 and MIX_TILE % RET_CHUNK == 0
    assert d % COL_CHUNK == 0 and POOL_HALO % 8 == 0

    cos, sin = _rotary_tables(seq)
    inv_count = _pool_inv_counts(seq)
    dmat, xi_f, xi_b, zeta_f, zeta_b, dec = _decay_tables(ret_decay_logit)
    kv = _mem_kv(mem, mem_norm_g, w_mem_kv.astype(BF16))

    w_pool = _pool_weights(w_pool_grp, pool_scale, w_pool_o)
    w_in_b = w_in.astype(BF16)
    w_ret_o_b = w_ret_o.astype(BF16)
    w_mem_o_b = w_mem_o.astype(BF16)
    w_out_b = w_out.astype(BF16)
    w_ff1_b = w_ff1.astype(BF16)
    w_ff2_b = w_ff2.astype(BF16)
    gains1 = norm1_g.reshape(depth, 1, d)
    gains2 = norm2_g.reshape(depth, 1, d)

    x2 = x.reshape(batch * seq, d)
    for l in range(depth):
        qk, v, g, p, qm = _in_proj(x2, gains1, w_in_b, cos, sin, l, seq)
        states = _ret_states(qk, v, zeta_f, zeta_b, dec, l, batch, seq)
        x2 = _mix(x2, gains1, qk, v, g, p, inv_count, qm, states, kv, dmat, xi_f, xi_b, w_in_b, w_ret_o_b, w_pool,
                  w_mem_o_b, w_out_b, l, batch, seq)
        x2 = _mlp(x2, gains2, w_ff1_b, w_ff2_b, final_norm_g, l, l == depth - 1)
    return x2.reshape(batch, seq, d)
```

```python
import functools

import jax
import jax.numpy as jnp
from jax import lax
from jax.experimental import pallas as pl
from jax.experimental.pallas import tpu as pltpu

F32 = jnp.float32
BF16 = jnp.bfloat16

RET_HEADS = 4
RET_QK_DIM = 128
RET_V_DIM = 256
MEM_HEADS = 4
MEM_HEAD_DIM = 128
POOL_WINDOWS = (2, 4, 8, 16)
POOL_GROUP = 128
N_BRANCHES = 3
ROPE_BASE = 10000.0
EPS = 1e-6
LOG2_E = 1.4426950408889634

RET_QK_W = RET_HEADS * RET_QK_DIM
RET_V_W = RET_HEADS * RET_V_DIM
MEM_Q_W = MEM_HEADS * MEM_HEAD_DIM
POOL_W = POOL_GROUP * len(POOL_WINDOWS)
POOL_HALO = max(POOL_WINDOWS) // 2
GATE_COL0 = 2 * RET_QK_W + 2 * RET_V_W + POOL_W + MEM_Q_W

V7X_LANES = 128
V7X_VMEM_BYTES = 64 * 1024 * 1024
VMEM_LIMIT_BYTES = V7X_VMEM_BYTES - 4 * 1024 * 1024

RET_CHUNK = 256
ROW_TILE = 1024
MLP_TILE = 1024
MIX_TILE = 512
COL_CHUNK = 512
IN_PROJ_CHUNK = 256
MIX_FILL_CHUNK = 256


def _params(*semantics):
    return pltpu.CompilerParams(dimension_semantics=semantics, vmem_limit_bytes=VMEM_LIMIT_BYTES)


def _resident(shape):
    zeros = (0,) * len(shape)
    return pl.BlockSpec(shape, lambda *_: zeros, pipeline_mode=pl.Buffered(1))


def _layer_resident(stacked_shape, layer):
    index = (layer,) + (0,) * (len(stacked_shape) - 1)
    return pl.BlockSpec((None,) + tuple(stacked_shape[1:]), lambda *_: index, pipeline_mode=pl.Buffered(1))


def _rms_norm(x, gain):
    return x * lax.rsqrt(jnp.mean(x * x, axis=-1, keepdims=True) + EPS) * gain


def _sigmoid(x):
    return 1.0 / (1.0 + jnp.exp(-x))


_dot = functools.partial(jnp.dot, preferred_element_type=F32)
_dot_nt = functools.partial(lax.dot_general, dimension_numbers=(((1,), (1,)), ((), ())), preferred_element_type=F32)
_dot_tn = functools.partial(lax.dot_general, dimension_numbers=(((0,), (0,)), ((), ())), preferred_element_type=F32)


def _decay_tables_kernel(logit_ref, dmat_ref, xi_f_ref, xi_b_ref, zeta_f_ref, zeta_b_ref, dec_ref):
    c = RET_CHUNK
    logit = logit_ref[0]
    log_g = jnp.minimum(logit, 0.0) - jnp.log1p(jnp.exp(-jnp.abs(logit)))
    lg_f, lg_b = log_g[0], log_g[1]
    row = lax.broadcasted_iota(jnp.int32, (c, RET_QK_W), 0).astype(F32)
    xi_f_ref[0] = jnp.exp(lg_f * (row + 1.0))
    xi_b_ref[0] = jnp.exp(lg_b * (c - row))
    zeta_f_ref[0] = jnp.exp(lg_f * (c - 1.0 - row))
    zeta_b_ref[0] = jnp.exp(lg_b * row)
    dec_ref[0, 0] = jnp.exp(lg_f * c)
    dec_ref[0, 1] = jnp.exp(lg_b * c)
    i = lax.broadcasted_iota(jnp.int32, (c, c), 0)
    j = lax.broadcasted_iota(jnp.int32, (c, c), 1)
    diff = (i - j).astype(F32)
    for h in range(RET_HEADS):
        lf = lg_f[:, h * RET_QK_DIM:h * RET_QK_DIM + 1]
        lb = lg_b[:, h * RET_QK_DIM:h * RET_QK_DIM + 1]
        fwd = jnp.exp(lf * jnp.maximum(diff, 0.0))
        bwd = jnp.exp(lb * jnp.maximum(-diff, 0.0))
        dmat_ref[0, h] = jnp.where(diff >= 0.0, fwd, bwd)


def _decay_tables(ret_decay_logit):
    depth = ret_decay_logit.shape[0]
    c = RET_CHUNK
    logit = jnp.repeat(ret_decay_logit.astype(F32), RET_QK_DIM, axis=-1)[:, :, None, :]
    vec = jax.ShapeDtypeStruct((depth, c, RET_QK_W), F32)
    vec_spec = pl.BlockSpec((1, c, RET_QK_W), lambda l: (l, 0, 0))
    return pl.pallas_call(
        _decay_tables_kernel,
        grid=(depth,),
        in_specs=[pl.BlockSpec((1, 2, 1, RET_QK_W), lambda l: (l, 0, 0, 0))],
        out_specs=[pl.BlockSpec((1, RET_HEADS, c, c), lambda l: (l, 0, 0, 0)),
                   vec_spec, vec_spec, vec_spec, vec_spec,
                   pl.BlockSpec((1, 2, 1, RET_QK_W), lambda l: (l, 0, 0, 0))],
        out_shape=[jax.ShapeDtypeStruct((depth, RET_HEADS, c, c), F32), vec, vec, vec, vec,
                   jax.ShapeDtypeStruct((depth, 2, 1, RET_QK_W), F32)],
        compiler_params=_params("arbitrary"),
        name="decay_tables",
    )(logit)


def _mem_kv_kernel(mem_ref, gain_ref, w_ref, kv_ref):
    mem_n = _rms_norm(mem_ref[0], gain_ref[...]).astype(BF16)
    for l in range(w_ref.shape[0]):
        kv_ref[l, 0] = _dot(mem_n, w_ref[l]).astype(BF16)


def _mem_kv(mem, mem_norm_g, w_mem_kv):
    b, m, d = mem.shape
    depth, _, kvw = w_mem_kv.shape
    return pl.pallas_call(
        _mem_kv_kernel,
        grid=(b,),
        in_specs=[pl.BlockSpec((1, m, d), lambda i: (i, 0, 0)),
                  _resident((1, d)),
                  _resident((depth, d, kvw))],
        out_specs=pl.BlockSpec((depth, 1, m, kvw), lambda i: (0, i, 0, 0)),
        out_shape=jax.ShapeDtypeStruct((depth, b, m, kvw), BF16),
        compiler_params=_params("arbitrary"),
        name="mem_kv",
    )(mem, mem_norm_g.reshape(1, d), w_mem_kv)


def _chunk_outer(k, v_ref, rows, zeta):
    kz = (k.astype(F32) * zeta).astype(BF16)
    return [_dot_tn(kz[:, h * RET_QK_DIM:(h + 1) * RET_QK_DIM], v_ref[rows, h * RET_V_DIM:(h + 1) * RET_V_DIM])
            for h in range(RET_HEADS)]


def _decay_and_add(acc, outer, dec):
    for h in range(RET_HEADS):
        acc[h] = acc[h] * dec[:, h * RET_QK_DIM:h * RET_QK_DIM + 1] + outer[h]


def _in_proj_kernel(x_ref, gain_ref, w_ref, cos_ref, sin_ref, zeta_f_ref, dec_ref,
                    qk_ref, v_ref, g_ref, p_ref, qm_ref, state_f_ref, acc, outer_scr, *, tiles_per_seq):
    @pl.when(pl.program_id(0) % tiles_per_seq == 0)
    def _():
        acc[...] = jnp.zeros_like(acc)

    x = x_ref[...]
    h = (x * gain_ref[...]).astype(BF16)
    inv_rms = lax.rsqrt(jnp.mean(x * x, axis=-1, keepdims=True) + EPS)
    cos = cos_ref[...]
    sin = sin_ref[...]

    def rotary(a, scale):
        heads = []
        for hd in range(a.shape[1] // RET_QK_DIM):
            ah = a[:, hd * RET_QK_DIM:(hd + 1) * RET_QK_DIM]
            heads.append(ah * cos + pltpu.roll(ah, RET_QK_DIM // 2, axis=1) * sin)
        rotated = jnp.concatenate(heads, axis=1)
        return rotated if scale is None else rotated * scale

    groups = [
        (qk_ref, 0, RET_QK_W, lambda a: rotary(a, None)),
        (qk_ref, RET_QK_W, RET_QK_W, lambda a: rotary(a, RET_QK_DIM ** -0.5)),
        (v_ref, 0, RET_V_W, lambda a: a),
        (g_ref, 0, RET_V_W, lambda a: a * _sigmoid(a)),
        (p_ref, 0, POOL_W, lambda a: a),
        (qm_ref, 0, MEM_Q_W, lambda a: a),
    ]
    chunk = RET_CHUNK
    n_chunks = x_ref.shape[0] // chunk
    outer_jobs = list(range(n_chunks))

    def outer_job():
        ci = outer_jobs.pop(0)
        rows = slice(ci * chunk, (ci + 1) * chunk)
        for hd, outer in enumerate(_chunk_outer(qk_ref[rows, RET_QK_W:], v_ref, rows, zeta_f_ref[0])):
            outer_scr[ci, hd] = outer

    def scan():
        for ci in range(n_chunks):
            for hd in range(RET_HEADS):
                state_f_ref[ci, hd] = acc[hd].astype(BF16)
            _decay_and_add(acc, outer_scr[ci], dec_ref[0, 0])

    w_col = 0
    scanned = False
    for out_ref, out_col, width, epilogue in groups:
        for c in range(0, width, IN_PROJ_CHUNK):
            a = _dot(h, w_ref[:, w_col + c:w_col + c + IN_PROJ_CHUNK]) * inv_rms
            out_ref[:, out_col + c:out_col + c + IN_PROJ_CHUNK] = epilogue(a).astype(out_ref.dtype)
            if out_ref is not qk_ref and out_ref is not v_ref:
                if outer_jobs:
                    outer_job()
                elif not scanned:
                    scan()
                    scanned = True
        w_col += width
    assert scanned and not outer_jobs


def _in_proj(x2, gains, w_in, cos, sin, zeta_f, dec, layer, seq):
    t, d = x2.shape
    tm = ROW_TILE
    c = RET_CHUNK
    pos_tiles = seq // tm
    w_cols = GATE_COL0

    def rows(width):
        return pl.BlockSpec((tm, width), lambda i: (i, 0))

    pos_spec = pl.BlockSpec((tm, RET_QK_DIM), lambda i: (i % pos_tiles, 0))
    w_spec = pl.BlockSpec((None, d, w_cols), lambda i: (layer, 0, 0), pipeline_mode=pl.Buffered(1))
    state_shape = (RET_HEADS, RET_QK_DIM, RET_V_DIM)
    return pl.pallas_call(
        functools.partial(_in_proj_kernel, tiles_per_seq=pos_tiles),
        grid=(t // tm,),
        in_specs=[rows(d), _layer_resident(gains.shape, layer), w_spec, pos_spec, pos_spec,
                  pl.BlockSpec((1, c, RET_QK_W), lambda i: (layer, 0, 0), pipeline_mode=pl.Buffered(1)),
                  pl.BlockSpec((1, 2, 1, RET_QK_W), lambda i: (layer, 0, 0, 0), pipeline_mode=pl.Buffered(1))],
        out_specs=[rows(2 * RET_QK_W), rows(RET_V_W), rows(RET_V_W), rows(POOL_W), rows(MEM_Q_W),
                   pl.BlockSpec((tm // c,) + state_shape, lambda i: (i, 0, 0, 0))],
        out_shape=[jax.ShapeDtypeStruct((t, 2 * RET_QK_W), BF16),
                   jax.ShapeDtypeStruct((t, RET_V_W), BF16),
                   jax.ShapeDtypeStruct((t, RET_V_W), BF16),
                   jax.ShapeDtypeStruct((t, POOL_W), F32),
                   jax.ShapeDtypeStruct((t, MEM_Q_W), BF16),
                   jax.ShapeDtypeStruct((t // c,) + state_shape, BF16)],
        scratch_shapes=[pltpu.VMEM(state_shape, F32), pltpu.VMEM((tm // c,) + state_shape, F32)],
        compiler_params=_params("arbitrary"),
        name="in_proj",
    )(x2, gains, w_in, cos, sin, zeta_f, dec)


def _window_sums(padded):
    length = padded.shape[0]
    halo = POOL_HALO
    ts = length - 2 * halo

    def ahead(a, k):
        return pltpu.roll(a, length - k, axis=0)

    def behind(a, k):
        return pltpu.roll(a, k, axis=0)

    sums = []
    for gi, w in enumerate(POOL_WINDOWS):
        a = padded[:, gi * POOL_GROUP:(gi + 1) * POOL_GROUP]
        span = 1
        while 2 * span < w:
            a = a + ahead(a, span)
            span *= 2
        assert 2 * span == w and span <= halo
        sums.append((a + behind(a, span))[halo:halo + ts, :])
    return sums


def _mix_kernel(x_ref, gain_ref, qk_ref, v_ref, g_ref, p_ref, p_prev_ref, p_next_ref, inv_count_ref, qm_ref,
                state_f_ref, kv_ref, dmat_ref, xi_f_ref, xi_b_ref, zeta_b_ref, dec_ref,
                w_gate_ret_ref, w_gate_pool_ref, w_gate_mem_ref,
                w_ret_o_ref, w_pool_ref, w_mem_o_ref, w_out_ref,
                out_ref, ret_scr, gate_scr, acc_b):
    ts = x_ref.shape[0]
    c = RET_CHUNK
    n_tiles = pl.num_programs(1)
    tile = n_tiles - 1 - pl.program_id(1)

    @pl.when(pl.program_id(1) == 0)
    def _():
        acc_b[...] = jnp.zeros_like(acc_b)

    d = x_ref.shape[1]
    fill_cols = list(range(0, d, MIX_FILL_CHUNK))

    def out_proj_chunk(lhs, w_ref, col):
        return _dot(lhs, w_ref[:, col:col + MIX_FILL_CHUNK]).astype(BF16)

    x = x_ref[...]
    x_gained = (x * gain_ref[...]).astype(BF16)
    half_inv_rms = 0.5 * lax.rsqrt(jnp.mean(x * x, axis=-1, keepdims=True) + EPS)
    gate_w_refs = (w_gate_ret_ref, w_gate_pool_ref, w_gate_mem_ref)
    gate_jobs = [(b, col) for b in range(N_BRANCHES) for col in fill_cols]

    def gate_job():
        b, col = gate_jobs.pop(0)
        z_half = _dot(x_gained, gate_w_refs[b][:, col:col + MIX_FILL_CHUNK]) * half_inv_rms
        gate_scr[:, b * d + col:b * d + col + MIX_FILL_CHUNK] = jnp.tanh(z_half).astype(BF16) * 0.5 + 0.5

    xi_f = xi_f_ref[0]
    xi_b = xi_b_ref[0]
    for ci in reversed(range(ts // c)):
        rows = slice(ci * c, (ci + 1) * c)
        q = qk_ref[rows, :RET_QK_W]
        q32 = q.astype(F32)
        q_f = (q32 * xi_f).astype(BF16)
        q_b = (q32 * xi_b).astype(BF16)
        for h in range(RET_HEADS):
            qk_cols = slice(h * RET_QK_DIM, (h + 1) * RET_QK_DIM)
            v_cols = slice(h * RET_V_DIM, (h + 1) * RET_V_DIM)
            k_h = qk_ref[rows, RET_QK_W + h * RET_QK_DIM:RET_QK_W + (h + 1) * RET_QK_DIM]
            s = _dot_nt(q[:, qk_cols], k_h) * dmat_ref[0, h]
            q_fb = jnp.concatenate([q_f[:, qk_cols], q_b[:, qk_cols]], axis=1)
            state = jnp.concatenate([state_f_ref[ci, h], acc_b[h].astype(BF16)], axis=0)
            o = _dot(s.astype(BF16), v_ref[rows, v_cols]) + _dot(q_fb, state)
            mu = jnp.mean(o, axis=-1, keepdims=True)
            cen = o - mu
            var = jnp.mean(cen * cen, axis=-1, keepdims=True)
            o_n = cen * lax.rsqrt(var + EPS)
            ret_scr[rows, v_cols] = (o_n * g_ref[rows, v_cols].astype(F32)).astype(BF16)
            if gate_jobs:
                gate_job()
        _decay_and_add(acc_b, _chunk_outer(qk_ref[rows, RET_QK_W:], v_ref, rows, zeta_b_ref[0]), dec_ref[0, 1])

    p = p_ref[...]
    padded = jnp.concatenate([jnp.where(tile > 0, p_prev_ref[...], 0.0), p,
                              jnp.where(tile < n_tiles - 1, p_next_ref[...], 0.0)], axis=0)
    ret_lhs = ret_scr[...]
    groups, o_ret = [], []
    for gi, win in enumerate(_window_sums(padded)):
        cols = slice(gi * POOL_GROUP, (gi + 1) * POOL_GROUP)
        groups.append((win * inv_count_ref[:, cols] - p[:, cols]).astype(BF16))
        if gi < len(fill_cols):
            o_ret.append(out_proj_chunk(ret_lhs, w_ret_o_ref, fill_cols[gi]))
    o_ret += [out_proj_chunk(ret_lhs, w_ret_o_ref, col) for col in fill_cols[len(o_ret):]]
    pool_lhs = jnp.concatenate(groups, axis=1)

    exp2_scale = (MEM_HEAD_DIM ** -0.5) * LOG2_E
    heads, o_pool = [], []
    for h in range(MEM_HEADS):
        cols = slice(h * MEM_HEAD_DIM, (h + 1) * MEM_HEAD_DIM)
        k_h = kv_ref[0, :, cols]
        v_h = kv_ref[0, :, MEM_Q_W + h * MEM_HEAD_DIM:MEM_Q_W + (h + 1) * MEM_HEAD_DIM]
        s = _dot_nt(qm_ref[:, cols], k_h)
        e = jnp.exp2((s - jnp.max(s, axis=-1, keepdims=True)) * exp2_scale)
        o = _dot(e.astype(BF16), v_h) / jnp.sum(e, axis=-1, keepdims=True)
        heads.append(o.astype(BF16))
        if gate_jobs:
            gate_job()
        if h < len(fill_cols):
            o_pool.append(out_proj_chunk(pool_lhs, w_pool_ref, fill_cols[h]))
    while gate_jobs:
        gate_job()
    o_pool += [out_proj_chunk(pool_lhs, w_pool_ref, col) for col in fill_cols[len(o_pool):]]
    mem_lhs = jnp.concatenate(heads, axis=1)

    merged = []
    for j, col in enumerate(fill_cols):
        o_mem = out_proj_chunk(mem_lhs, w_mem_o_ref, col)
        gates = [gate_scr[:, b * d + col:b * d + col + MIX_FILL_CHUNK] for b in range(N_BRANCHES)]
        merged.append(gates[0] * o_ret[j] + gates[1] * o_pool[j] + gates[2] * o_mem)
    out_ref[...] = x + _dot(jnp.concatenate(merged, axis=1), w_out_ref[...])


def _mix(x2, gains, qk, v, g, p, inv_count, qm, states_f, kv, dmat, xi_f, xi_b, zeta_b, dec, w_in, w_ret_o, w_pool,
         w_mem_o, w_out, layer, batch, seq):
    t, d = x2.shape
    assert GATE_COL0 % d == 0

    def gate_weight(branch):
        return pl.BlockSpec((None, d, d), lambda b, i: (layer, 0, GATE_COL0 // d + branch),
                            pipeline_mode=pl.Buffered(1))

    ts = MIX_TILE
    c = RET_CHUNK
    nt = seq // ts
    halo = POOL_HALO
    halo_per_tile = ts // halo
    last_halo = t // halo - 1

    def tile_row(b, i):
        return b * nt + nt - 1 - i

    def rows(width):
        return pl.BlockSpec((ts, width), lambda b, i: (tile_row(b, i), 0))

    p_prev = pl.BlockSpec((halo, POOL_W), lambda b, i: (jnp.maximum(tile_row(b, i) * halo_per_tile - 1, 0), 0))
    p_next = pl.BlockSpec((halo, POOL_W),
                          lambda b, i: (jnp.minimum((tile_row(b, i) + 1) * halo_per_tile, last_halo), 0))
    state = pl.BlockSpec((ts // c,) + states_f.shape[1:], lambda b, i: (tile_row(b, i), 0, 0, 0))
    kv_spec = pl.BlockSpec((1,) + kv.shape[2:], lambda b, i: (layer * batch + b, 0, 0))
    dmat_spec = pl.BlockSpec((1, RET_HEADS, c, c), lambda b, i: (layer, 0, 0, 0), pipeline_mode=pl.Buffered(1))
    table_spec = pl.BlockSpec((1, c, RET_QK_W), lambda b, i: (layer, 0, 0), pipeline_mode=pl.Buffered(1))
    dec_spec = pl.BlockSpec((1, 2, 1, RET_QK_W), lambda b, i: (layer, 0, 0, 0), pipeline_mode=pl.Buffered(1))
    return pl.pallas_call(
        _mix_kernel,
        grid=(batch, nt),
        in_specs=[rows(d), _layer_resident(gains.shape, layer), rows(2 * RET_QK_W), rows(RET_V_W), rows(RET_V_W),
                  rows(POOL_W), p_prev, p_next, pl.BlockSpec((ts, POOL_W), lambda b, i: (nt - 1 - i, 0)),
                  rows(MEM_Q_W), state, kv_spec, dmat_spec, table_spec, table_spec, table_spec, dec_spec,
                  gate_weight(0), gate_weight(1), gate_weight(2),
                  _layer_resident(w_ret_o.shape, layer), _layer_resident(w_pool.shape, layer),
                  _layer_resident(w_mem_o.shape, layer), _layer_resident(w_out.shape, layer)],
        out_specs=rows(d),
        out_shape=jax.ShapeDtypeStruct((t, d), F32),
        scratch_shapes=[pltpu.VMEM((ts, RET_V_W), BF16), pltpu.VMEM((ts, N_BRANCHES * d), BF16),
                        pltpu.VMEM((RET_HEADS, RET_QK_DIM, RET_V_DIM), F32)],
        compiler_params=_params("arbitrary", "arbitrary"),
        name="mix",
    )(x2, gains, qk, v, g, p, p, p, inv_count, qm, states_f, kv.reshape((-1,) + kv.shape[2:]), dmat, xi_f, xi_b,
      zeta_b, dec, w_in, w_in, w_in, w_ret_o, w_pool, w_mem_o, w_out)


def _mlp_kernel(x_ref, gain_ref, w1_ref, w2_ref, final_gain_ref, out_ref, *, final_norm):
    x = x_ref[...]
    h = (x * gain_ref[...]).astype(BF16)
    acc = None
    for col in range(0, w1_ref.shape[1], COL_CHUNK):
        hid = jnp.maximum(_dot(h, w1_ref[:, col:col + COL_CHUNK]), 0.0)
        part = _dot((hid * hid).astype(BF16), w2_ref[col:col + COL_CHUNK, :])
        acc = part if acc is None else acc + part
    out = x + acc / (jnp.mean(x * x, axis=-1, keepdims=True) + EPS)
    out_ref[...] = _rms_norm(out, final_gain_ref[...]) if final_norm else out


def _mlp(x2, gains, w1, w2, final_gain, layer, final_norm):
    t, d = x2.shape
    tm = MLP_TILE
    rows = pl.BlockSpec((tm, d), lambda i: (i, 0))
    return pl.pallas_call(
        functools.partial(_mlp_kernel, final_norm=final_norm),
        grid=(t // tm,),
        in_specs=[rows, _layer_resident(gains.shape, layer), _layer_resident(w1.shape, layer),
                  _layer_resident(w2.shape, layer), _resident((1, d))],
        out_specs=rows,
        out_shape=jax.ShapeDtypeStruct((t, d), F32),
        compiler_params=_params("arbitrary"),
        name="mlp",
    )(x2, gains, w1, w2, final_gain.reshape(1, d))


def _pool_weight_kernel(w_grp_ref, scale_ref, w_o_ref, out_ref):
    for gi in range(len(POOL_WINDOWS)):
        rows = slice(gi * POOL_GROUP, (gi + 1) * POOL_GROUP)
        scaled = w_grp_ref[0, gi] * scale_ref[0, :, rows]
        out_ref[0, rows, :] = jnp.dot(scaled, w_o_ref[0, rows, :], preferred_element_type=F32,
                                      precision=lax.Precision.HIGHEST).astype(BF16)


def _pool_weights(w_pool_grp, pool_scale, w_pool_o):
    depth, groups, group_w, _ = w_pool_grp.shape
    d = w_pool_o.shape[2]
    return pl.pallas_call(
        _pool_weight_kernel,
        grid=(depth,),
        in_specs=[pl.BlockSpec((1, groups, group_w, group_w), lambda l: (l, 0, 0, 0)),
                  pl.BlockSpec((1, 1, POOL_W), lambda l: (l, 0, 0)),
                  pl.BlockSpec((1, POOL_W, d), lambda l: (l, 0, 0))],
        out_specs=pl.BlockSpec((1, POOL_W, d), lambda l: (l, 0, 0)),
        out_shape=jax.ShapeDtypeStruct((depth, POOL_W, d), BF16),
        compiler_params=_params("arbitrary"),
        name="pool_weights",
    )(w_pool_grp, pool_scale.reshape(depth, 1, POOL_W), w_pool_o)


def _rotary_tables(seq):
    inv = ROPE_BASE ** (-jnp.arange(0, RET_QK_DIM, 2, dtype=F32) / RET_QK_DIM)
    ang = jnp.arange(seq, dtype=F32)[:, None] * inv[None, :]
    cos, sin = jnp.cos(ang), jnp.sin(ang)
    return jnp.concatenate([cos, cos], axis=1), jnp.concatenate([-sin, sin], axis=1)


def _pool_inv_counts(seq):
    pos = jnp.arange(seq)
    cols = []
    for w in POOL_WINDOWS:
        count = jnp.minimum(pos + w // 2, seq) - jnp.maximum(pos - w // 2, 0)
        cols.append(jnp.broadcast_to((1.0 / count.astype(F32))[:, None], (seq, POOL_GROUP)))
    return jnp.concatenate(cols, axis=1)


def kernel(x, mem, w_in, ret_decay_logit, w_ret_o, w_pool_grp, pool_scale, w_pool_o, w_mem_kv, w_mem_o,
           w_out, w_ff1, w_ff2, norm1_g, norm2_g, mem_norm_g, final_norm_g):
    batch, seq, d = x.shape
    depth = w_in.shape[0]
    assert seq % MIX_TILE == 0 and seq % ROW_TILE == 0 and MIX_TILE % RET_CHUNK == 0
    assert d % COL_CHUNK == 0 and POOL_HALO % 8 == 0

    cos, sin = _rotary_tables(seq)
    inv_count = _pool_inv_counts(seq)
    dmat, xi_f, xi_b, zeta_f, zeta_b, dec = _decay_tables(ret_decay_logit)
    kv = _mem_kv(mem, mem_norm_g, w_mem_kv.astype(BF16))

    w_pool = _pool_weights(w_pool_grp, pool_scale, w_pool_o)
    w_in_b = w_in.astype(BF16)
    w_ret_o_b = w_ret_o.astype(BF16)
    w_mem_o_b = w_mem_o.astype(BF16)
    w_out_b = w_out.astype(BF16)
    w_ff1_b = w_ff1.astype(BF16)
    w_ff2_b = w_ff2.astype(BF16)
    gains1 = norm1_g.reshape(depth, 1, d)
    gains2 = norm2_g.reshape(depth, 1, d)

    x2 = x.reshape(batch * seq, d)
    for l in range(depth):
        qk, v, g, p, qm, states_f = _in_proj(x2, gains1, w_in_b, cos, sin, zeta_f, dec, l, seq)
        x2 = _mix(x2, gains1, qk, v, g, p, inv_count, qm, states_f, kv, dmat, xi_f, xi_b, zeta_b, dec, w_in_b,
                  w_ret_o_b, w_pool, w_mem_o_b, w_out_b, l, batch, seq)
        x2 = _mlp(x2, gains2, w_ff1_b, w_ff2_b, final_norm_g, l, l == depth - 1)
    return x2.reshape(batch, seq, d)
```

```python
import functools

import jax
import jax.numpy as jnp
from jax import lax
from jax.experimental import pallas as pl
from jax.experimental.pallas import tpu as pltpu

F32 = jnp.float32
BF16 = jnp.bfloat16

RET_HEADS = 4
RET_QK_DIM = 128
RET_V_DIM = 256
MEM_HEADS = 4
MEM_HEAD_DIM = 128
POOL_WINDOWS = (2, 4, 8, 16)
POOL_GROUP = 128
N_BRANCHES = 3
ROPE_BASE = 10000.0
EPS = 1e-6
LOG2_E = 1.4426950408889634

RET_QK_W = RET_HEADS * RET_QK_DIM
RET_V_W = RET_HEADS * RET_V_DIM
MEM_Q_W = MEM_HEADS * MEM_HEAD_DIM
POOL_W = POOL_GROUP * len(POOL_WINDOWS)
POOL_HALO = max(POOL_WINDOWS) // 2
GATE_COL0 = 2 * RET_QK_W + 2 * RET_V_W + POOL_W + MEM_Q_W

V7X_LANES = 128
V7X_VMEM_BYTES = 64 * 1024 * 1024
VMEM_LIMIT_BYTES = V7X_VMEM_BYTES - 4 * 1024 * 1024

RET_CHUNK = 256
ROW_TILE = 1024
MLP_TILE = 1024
MIX_TILE = 512
COL_CHUNK = 512
IN_PROJ_CHUNK = 256
MIX_FILL_CHUNK = 256


def _params(*semantics):
    return pltpu.CompilerParams(dimension_semantics=semantics, vmem_limit_bytes=VMEM_LIMIT_BYTES)


def _resident(shape):
    zeros = (0,) * len(shape)
    return pl.BlockSpec(shape, lambda *_: zeros, pipeline_mode=pl.Buffered(1))


def _layer_resident(stacked_shape, layer):
    index = (layer,) + (0,) * (len(stacked_shape) - 1)
    return pl.BlockSpec((None,) + tuple(stacked_shape[1:]), lambda *_: index, pipeline_mode=pl.Buffered(1))


def _rms_norm(x, gain):
    return x * lax.rsqrt(jnp.mean(x * x, axis=-1, keepdims=True) + EPS) * gain


def _sigmoid(x):
    return 1.0 / (1.0 + jnp.exp(-x))


_dot = functools.partial(jnp.dot, preferred_element_type=F32)
_dot_nt = functools.partial(lax.dot_general, dimension_numbers=(((1,), (1,)), ((), ())), preferred_element_type=F32)
_dot_tn = functools.partial(lax.dot_general, dimension_numbers=(((0,), (0,)), ((), ())), preferred_element_type=F32)


def _decay_tables_kernel(logit_ref, dmat_ref, xi_f_ref, xi_b_ref, zeta_f_ref, zeta_b_ref, dec_ref):
    c = RET_CHUNK
    logit = logit_ref[0]
    log_g = jnp.minimum(logit, 0.0) - jnp.log1p(jnp.exp(-jnp.abs(logit)))
    lg_f, lg_b = log_g[0], log_g[1]
    row = lax.broadcasted_iota(jnp.int32, (c, RET_QK_W), 0).astype(F32)
    xi_f_ref[0] = jnp.exp(lg_f * (row + 1.0))
    xi_b_ref[0] = jnp.exp(lg_b * (c - row))
    zeta_f_ref[0] = jnp.exp(lg_f * (c - 1.0 - row))
    zeta_b_ref[0] = jnp.exp(lg_b * row)
    dec_ref[0, 0] = jnp.exp(lg_f * c)
    dec_ref[0, 1] = jnp.exp(lg_b * c)
    i = lax.broadcasted_iota(jnp.int32, (c, c), 0)
    j = lax.broadcasted_iota(jnp.int32, (c, c), 1)
    diff = (i - j).astype(F32)
    for h in range(RET_HEADS):
        lf = lg_f[:, h * RET_QK_DIM:h * RET_QK_DIM + 1]
        lb = lg_b[:, h * RET_QK_DIM:h * RET_QK_DIM + 1]
        fwd = jnp.exp(lf * jnp.maximum(diff, 0.0))
        bwd = jnp.exp(lb * jnp.maximum(-diff, 0.0))
        dmat_ref[0, h] = jnp.where(diff >= 0.0, fwd, bwd)


def _decay_tables(ret_decay_logit):
    depth = ret_decay_logit.shape[0]
    c = RET_CHUNK
    logit = jnp.repeat(ret_decay_logit.astype(F32), RET_QK_DIM, axis=-1)[:, :, None, :]
    vec = jax.ShapeDtypeStruct((depth, c, RET_QK_W), F32)
    vec_spec = pl.BlockSpec((1, c, RET_QK_W), lambda l: (l, 0, 0))
    return pl.pallas_call(
        _decay_tables_kernel,
        grid=(depth,),
        in_specs=[pl.BlockSpec((1, 2, 1, RET_QK_W), lambda l: (l, 0, 0, 0))],
        out_specs=[pl.BlockSpec((1, RET_HEADS, c, c), lambda l: (l, 0, 0, 0)),
                   vec_spec, vec_spec, vec_spec, vec_spec,
                   pl.BlockSpec((1, 2, 1, RET_QK_W), lambda l: (l, 0, 0, 0))],
        out_shape=[jax.ShapeDtypeStruct((depth, RET_HEADS, c, c), F32), vec, vec, vec, vec,
                   jax.ShapeDtypeStruct((depth, 2, 1, RET_QK_W), F32)],
        compiler_params=_params("arbitrary"),
        name="decay_tables",
    )(logit)


def _mem_kv_kernel(mem_ref, gain_ref, w_ref, kv_ref):
    mem_n = _rms_norm(mem_ref[...], gain_ref[...]).astype(BF16)
    for l in range(w_ref.shape[0]):
        kv_ref[l] = _dot(mem_n, w_ref[l]).astype(BF16)


def _mem_kv(mem, mem_norm_g, w_mem_kv):
    b, m, d = mem.shape
    depth, _, kvw = w_mem_kv.shape
    rows = b * m
    tm = min(MLP_TILE, rows)
    assert rows % tm == 0
    kv = pl.pallas_call(
        _mem_kv_kernel,
        grid=(rows // tm,),
        in_specs=[pl.BlockSpec((tm, d), lambda i: (i, 0)),
                  _resident((1, d)),
                  _resident((depth, d, kvw))],
        out_specs=pl.BlockSpec((depth, tm, kvw), lambda i: (0, i, 0)),
        out_shape=jax.ShapeDtypeStruct((depth, rows, kvw), BF16),
        compiler_params=_params("arbitrary"),
        name="mem_kv",
    )(mem.reshape(rows, d), mem_norm_g.reshape(1, d), w_mem_kv)
    return kv.reshape(depth, b, m, kvw)


def _chunk_outer(k, v_ref, rows, zeta):
    kz = (k.astype(F32) * zeta).astype(BF16)
    return [_dot_tn(kz[:, h * RET_QK_DIM:(h + 1) * RET_QK_DIM], v_ref[rows, h * RET_V_DIM:(h + 1) * RET_V_DIM])
            for h in range(RET_HEADS)]


def _decay_and_add(acc, outer, dec):
    for h in range(RET_HEADS):
        acc[h] = acc[h] * dec[:, h * RET_QK_DIM:h * RET_QK_DIM + 1] + outer[h]


def _in_proj_kernel(x_ref, gain_ref, w_ref, cos_ref, sin_ref, zeta_f_ref, dec_ref,
                    qk_ref, v_ref, g_ref, p_ref, qm_ref, state_f_ref, acc, outer_scr, *, tiles_per_seq):
    @pl.when(pl.program_id(0) % tiles_per_seq == 0)
    def _():
        acc[...] = jnp.zeros_like(acc)

    x = x_ref[...]
    h = (x * gain_ref[...]).astype(BF16)
    inv_rms = lax.rsqrt(jnp.mean(x * x, axis=-1, keepdims=True) + EPS)
    cos = cos_ref[...]
    sin = sin_ref[...]

    def rotary(a, scale):
        heads = []
        for hd in range(a.shape[1] // RET_QK_DIM):
            ah = a[:, hd * RET_QK_DIM:(hd + 1) * RET_QK_DIM]
            heads.append(ah * cos + pltpu.roll(ah, RET_QK_DIM // 2, axis=1) * sin)
        rotated = jnp.concatenate(heads, axis=1)
        return rotated if scale is None else rotated * scale

    groups = [
        (qk_ref, 0, RET_QK_W, lambda a: rotary(a, None)),
        (qk_ref, RET_QK_W, RET_QK_W, lambda a: rotary(a, RET_QK_DIM ** -0.5)),
        (v_ref, 0, RET_V_W, lambda a: a),
        (g_ref, 0, RET_V_W, lambda a: a * _sigmoid(a)),
        (p_ref, 0, POOL_W, lambda a: a),
        (qm_ref, 0, MEM_Q_W, lambda a: a),
    ]
    chunk = RET_CHUNK
    n_chunks = x_ref.shape[0] // chunk
    outer_jobs = list(range(n_chunks))

    def outer_job():
        ci = outer_jobs.pop(0)
        rows = slice(ci * chunk, (ci + 1) * chunk)
        for hd, outer in enumerate(_chunk_outer(qk_ref[rows, RET_QK_W:], v_ref, rows, zeta_f_ref[0])):
            outer_scr[ci, hd] = outer

    def scan():
        for ci in range(n_chunks):
            for hd in range(RET_HEADS):
                state_f_ref[ci, hd] = acc[hd].astype(BF16)
            _decay_and_add(acc, outer_scr[ci], dec_ref[0, 0])

    w_col = 0
    scanned = False
    for out_ref, out_col, width, epilogue in groups:
        for c in range(0, width, IN_PROJ_CHUNK):
            a = _dot(h, w_ref[:, w_col + c:w_col + c + IN_PROJ_CHUNK]) * inv_rms
            out_ref[:, out_col + c:out_col + c + IN_PROJ_CHUNK] = epilogue(a).astype(out_ref.dtype)
            if out_ref is not qk_ref and out_ref is not v_ref:
                if outer_jobs:
                    outer_job()
                elif not scanned:
                    scan()
                    scanned = True
        w_col += width
    assert scanned and not outer_jobs


def _in_proj(x2, gains, w_in, cos, sin, zeta_f, dec, layer, seq):
    t, d = x2.shape
    tm = ROW_TILE
    c = RET_CHUNK
    pos_tiles = seq // tm
    w_cols = GATE_COL0

    def rows(width):
        return pl.BlockSpec((tm, width), lambda i: (i, 0))

    pos_spec = pl.BlockSpec((tm, RET_QK_DIM), lambda i: (i % pos_tiles, 0))
    w_spec = pl.BlockSpec((None, d, w_cols), lambda i: (layer, 0, 0), pipeline_mode=pl.Buffered(1))
    state_shape = (RET_HEADS, RET_QK_DIM, RET_V_DIM)
    return pl.pallas_call(
        functools.partial(_in_proj_kernel, tiles_per_seq=pos_tiles),
        grid=(t // tm,),
        in_specs=[rows(d), _layer_resident(gains.shape, layer), w_spec, pos_spec, pos_spec,
                  pl.BlockSpec((1, c, RET_QK_W), lambda i: (layer, 0, 0), pipeline_mode=pl.Buffered(1)),
                  pl.BlockSpec((1, 2, 1, RET_QK_W), lambda i: (layer, 0, 0, 0), pipeline_mode=pl.Buffered(1))],
        out_specs=[rows(2 * RET_QK_W), rows(RET_V_W), rows(RET_V_W), rows(POOL_W), rows(MEM_Q_W),
                   pl.BlockSpec((tm // c,) + state_shape, lambda i: (i, 0, 0, 0))],
        out_shape=[jax.ShapeDtypeStruct((t, 2 * RET_QK_W), BF16),
                   jax.ShapeDtypeStruct((t, RET_V_W), BF16),
                   jax.ShapeDtypeStruct((t, RET_V_W), BF16),
                   jax.ShapeDtypeStruct((t, POOL_W), F32),
                   jax.ShapeDtypeStruct((t, MEM_Q_W), BF16),
                   jax.ShapeDtypeStruct((t // c,) + state_shape, BF16)],
        scratch_shapes=[pltpu.VMEM(state_shape, F32), pltpu.VMEM((tm // c,) + state_shape, F32)],
        compiler_params=_params("arbitrary"),
        name="in_proj",
    )(x2, gains, w_in, cos, sin, zeta_f, dec)


def _window_sums(padded):
    length = padded.shape[0]
    halo = POOL_HALO
    ts = length - 2 * halo

    def ahead(a, k):
        return pltpu.roll(a, length - k, axis=0)

    def behind(a, k):
        return pltpu.roll(a, k, axis=0)

    sums = []
    for gi, w in enumerate(POOL_WINDOWS):
        a = padded[:, gi * POOL_GROUP:(gi + 1) * POOL_GROUP]
        span = 1
        while 2 * span < w:
            a = a + ahead(a, span)
            span *= 2
        assert 2 * span == w and span <= halo
        sums.append((a + behind(a, span))[halo:halo + ts, :])
    return sums


def _mix_kernel(x_ref, gain_ref, qk_ref, v_ref, g_ref, p_ref, p_prev_ref, p_next_ref, inv_count_ref, qm_ref,
                state_f_ref, kv_ref, dmat_ref, xi_f_ref, xi_b_ref, zeta_b_ref, dec_ref,
                w_gate_ret_ref, w_gate_pool_ref, w_gate_mem_ref,
                w_ret_o_ref, w_pool_ref, w_mem_o_ref, w_out_ref,
                out_ref, ret_scr, gate_scr, acc_b):
    ts = x_ref.shape[0]
    c = RET_CHUNK
    n_tiles = pl.num_programs(1)
    tile = n_tiles - 1 - pl.program_id(1)

    @pl.when(pl.program_id(1) == 0)
    def _():
        acc_b[...] = jnp.zeros_like(acc_b)

    d = x_ref.shape[1]
    fill_cols = list(range(0, d, MIX_FILL_CHUNK))

    def out_proj_chunk(lhs, w_ref, col):
        return _dot(lhs, w_ref[:, col:col + MIX_FILL_CHUNK]).astype(BF16)

    gate_w_refs = (w_gate_ret_ref, w_gate_pool_ref, w_gate_mem_ref)
    gate_jobs = [(b, col) for b in range(N_BRANCHES) for col in fill_cols]
    half_h = []

    def gate_job():
        if not half_h:
            x = x_ref[...]
            half_inv_rms = 0.5 * lax.rsqrt(jnp.mean(x * x, axis=-1, keepdims=True) + EPS)
            half_h.append((x * half_inv_rms * gain_ref[...]).astype(BF16))
        b, col = gate_jobs.pop(0)
        z_half = _dot(half_h[0], gate_w_refs[b][:, col:col + MIX_FILL_CHUNK])
        gate_scr[:, b * d + col:b * d + col + MIX_FILL_CHUNK] = jnp.tanh(z_half).astype(BF16) * 0.5 + 0.5

    xi_f = xi_f_ref[0].astype(BF16)
    xi_b = xi_b_ref[0].astype(BF16)
    def normalise_and_gate(o, rows, v_cols):
        mu = jnp.mean(o, axis=-1, keepdims=True)
        cen = o - mu
        var = jnp.mean(cen * cen, axis=-1, keepdims=True)
        o_n = cen * lax.rsqrt(var + EPS)
        ret_scr[rows, v_cols] = o_n.astype(BF16) * g_ref[rows, v_cols]

    pending = None
    for ci in reversed(range(ts // c)):
        rows = slice(ci * c, (ci + 1) * c)
        q = qk_ref[rows, :RET_QK_W]
        q_f = q * xi_f
        q_b = q * xi_b
        for h in range(RET_HEADS):
            qk_cols = slice(h * RET_QK_DIM, (h + 1) * RET_QK_DIM)
            v_cols = slice(h * RET_V_DIM, (h + 1) * RET_V_DIM)
            k_h = qk_ref[rows, RET_QK_W + h * RET_QK_DIM:RET_QK_W + (h + 1) * RET_QK_DIM]
            s = _dot_nt(q[:, qk_cols], k_h) * dmat_ref[0, h]
            q_fb = jnp.concatenate([q_f[:, qk_cols], q_b[:, qk_cols]], axis=1)
            state = jnp.concatenate([state_f_ref[ci, h], acc_b[h].astype(BF16)], axis=0)
            o = _dot(s.astype(BF16), v_ref[rows, v_cols]) + _dot(q_fb, state)
            if pending is not None:
                normalise_and_gate(*pending)
                if gate_jobs:
                    gate_job()
            pending = (o, rows, v_cols)
        _decay_and_add(acc_b, _chunk_outer(qk_ref[rows, RET_QK_W:], v_ref, rows, zeta_b_ref[0]), dec_ref[0, 1])
    normalise_and_gate(*pending)
    if gate_jobs:
        gate_job()

    p = p_ref[...]
    padded = jnp.concatenate([jnp.where(tile > 0, p_prev_ref[...], 0.0), p,
                              jnp.where(tile < n_tiles - 1, p_next_ref[...], 0.0)], axis=0)
    ret_lhs = ret_scr[...]
    groups, o_ret = [], []
    for gi, win in enumerate(_window_sums(padded)):
        cols = slice(gi * POOL_GROUP, (gi + 1) * POOL_GROUP)
        groups.append((win * inv_count_ref[:, cols] - p[:, cols]).astype(BF16))
        if gi < len(fill_cols):
            o_ret.append(out_proj_chunk(ret_lhs, w_ret_o_ref, fill_cols[gi]))
    o_ret += [out_proj_chunk(ret_lhs, w_ret_o_ref, col) for col in fill_cols[len(o_ret):]]
    pool_lhs = jnp.concatenate(groups, axis=1)

    exp2_scale = (MEM_HEAD_DIM ** -0.5) * LOG2_E
    heads, o_pool = [], []
    for h in range(MEM_HEADS):
        cols = slice(h * MEM_HEAD_DIM, (h + 1) * MEM_HEAD_DIM)
        k_h = kv_ref[0, :, cols]
        v_h = kv_ref[0, :, MEM_Q_W + h * MEM_HEAD_DIM:MEM_Q_W + (h + 1) * MEM_HEAD_DIM]
        s = _dot_nt(qm_ref[:, cols], k_h)
        e = jnp.exp2((s - jnp.max(s, axis=-1, keepdims=True)) * exp2_scale)
        o = _dot(e.astype(BF16), v_h) / jnp.sum(e, axis=-1, keepdims=True)
        heads.append(o.astype(BF16))
        if gate_jobs:
            gate_job()
        if h < len(fill_cols):
            o_pool.append(out_proj_chunk(pool_lhs, w_pool_ref, fill_cols[h]))
    while gate_jobs:
        gate_job()
    o_pool += [out_proj_chunk(pool_lhs, w_pool_ref, col) for col in fill_cols[len(o_pool):]]
    mem_lhs = jnp.concatenate(heads, axis=1)

    merged = []
    for j, col in enumerate(fill_cols):
        o_mem = out_proj_chunk(mem_lhs, w_mem_o_ref, col)
        gates = [gate_scr[:, b * d + col:b * d + col + MIX_FILL_CHUNK] for b in range(N_BRANCHES)]
        merged.append(gates[0] * o_ret[j] + gates[1] * o_pool[j] + gates[2] * o_mem)
    out_ref[...] = x_ref[...] + _dot(jnp.concatenate(merged, axis=1), w_out_ref[...])


def _mix(x2, gains, qk, v, g, p, inv_count, qm, states_f, kv, dmat, xi_f, xi_b, zeta_b, dec, w_in, w_ret_o, w_pool,
         w_mem_o, w_out, layer, batch, seq):
    t, d = x2.shape
    assert GATE_COL0 % d == 0

    def gate_weight(branch):
        return pl.BlockSpec((None, d, d), lambda b, i: (layer, 0, GATE_COL0 // d + branch),
                            pipeline_mode=pl.Buffered(1))

    ts = MIX_TILE
    c = RET_CHUNK
    nt = seq // ts
    halo = POOL_HALO
    halo_per_tile = ts // halo
    last_halo = t // halo - 1

    def tile_row(b, i):
        return b * nt + nt - 1 - i

    def rows(width):
        return pl.BlockSpec((ts, width), lambda b, i: (tile_row(b, i), 0))

    p_prev = pl.BlockSpec((halo, POOL_W), lambda b, i: (jnp.maximum(tile_row(b, i) * halo_per_tile - 1, 0), 0))
    p_next = pl.BlockSpec((halo, POOL_W),
                          lambda b, i: (jnp.minimum((tile_row(b, i) + 1) * halo_per_tile, last_halo), 0))
    state = pl.BlockSpec((ts // c,) + states_f.shape[1:], lambda b, i: (tile_row(b, i), 0, 0, 0))
    kv_spec = pl.BlockSpec((1,) + kv.shape[2:], lambda b, i: (layer * batch + b, 0, 0))
    dmat_spec = pl.BlockSpec((1, RET_HEADS, c, c), lambda b, i: (layer, 0, 0, 0), pipeline_mode=pl.Buffered(1))
    table_spec = pl.BlockSpec((1, c, RET_QK_W), lambda b, i: (layer, 0, 0), pipeline_mode=pl.Buffered(1))
    dec_spec = pl.BlockSpec((1, 2, 1, RET_QK_W), lambda b, i: (layer, 0, 0, 0), pipeline_mode=pl.Buffered(1))
    return pl.pallas_call(
        _mix_kernel,
        grid=(batch, nt),
        in_specs=[rows(d), _layer_resident(gains.shape, layer), rows(2 * RET_QK_W), rows(RET_V_W), rows(RET_V_W),
                  rows(POOL_W), p_prev, p_next, pl.BlockSpec((ts, POOL_W), lambda b, i: (nt - 1 - i, 0)),
                  rows(MEM_Q_W), state, kv_spec, dmat_spec, table_spec, table_spec, table_spec, dec_spec,
                  gate_weight(0), gate_weight(1), gate_weight(2),
                  _layer_resident(w_ret_o.shape, layer), _layer_resident(w_pool.shape, layer),
                  _layer_resident(w_mem_o.shape, layer), _layer_resident(w_out.shape, layer)],
        out_specs=rows(d),
        out_shape=jax.ShapeDtypeStruct((t, d), F32),
        scratch_shapes=[pltpu.VMEM((ts, RET_V_W), BF16), pltpu.VMEM((ts, N_BRANCHES * d), BF16),
                        pltpu.VMEM((RET_HEADS, RET_QK_DIM, RET_V_DIM), F32)],
        compiler_params=_params("arbitrary", "arbitrary"),
        name="mix",
    )(x2, gains, qk, v, g, p, p, p, inv_count, qm, states_f, kv.reshape((-1,) + kv.shape[2:]), dmat, xi_f, xi_b,
      zeta_b, dec, w_in, w_in, w_in, w_ret_o, w_pool, w_mem_o, w_out)


def _mlp_kernel(x_ref, gain_ref, w1_ref, w2_ref, final_gain_ref, out_ref, *, final_norm):
    x = x_ref[...]
    h = (x * gain_ref[...]).astype(BF16)
    acc = None
    for col in range(0, w1_ref.shape[1], COL_CHUNK):
        hid = jnp.maximum(_dot(h, w1_ref[:, col:col + COL_CHUNK]), 0.0)
        part = _dot((hid * hid).astype(BF16), w2_ref[col:col + COL_CHUNK, :])
        acc = part if acc is None else acc + part
    out = x + acc / (jnp.mean(x * x, axis=-1, keepdims=True) + EPS)
    out_ref[...] = _rms_norm(out, final_gain_ref[...]) if final_norm else out


def _mlp(x2, gains, w1, w2, final_gain, layer, final_norm):
    t, d = x2.shape
    tm = MLP_TILE
    rows = pl.BlockSpec((tm, d), lambda i: (i, 0))
    return pl.pallas_call(
        functools.partial(_mlp_kernel, final_norm=final_norm),
        grid=(t // tm,),
        in_specs=[rows, _layer_resident(gains.shape, layer), _layer_resident(w1.shape, layer),
                  _layer_resident(w2.shape, layer), _resident((1, d))],
        out_specs=rows,
        out_shape=jax.ShapeDtypeStruct((t, d), F32),
        compiler_params=_params("arbitrary"),
        name="mlp",
    )(x2, gains, w1, w2, final_gain.reshape(1, d))


def _pool_weight_kernel(w_grp_ref, scale_ref, w_o_ref, out_ref):
    for gi in range(len(POOL_WINDOWS)):
        rows = slice(gi * POOL_GROUP, (gi + 1) * POOL_GROUP)
        scaled = w_grp_ref[0, gi] * scale_ref[0, :, rows]
        out_ref[0, rows, :] = jnp.dot(scaled, w_o_ref[0, rows, :], preferred_element_type=F32,
                                      precision=lax.Precision.HIGHEST).astype(BF16)


def _pool_weights(w_pool_grp, pool_scale, w_pool_o):
    depth, groups, group_w, _ = w_pool_grp.shape
    d = w_pool_o.shape[2]
    return pl.pallas_call(
        _pool_weight_kernel,
        grid=(depth,),
        in_specs=[pl.BlockSpec((1, groups, group_w, group_w), lambda l: (l, 0, 0, 0)),
                  pl.BlockSpec((1, 1, POOL_W), lambda l: (l, 0, 0)),
                  pl.BlockSpec((1, POOL_W, d), lambda l: (l, 0, 0))],
        out_specs=pl.BlockSpec((1, POOL_W, d), lambda l: (l, 0, 0)),
        out_shape=jax.ShapeDtypeStruct((depth, POOL_W, d), BF16),
        compiler_params=_params("arbitrary"),
        name="pool_weights",
    )(w_pool_grp, pool_scale.reshape(depth, 1, POOL_W), w_pool_o)


def _rotary_tables(seq):
    inv = ROPE_BASE ** (-jnp.arange(0, RET_QK_DIM, 2, dtype=F32) / RET_QK_DIM)
    ang = jnp.arange(seq, dtype=F32)[:, None] * inv[None, :]
    cos, sin = jnp.cos(ang), jnp.sin(ang)
    return jnp.concatenate([cos, cos], axis=1), jnp.concatenate([-sin, sin], axis=1)


def _pool_inv_counts(seq):
    pos = jnp.arange(seq)
    cols = []
    for w in POOL_WINDOWS:
        count = jnp.minimum(pos + w // 2, seq) - jnp.maximum(pos - w // 2, 0)
        cols.append(jnp.broadcast_to((1.0 / count.astype(F32))[:, None], (seq, POOL_GROUP)))
    return jnp.concatenate(cols, axis=1)


def kernel(x, mem, w_in, ret_decay_logit, w_ret_o, w_pool_grp, pool_scale, w_pool_o, w_mem_kv, w_mem_o,
           w_out, w_ff1, w_ff2, norm1_g, norm2_g, mem_norm_g, final_norm_g):
    batch, seq, d = x.shape
    depth = w_in.shape[0]
    assert seq % MIX_TILE == 0 and seq % ROW_TILE == 0 and MIX_TILE % RET_CHUNK == 0
    assert d % COL_CHUNK == 0 and POOL_HALO % 8 == 0

    cos, sin = _rotary_tables(seq)
    inv_count = _pool_inv_counts(seq)
    dmat, xi_f, xi_b, zeta_f, zeta_b, dec = _decay_tables(ret_decay_logit)
    kv = _mem_kv(mem, mem_norm_g, w_mem_kv.astype(BF16))

    w_pool = _pool_weights(w_pool_grp, pool_scale, w_pool_o)
    w_in_b = w_in.astype(BF16)
    w_ret_o_b = w_ret_o.astype(BF16)
    w_mem_o_b = w_mem_o.astype(BF16)
    w_out_b = w_out.astype(BF16)
    w_ff1_b = w_ff1.astype(BF16)
    w_ff2_b = w_ff2.astype(BF16)
    gains1 = norm1_g.reshape(depth, 1, d)
    gains2 = norm2_g.reshape(depth, 1, d)

    x2 = x.reshape(batch * seq, d)
    for l in range(depth):
        qk, v, g, p, qm, states_f = _in_proj(x2, gains1, w_in_b, cos, sin, zeta_f, dec, l, seq)
        x2 = _mix(x2, gains1, qk, v, g, p, inv_count, qm, states_f, kv, dmat, xi_f, xi_b, zeta_b, dec, w_in_b,
                  w_ret_o_b, w_pool, w_mem_o_b, w_out_b, l, batch, seq)
        x2 = _mlp(x2, gains2, w_ff1_b, w_ff2_b, final_norm_g, l, l == depth - 1)
    return x2.reshape(batch, seq, d)
```

```python
import functools

import jax
import jax.numpy as jnp
from jax import lax
from jax.experimental import pallas as pl
from jax.experimental.pallas import tpu as pltpu

F32 = jnp.float32
BF16 = jnp.bfloat16

RET_HEADS = 4
RET_QK_DIM = 128
RET_V_DIM = 256
MEM_HEADS = 4
MEM_HEAD_DIM = 128
POOL_WINDOWS = (2, 4, 8, 16)
POOL_GROUP = 128
N_BRANCHES = 3
ROPE_BASE = 10000.0
EPS = 1e-6
LOG2_E = 1.4426950408889634

RET_QK_W = RET_HEADS * RET_QK_DIM
RET_V_W = RET_HEADS * RET_V_DIM
MEM_Q_W = MEM_HEADS * MEM_HEAD_DIM
POOL_W = POOL_GROUP * len(POOL_WINDOWS)
POOL_HALO = max(POOL_WINDOWS) // 2
GATE_COL0 = 2 * RET_QK_W + 2 * RET_V_W + POOL_W + MEM_Q_W

V7X_LANES = 128
V7X_VMEM_BYTES = 64 * 1024 * 1024
VMEM_LIMIT_BYTES = V7X_VMEM_BYTES - 4 * 1024 * 1024

RET_CHUNK = 256
ROW_TILE = 1024
MLP_TILE = 1024
MIX_TILE = 512
COL_CHUNK = 512
IN_PROJ_CHUNK = 256
MIX_FILL_CHUNK = 256


def _params(*semantics):
    return pltpu.CompilerParams(dimension_semantics=semantics, vmem_limit_bytes=VMEM_LIMIT_BYTES)


def _resident(shape):
    zeros = (0,) * len(shape)
    return pl.BlockSpec(shape, lambda *_: zeros, pipeline_mode=pl.Buffered(1))


def _layer_resident(stacked_shape, layer):
    index = (layer,) + (0,) * (len(stacked_shape) - 1)
    return pl.BlockSpec((None,) + tuple(stacked_shape[1:]), lambda *_: index, pipeline_mode=pl.Buffered(1))


def _rms_norm(x, gain):
    return x * lax.rsqrt(jnp.mean(x * x, axis=-1, keepdims=True) + EPS) * gain


def _swish(x):
    half = 0.5 * x
    return half + half * jnp.tanh(half)


_dot = functools.partial(jnp.dot, preferred_element_type=F32)
_dot_nt = functools.partial(lax.dot_general, dimension_numbers=(((1,), (1,)), ((), ())), preferred_element_type=F32)
_dot_tn = functools.partial(lax.dot_general, dimension_numbers=(((0,), (0,)), ((), ())), preferred_element_type=F32)


def _decay_tables_kernel(logit_ref, dmat_ref, xi_f_ref, xi_b_ref, zeta_f_ref, zeta_b_ref, dec_ref):
    c = RET_CHUNK
    logit = logit_ref[0]
    log_g = jnp.minimum(logit, 0.0) - jnp.log1p(jnp.exp(-jnp.abs(logit)))
    lg_f, lg_b = log_g[0], log_g[1]
    row = lax.broadcasted_iota(jnp.int32, (c, RET_QK_W), 0).astype(F32)
    xi_f_ref[0] = jnp.exp(lg_f * (row + 1.0)).astype(BF16)
    xi_b_ref[0] = jnp.exp(lg_b * (c - row)).astype(BF16)
    zeta_f_ref[0] = jnp.exp(lg_f * (c - 1.0 - row)).astype(BF16)
    zeta_b_ref[0] = jnp.exp(lg_b * row).astype(BF16)
    dec_ref[0, 0] = jnp.exp(lg_f * c)
    dec_ref[0, 1] = jnp.exp(lg_b * c)
    i = lax.broadcasted_iota(jnp.int32, (c, c), 0)
    j = lax.broadcasted_iota(jnp.int32, (c, c), 1)
    diff = (i - j).astype(F32)
    for h in range(RET_HEADS):
        lf = lg_f[:, h * RET_QK_DIM:h * RET_QK_DIM + 1]
        lb = lg_b[:, h * RET_QK_DIM:h * RET_QK_DIM + 1]
        fwd = jnp.exp(lf * jnp.maximum(diff, 0.0))
        bwd = jnp.exp(lb * jnp.maximum(-diff, 0.0))
        dmat_ref[0, h] = jnp.where(diff >= 0.0, fwd, bwd)


def _decay_tables(ret_decay_logit):
    depth = ret_decay_logit.shape[0]
    c = RET_CHUNK
    logit = jnp.repeat(ret_decay_logit.astype(F32), RET_QK_DIM, axis=-1)[:, :, None, :]
    vec = jax.ShapeDtypeStruct((depth, c, RET_QK_W), BF16)
    vec_spec = pl.BlockSpec((1, c, RET_QK_W), lambda l: (l, 0, 0))
    return pl.pallas_call(
        _decay_tables_kernel,
        grid=(depth,),
        in_specs=[pl.BlockSpec((1, 2, 1, RET_QK_W), lambda l: (l, 0, 0, 0))],
        out_specs=[pl.BlockSpec((1, RET_HEADS, c, c), lambda l: (l, 0, 0, 0)),
                   vec_spec, vec_spec, vec_spec, vec_spec,
                   pl.BlockSpec((1, 2, 1, RET_QK_W), lambda l: (l, 0, 0, 0))],
        out_shape=[jax.ShapeDtypeStruct((depth, RET_HEADS, c, c), F32), vec, vec, vec, vec,
                   jax.ShapeDtypeStruct((depth, 2, 1, RET_QK_W), F32)],
        compiler_params=_params("arbitrary"),
        name="decay_tables",
    )(logit)


def _mem_kv_kernel(mem_ref, gain_ref, w_ref, kv_ref):
    mem_n = _rms_norm(mem_ref[...], gain_ref[...]).astype(BF16)
    for l in range(w_ref.shape[0]):
        kv_ref[l] = _dot(mem_n, w_ref[l]).astype(BF16)


def _mem_kv(mem, mem_norm_g, w_mem_kv):
    b, m, d = mem.shape
    depth, _, kvw = w_mem_kv.shape
    rows = b * m
    tm = min(MLP_TILE, rows)
    assert rows % tm == 0
    kv = pl.pallas_call(
        _mem_kv_kernel,
        grid=(rows // tm,),
        in_specs=[pl.BlockSpec((tm, d), lambda i: (i, 0)),
                  _resident((1, d)),
                  _resident((depth, d, kvw))],
        out_specs=pl.BlockSpec((depth, tm, kvw), lambda i: (0, i, 0)),
        out_shape=jax.ShapeDtypeStruct((depth, rows, kvw), BF16),
        compiler_params=_params("arbitrary"),
        name="mem_kv",
    )(mem.reshape(rows, d), mem_norm_g.reshape(1, d), w_mem_kv)
    return kv.reshape(depth, b, m, kvw)


def _chunk_outer(k, v_ref, rows, zeta):
    kz = k * zeta
    return [_dot_tn(kz[:, h * RET_QK_DIM:(h + 1) * RET_QK_DIM], v_ref[rows, h * RET_V_DIM:(h + 1) * RET_V_DIM])
            for h in range(RET_HEADS)]


def _decay_and_add(acc, outer, dec):
    for h in range(RET_HEADS):
        acc[h] = acc[h] * dec[:, h * RET_QK_DIM:h * RET_QK_DIM + 1] + outer[h]


def _in_proj_kernel(x_ref, gain_ref, w_ref, cos_ref, sin_ref, zeta_f_ref, dec_ref,
                    qk_ref, v_ref, g_ref, p_ref, qm_ref, state_f_ref, acc, outer_scr, *, tiles_per_seq):
    @pl.when(pl.program_id(0) % tiles_per_seq == 0)
    def _():
        acc[...] = jnp.zeros_like(acc)

    h = _rms_norm(x_ref[...], gain_ref[...]).astype(BF16)
    cos = cos_ref[...]
    sin = sin_ref[...]

    def rotary(a, scale):
        heads = []
        for hd in range(a.shape[1] // RET_QK_DIM):
            ah = a[:, hd * RET_QK_DIM:(hd + 1) * RET_QK_DIM]
            heads.append(ah * cos + pltpu.roll(ah, RET_QK_DIM // 2, axis=1) * sin)
        rotated = jnp.concatenate(heads, axis=1)
        return rotated if scale is None else rotated * scale

    groups = [
        (qk_ref, 0, RET_QK_W, lambda a: rotary(a, None)),
        (qk_ref, RET_QK_W, RET_QK_W, lambda a: rotary(a, RET_QK_DIM ** -0.5)),
        (v_ref, 0, RET_V_W, lambda a: a),
        (g_ref, 0, RET_V_W, _swish),
        (p_ref, 0, POOL_W, lambda a: a),
        (qm_ref, 0, MEM_Q_W, lambda a: a),
    ]
    chunk = RET_CHUNK
    n_chunks = x_ref.shape[0] // chunk
    outer_jobs = list(range(n_chunks))

    def outer_job():
        ci = outer_jobs.pop(0)
        rows = slice(ci * chunk, (ci + 1) * chunk)
        for hd, outer in enumerate(_chunk_outer(qk_ref[rows, RET_QK_W:], v_ref, rows, zeta_f_ref[0])):
            outer_scr[ci, hd] = outer

    def scan():
        for ci in range(n_chunks):
            for hd in range(RET_HEADS):
                state_f_ref[ci, hd] = acc[hd].astype(BF16)
            _decay_and_add(acc, outer_scr[ci], dec_ref[0, 0])

    w_col = 0
    scanned = False
    for out_ref, out_col, width, epilogue in groups:
        for c in range(0, width, IN_PROJ_CHUNK):
            a = _dot(h, w_ref[:, w_col + c:w_col + c + IN_PROJ_CHUNK])
            out_ref[:, out_col + c:out_col + c + IN_PROJ_CHUNK] = epilogue(a).astype(out_ref.dtype)
            if out_ref is not qk_ref and out_ref is not v_ref:
                if outer_jobs:
                    outer_job()
                elif not scanned:
                    scan()
                    scanned = True
        w_col += width
    assert scanned and not outer_jobs


def _in_proj(x2, gains, w_in, cos, sin, zeta_f, dec, layer, seq):
    t, d = x2.shape
    tm = ROW_TILE
    c = RET_CHUNK
    pos_tiles = seq // tm
    w_cols = GATE_COL0

    def rows(width):
        return pl.BlockSpec((tm, width), lambda i: (i, 0))

    pos_spec = pl.BlockSpec((tm, RET_QK_DIM), lambda i: (i % pos_tiles, 0))
    w_spec = pl.BlockSpec((None, d, w_cols), lambda i: (layer, 0, 0), pipeline_mode=pl.Buffered(1))
    state_shape = (RET_HEADS, RET_QK_DIM, RET_V_DIM)
    return pl.pallas_call(
        functools.partial(_in_proj_kernel, tiles_per_seq=pos_tiles),
        grid=(t // tm,),
        in_specs=[rows(d), _layer_resident(gains.shape, layer), w_spec, pos_spec, pos_spec,
                  pl.BlockSpec((1, c, RET_QK_W), lambda i: (layer, 0, 0), pipeline_mode=pl.Buffered(1)),
                  pl.BlockSpec((1, 2, 1, RET_QK_W), lambda i: (layer, 0, 0, 0), pipeline_mode=pl.Buffered(1))],
        out_specs=[rows(2 * RET_QK_W), rows(RET_V_W), rows(RET_V_W), rows(POOL_W), rows(MEM_Q_W),
                   pl.BlockSpec((tm // c,) + state_shape, lambda i: (i, 0, 0, 0))],
        out_shape=[jax.ShapeDtypeStruct((t, 2 * RET_QK_W), BF16),
                   jax.ShapeDtypeStruct((t, RET_V_W), BF16),
                   jax.ShapeDtypeStruct((t, RET_V_W), BF16),
                   jax.ShapeDtypeStruct((t, POOL_W), F32),
                   jax.ShapeDtypeStruct((t, MEM_Q_W), BF16),
                   jax.ShapeDtypeStruct((t // c,) + state_shape, BF16)],
        scratch_shapes=[pltpu.VMEM(state_shape, F32), pltpu.VMEM((tm // c,) + state_shape, F32)],
        compiler_params=_params("arbitrary"),
        name="in_proj",
    )(x2, gains, w_in, cos, sin, zeta_f, dec)


def _window_sums(padded):
    length = padded.shape[0]
    halo = POOL_HALO
    ts = length - 2 * halo

    def ahead(a, k):
        return pltpu.roll(a, length - k, axis=0)

    def behind(a, k):
        return pltpu.roll(a, k, axis=0)

    sums = []
    for gi, w in enumerate(POOL_WINDOWS):
        a = padded[:, gi * POOL_GROUP:(gi + 1) * POOL_GROUP]
        span = 1
        while 2 * span < w:
            a = a + ahead(a, span)
            span *= 2
        assert 2 * span == w and span <= halo
        sums.append((a + behind(a, span))[halo:halo + ts, :])
    return sums


def _mix_kernel(x_ref, gain_ref, qk_ref, v_ref, g_ref, p_ref, p_prev_ref, p_next_ref, inv_count_ref, qm_ref,
                state_f_ref, kv_ref, dmat_ref, xi_f_ref, xi_b_ref, zeta_b_ref, dec_ref,
                w_gate_ret_ref, w_gate_pool_ref, w_gate_mem_ref,
                w_ret_o_ref, w_pool_ref, w_mem_o_ref, w_out_ref,
                out_ref, ret_scr, gate_scr, acc_b):
    ts = x_ref.shape[0]
    c = RET_CHUNK
    n_tiles = pl.num_programs(1)
    tile = n_tiles - 1 - pl.program_id(1)

    @pl.when(pl.program_id(1) == 0)
    def _():
        acc_b[...] = jnp.zeros_like(acc_b)

    d = x_ref.shape[1]
    fill_cols = list(range(0, d, MIX_FILL_CHUNK))

    def out_proj_chunk(lhs, w_ref, col):
        return _dot(lhs, w_ref[:, col:col + MIX_FILL_CHUNK]).astype(BF16)

    gate_w_refs = (w_gate_ret_ref, w_gate_pool_ref, w_gate_mem_ref)
    gate_jobs = [(b, col) for b in range(N_BRANCHES) for col in fill_cols]
    half_h = []

    def gate_job():
        if not half_h:
            x = x_ref[...]
            half_inv_rms = 0.5 * lax.rsqrt(jnp.mean(x * x, axis=-1, keepdims=True) + EPS)
            half_h.append((x * half_inv_rms * gain_ref[...]).astype(BF16))
        b, col = gate_jobs.pop(0)
        z_half = _dot(half_h[0], gate_w_refs[b][:, col:col + MIX_FILL_CHUNK])
        gate_scr[:, b * d + col:b * d + col + MIX_FILL_CHUNK] = jnp.tanh(z_half).astype(BF16) * 0.5 + 0.5

    xi_f = xi_f_ref[0]
    xi_b = xi_b_ref[0]
    def normalise_and_gate(o, rows, v_cols):
        mu = jnp.mean(o, axis=-1, keepdims=True)
        cen = o - mu
        var = jnp.mean(cen * cen, axis=-1, keepdims=True)
        o_n = cen * lax.rsqrt(var + EPS)
        ret_scr[rows, v_cols] = o_n.astype(BF16) * g_ref[rows, v_cols]

    pending = None
    for ci in reversed(range(ts // c)):
        rows = slice(ci * c, (ci + 1) * c)
        q = qk_ref[rows, :RET_QK_W]
        q_f = q * xi_f
        q_b = q * xi_b
        for h in range(RET_HEADS):
            qk_cols = slice(h * RET_QK_DIM, (h + 1) * RET_QK_DIM)
            v_cols = slice(h * RET_V_DIM, (h + 1) * RET_V_DIM)
            k_h = qk_ref[rows, RET_QK_W + h * RET_QK_DIM:RET_QK_W + (h + 1) * RET_QK_DIM]
            s = _dot_nt(q[:, qk_cols], k_h) * dmat_ref[0, h]
            q_fb = jnp.concatenate([q_f[:, qk_cols], q_b[:, qk_cols]], axis=1)
            state = jnp.concatenate([state_f_ref[ci, h], acc_b[h].astype(BF16)], axis=0)
            o = _dot(s.astype(BF16), v_ref[rows, v_cols]) + _dot(q_fb, state)
            if pending is not None:
                normalise_and_gate(*pending)
                if gate_jobs:
                    gate_job()
            pending = (o, rows, v_cols)
        _decay_and_add(acc_b, _chunk_outer(qk_ref[rows, RET_QK_W:], v_ref, rows, zeta_b_ref[0]), dec_ref[0, 1])
    normalise_and_gate(*pending)
    if gate_jobs:
        gate_job()

    p = p_ref[...]
    padded = jnp.concatenate([jnp.where(tile > 0, p_prev_ref[...], 0.0), p,
                              jnp.where(tile < n_tiles - 1, p_next_ref[...], 0.0)], axis=0)
    ret_lhs = ret_scr[...]
    groups, o_ret = [], []
    for gi, win in enumerate(_window_sums(padded)):
        cols = slice(gi * POOL_GROUP, (gi + 1) * POOL_GROUP)
        groups.append((win * inv_count_ref[:, cols] - p[:, cols]).astype(BF16))
        if gi < len(fill_cols):
            o_ret.append(out_proj_chunk(ret_lhs, w_ret_o_ref, fill_cols[gi]))
    o_ret += [out_proj_chunk(ret_lhs, w_ret_o_ref, col) for col in fill_cols[len(o_ret):]]
    pool_lhs = jnp.concatenate(groups, axis=1)

    exp2_scale = (MEM_HEAD_DIM ** -0.5) * LOG2_E
    heads, o_pool = [], []
    for h in range(MEM_HEADS):
        cols = slice(h * MEM_HEAD_DIM, (h + 1) * MEM_HEAD_DIM)
        k_h = kv_ref[0, :, cols]
        v_h = kv_ref[0, :, MEM_Q_W + h * MEM_HEAD_DIM:MEM_Q_W + (h + 1) * MEM_HEAD_DIM]
        s = _dot_nt(qm_ref[:, cols], k_h)
        e = jnp.exp2((s - jnp.max(s, axis=-1, keepdims=True)) * exp2_scale)
        o = _dot(e.astype(BF16), v_h) / jnp.sum(e, axis=-1, keepdims=True)
        heads.append(o.astype(BF16))
        if gate_jobs:
            gate_job()
        if h < len(fill_cols):
            o_pool.append(out_proj_chunk(pool_lhs, w_pool_ref, fill_cols[h]))
    while gate_jobs:
        gate_job()
    o_pool += [out_proj_chunk(pool_lhs, w_pool_ref, col) for col in fill_cols[len(o_pool):]]
    mem_lhs = jnp.concatenate(heads, axis=1)

    merged = []
    for j, col in enumerate(fill_cols):
        o_mem = out_proj_chunk(mem_lhs, w_mem_o_ref, col)
        gates = [gate_scr[:, b * d + col:b * d + col + MIX_FILL_CHUNK] for b in range(N_BRANCHES)]
        merged.append(gates[0] * o_ret[j] + gates[1] * o_pool[j] + gates[2] * o_mem)
    out_ref[...] = x_ref[...] + _dot(jnp.concatenate(merged, axis=1), w_out_ref[...])


def _mix(x2, gains, qk, v, g, p, inv_count, qm, states_f, kv, dmat, xi_f, xi_b, zeta_b, dec, w_in, w_ret_o, w_pool,
         w_mem_o, w_out, layer, batch, seq):
    t, d = x2.shape
    assert GATE_COL0 % d == 0

    def gate_weight(branch):
        return pl.BlockSpec((None, d, d), lambda b, i: (layer, 0, GATE_COL0 // d + branch),
                            pipeline_mode=pl.Buffered(1))

    ts = MIX_TILE
    c = RET_CHUNK
    nt = seq // ts
    halo = POOL_HALO
    halo_per_tile = ts // halo
    last_halo = t // halo - 1

    def tile_row(b, i):
        return b * nt + nt - 1 - i

    def rows(width):
        return pl.BlockSpec((ts, width), lambda b, i: (tile_row(b, i), 0))

    p_prev = pl.BlockSpec((halo, POOL_W), lambda b, i: (jnp.maximum(tile_row(b, i) * halo_per_tile - 1, 0), 0))
    p_next = pl.BlockSpec((halo, POOL_W),
                          lambda b, i: (jnp.minimum((tile_row(b, i) + 1) * halo_per_tile, last_halo), 0))
    state = pl.BlockSpec((ts // c,) + states_f.shape[1:], lambda b, i: (tile_row(b, i), 0, 0, 0))
    kv_spec = pl.BlockSpec((1,) + kv.shape[2:], lambda b, i: (layer * batch + b, 0, 0))
    dmat_spec = pl.BlockSpec((1, RET_HEADS, c, c), lambda b, i: (layer, 0, 0, 0), pipeline_mode=pl.Buffered(1))
    table_spec = pl.BlockSpec((1, c, RET_QK_W), lambda b, i: (layer, 0, 0), pipeline_mode=pl.Buffered(1))
    dec_spec = pl.BlockSpec((1, 2, 1, RET_QK_W), lambda b, i: (layer, 0, 0, 0), pipeline_mode=pl.Buffered(1))
    return pl.pallas_call(
        _mix_kernel,
        grid=(batch, nt),
        in_specs=[rows(d), _layer_resident(gains.shape, layer), rows(2 * RET_QK_W), rows(RET_V_W), rows(RET_V_W),
                  rows(POOL_W), p_prev, p_next, pl.BlockSpec((ts, POOL_W), lambda b, i: (nt - 1 - i, 0)),
                  rows(MEM_Q_W), state, kv_spec, dmat_spec, table_spec, table_spec, table_spec, dec_spec,
                  gate_weight(0), gate_weight(1), gate_weight(2),
                  _layer_resident(w_ret_o.shape, layer), _layer_resident(w_pool.shape, layer),
                  _layer_resident(w_mem_o.shape, layer), _layer_resident(w_out.shape, layer)],
        out_specs=rows(d),
        out_shape=jax.ShapeDtypeStruct((t, d), F32),
        scratch_shapes=[pltpu.VMEM((ts, RET_V_W), BF16), pltpu.VMEM((ts, N_BRANCHES * d), BF16),
                        pltpu.VMEM((RET_HEADS, RET_QK_DIM, RET_V_DIM), F32)],
        compiler_params=_params("arbitrary", "arbitrary"),
        name="mix",
    )(x2, gains, qk, v, g, p, p, p, inv_count, qm, states_f, kv.reshape((-1,) + kv.shape[2:]), dmat, xi_f, xi_b,
      zeta_b, dec, w_in, w_in, w_in, w_ret_o, w_pool, w_mem_o, w_out)


def _mlp_kernel(x_ref, gain_ref, w1_ref, w2_ref, final_gain_ref, out_ref, hid_scr, *, final_norm):
    x = x_ref[...]
    h = (x * gain_ref[...]).astype(BF16)
    for col in range(0, w1_ref.shape[1], COL_CHUNK):
        hid = jnp.maximum(_dot(h, w1_ref[:, col:col + COL_CHUNK]), 0.0)
        hid_scr[:, col:col + COL_CHUNK] = (hid * hid).astype(BF16)
    acc = _dot(hid_scr[...], w2_ref[...])
    out = x + acc / (jnp.mean(x * x, axis=-1, keepdims=True) + EPS)
    out_ref[...] = _rms_norm(out, final_gain_ref[...]) if final_norm else out


def _mlp(x2, gains, w1, w2, final_gain, layer, final_norm):
    t, d = x2.shape
    tm = MLP_TILE
    rows = pl.BlockSpec((tm, d), lambda i: (i, 0))
    return pl.pallas_call(
        functools.partial(_mlp_kernel, final_norm=final_norm),
        grid=(t // tm,),
        in_specs=[rows, _layer_resident(gains.shape, layer), _layer_resident(w1.shape, layer),
                  _layer_resident(w2.shape, layer), _resident((1, d))],
        out_specs=rows,
        out_shape=jax.ShapeDtypeStruct((t, d), F32),
        scratch_shapes=[pltpu.VMEM((tm, w1.shape[2]), BF16)],
        compiler_params=_params("arbitrary"),
        name="mlp",
    )(x2, gains, w1, w2, final_gain.reshape(1, d))


def _pool_weight_kernel(w_grp_ref, scale_ref, w_o_ref, out_ref):
    for gi in range(len(POOL_WINDOWS)):
        rows = slice(gi * POOL_GROUP, (gi + 1) * POOL_GROUP)
        scaled = w_grp_ref[0, gi] * scale_ref[0, :, rows]
        out_ref[0, rows, :] = jnp.dot(scaled, w_o_ref[0, rows, :], preferred_element_type=F32,
                                      precision=lax.Precision.HIGHEST).astype(BF16)


def _pool_weights(w_pool_grp, pool_scale, w_pool_o):
    depth, groups, group_w, _ = w_pool_grp.shape
    d = w_pool_o.shape[2]
    return pl.pallas_call(
        _pool_weight_kernel,
        grid=(depth,),
        in_specs=[pl.BlockSpec((1, groups, group_w, group_w), lambda l: (l, 0, 0, 0)),
                  pl.BlockSpec((1, 1, POOL_W), lambda l: (l, 0, 0)),
                  pl.BlockSpec((1, POOL_W, d), lambda l: (l, 0, 0))],
        out_specs=pl.BlockSpec((1, POOL_W, d), lambda l: (l, 0, 0)),
        out_shape=jax.ShapeDtypeStruct((depth, POOL_W, d), BF16),
        compiler_params=_params("arbitrary"),
        name="pool_weights",
    )(w_pool_grp, pool_scale.reshape(depth, 1, POOL_W), w_pool_o)


def _rotary_tables(seq):
    inv = ROPE_BASE ** (-jnp.arange(0, RET_QK_DIM, 2, dtype=F32) / RET_QK_DIM)
    ang = jnp.arange(seq, dtype=F32)[:, None] * inv[None, :]
    cos, sin = jnp.cos(ang), jnp.sin(ang)
    return jnp.concatenate([cos, cos], axis=1), jnp.concatenate([-sin, sin], axis=1)


def _pool_inv_counts(seq):
    pos = jnp.arange(seq)
    cols = []
    for w in POOL_WINDOWS:
        count = jnp.minimum(pos + w // 2, seq) - jnp.maximum(pos - w // 2, 0)
        cols.append(jnp.broadcast_to((1.0 / count.astype(F32))[:, None], (seq, POOL_GROUP)))
    return jnp.concatenate(cols, axis=1)


def kernel(x, mem, w_in, ret_decay_logit, w_ret_o, w_pool_grp, pool_scale, w_pool_o, w_mem_kv, w_mem_o,
           w_out, w_ff1, w_ff2, norm1_g, norm2_g, mem_norm_g, final_norm_g):
    batch, seq, d = x.shape
    depth = w_in.shape[0]
    assert seq % MIX_TILE == 0 and seq % ROW_TILE == 0 and MIX_TILE % RET_CHUNK == 0
    assert d % COL_CHUNK == 0 and POOL_HALO % 8 == 0

    cos, sin = _rotary_tables(seq)
    inv_count = _pool_inv_counts(seq)
    dmat, xi_f, xi_b, zeta_f, zeta_b, dec = _decay_tables(ret_decay_logit)
    kv = _mem_kv(mem, mem_norm_g, w_mem_kv.astype(BF16))

    w_pool = _pool_weights(w_pool_grp, pool_scale, w_pool_o)
    w_in_b = w_in.astype(BF16)
    w_ret_o_b = w_ret_o.astype(BF16)
    w_mem_o_b = w_mem_o.astype(BF16)
    w_out_b = w_out.astype(BF16)
    w_ff1_b = w_ff1.astype(BF16)
    w_ff2_b = w_ff2.astype(BF16)
    gains1 = norm1_g.reshape(depth, 1, d)
    gains2 = norm2_g.reshape(depth, 1, d)

    x2 = x.reshape(batch * seq, d)
    for l in range(depth):
        qk, v, g, p, qm, states_f = _in_proj(x2, gains1, w_in_b, cos, sin, zeta_f, dec, l, seq)
        x2 = _mix(x2, gains1, qk, v, g, p, inv_count, qm, states_f, kv, dmat, xi_f, xi_b, zeta_b, dec, w_in_b,
                  w_ret_o_b, w_pool, w_mem_o_b, w_out_b, l, batch, seq)
        x2 = _mlp(x2, gains2, w_ff1_b, w_ff2_b, final_norm_g, l, l == depth - 1)
    return x2.reshape(batch, seq, d)
```

```python
import functools

import jax
import jax.numpy as jnp
from jax import lax
from jax.experimental import pallas as pl
from jax.experimental.pallas import tpu as pltpu

F32 = jnp.float32
BF16 = jnp.bfloat16

RET_HEADS = 4
RET_QK_DIM = 128
RET_V_DIM = 256
MEM_HEADS = 4
MEM_HEAD_DIM = 128
POOL_WINDOWS = (2, 4, 8, 16)
POOL_GROUP = 128
N_BRANCHES = 3
ROPE_BASE = 10000.0
EPS = 1e-6
LOG2_E = 1.4426950408889634

RET_QK_W = RET_HEADS * RET_QK_DIM
RET_V_W = RET_HEADS * RET_V_DIM
MEM_Q_W = MEM_HEADS * MEM_HEAD_DIM
POOL_W = POOL_GROUP * len(POOL_WINDOWS)
POOL_HALO = max(POOL_WINDOWS) // 2
GATE_COL0 = 2 * RET_QK_W + 2 * RET_V_W + POOL_W + MEM_Q_W

V7X_LANES = 128
V7X_VMEM_BYTES = 64 * 1024 * 1024
VMEM_LIMIT_BYTES = V7X_VMEM_BYTES - 4 * 1024 * 1024

RET_CHUNK = 256
ROW_TILE = 1024
MLP_TILE = 1024
MIX_TILE = 512
COL_CHUNK = 512
IN_PROJ_CHUNK = 256
MIX_FILL_CHUNK = 256


def _params(*semantics):
    return pltpu.CompilerParams(dimension_semantics=semantics, vmem_limit_bytes=VMEM_LIMIT_BYTES)


def _resident(shape):
    zeros = (0,) * len(shape)
    return pl.BlockSpec(shape, lambda *_: zeros, pipeline_mode=pl.Buffered(1))


def _layer_resident(stacked_shape, layer):
    index = (layer,) + (0,) * (len(stacked_shape) - 1)
    return pl.BlockSpec((None,) + tuple(stacked_shape[1:]), lambda *_: index, pipeline_mode=pl.Buffered(1))


def _rms_norm(x, gain):
    return x * lax.rsqrt(jnp.mean(x * x, axis=-1, keepdims=True) + EPS) * gain


def _swish(x):
    half = 0.5 * x
    return half + half * jnp.tanh(half)


_dot = functools.partial(jnp.dot, preferred_element_type=F32)
_dot_nt = functools.partial(lax.dot_general, dimension_numbers=(((1,), (1,)), ((), ())), preferred_element_type=F32)
_dot_tn = functools.partial(lax.dot_general, dimension_numbers=(((0,), (0,)), ((), ())), preferred_element_type=F32)


def _decay_tables_kernel(logit_ref, dmat_ref, xi_f_ref, xi_b_ref, zeta_f_ref, zeta_b_ref, dec_ref):
    c = RET_CHUNK
    logit = logit_ref[0]
    log_g = jnp.minimum(logit, 0.0) - jnp.log1p(jnp.exp(-jnp.abs(logit)))
    lg_f, lg_b = log_g[0], log_g[1]
    row = lax.broadcasted_iota(jnp.int32, (c, RET_QK_W), 0).astype(F32)
    xi_f_ref[0] = jnp.exp(lg_f * (row + 1.0)).astype(BF16)
    xi_b_ref[0] = jnp.exp(lg_b * (c - row)).astype(BF16)
    zeta_f_ref[0] = jnp.exp(lg_f * (c - 1.0 - row)).astype(BF16)
    zeta_b_ref[0] = jnp.exp(lg_b * row).astype(BF16)
    dec_ref[0, 0] = jnp.exp(lg_f * c)
    dec_ref[0, 1] = jnp.exp(lg_b * c)
    i = lax.broadcasted_iota(jnp.int32, (c, c), 0)
    j = lax.broadcasted_iota(jnp.int32, (c, c), 1)
    diff = (i - j).astype(F32)
    for h in range(RET_HEADS):
        lf = lg_f[:, h * RET_QK_DIM:h * RET_QK_DIM + 1]
        lb = lg_b[:, h * RET_QK_DIM:h * RET_QK_DIM + 1]
        fwd = jnp.exp(lf * jnp.maximum(diff, 0.0))
        bwd = jnp.exp(lb * jnp.maximum(-diff, 0.0))
        dmat_ref[0, h] = jnp.where(diff >= 0.0, fwd, bwd)


def _decay_tables(ret_decay_logit):
    depth = ret_decay_logit.shape[0]
    c = RET_CHUNK
    logit = jnp.repeat(ret_decay_logit.astype(F32), RET_QK_DIM, axis=-1)[:, :, None, :]
    vec = jax.ShapeDtypeStruct((depth, c, RET_QK_W), BF16)
    vec_spec = pl.BlockSpec((1, c, RET_QK_W), lambda l: (l, 0, 0))
    return pl.pallas_call(
        _decay_tables_kernel,
        grid=(depth,),
        in_specs=[pl.BlockSpec((1, 2, 1, RET_QK_W), lambda l: (l, 0, 0, 0))],
        out_specs=[pl.BlockSpec((1, RET_HEADS, c, c), lambda l: (l, 0, 0, 0)),
                   vec_spec, vec_spec, vec_spec, vec_spec,
                   pl.BlockSpec((1, 2, 1, RET_QK_W), lambda l: (l, 0, 0, 0))],
        out_shape=[jax.ShapeDtypeStruct((depth, RET_HEADS, c, c), F32), vec, vec, vec, vec,
                   jax.ShapeDtypeStruct((depth, 2, 1, RET_QK_W), F32)],
        compiler_params=_params("arbitrary"),
        name="decay_tables",
    )(logit)


def _mem_kv_kernel(mem_ref, gain_ref, w_ref, kv_ref):
    mem_n = _rms_norm(mem_ref[...], gain_ref[...]).astype(BF16)
    for l in range(w_ref.shape[0]):
        kv_ref[l] = _dot(mem_n, w_ref[l]).astype(BF16)


def _mem_kv(mem, mem_norm_g, w_mem_kv):
    b, m, d = mem.shape
    depth, _, kvw = w_mem_kv.shape
    rows = b * m
    tm = min(MLP_TILE, rows)
    assert rows % tm == 0
    kv = pl.pallas_call(
        _mem_kv_kernel,
        grid=(rows // tm,),
        in_specs=[pl.BlockSpec((tm, d), lambda i: (i, 0)),
                  _resident((1, d)),
                  _resident((depth, d, kvw))],
        out_specs=pl.BlockSpec((depth, tm, kvw), lambda i: (0, i, 0)),
        out_shape=jax.ShapeDtypeStruct((depth, rows, kvw), BF16),
        compiler_params=_params("arbitrary"),
        name="mem_kv",
    )(mem.reshape(rows, d), mem_norm_g.reshape(1, d), w_mem_kv)
    return kv.reshape(depth, b, m, kvw)


def _chunk_outer(k, v_ref, rows, zeta):
    kz = k * zeta
    return [_dot_tn(kz[:, h * RET_QK_DIM:(h + 1) * RET_QK_DIM], v_ref[rows, h * RET_V_DIM:(h + 1) * RET_V_DIM])
            for h in range(RET_HEADS)]


def _decay_and_add(acc, outer, dec):
    for h in range(RET_HEADS):
        acc[h] = acc[h] * dec[:, h * RET_QK_DIM:h * RET_QK_DIM + 1] + outer[h]


def _in_proj_kernel(x_ref, gain_ref, w_ref, cos_ref, sin_ref, zeta_f_ref, dec_ref,
                    qk_ref, v_ref, g_ref, p_ref, qm_ref, state_f_ref, half_h_ref, acc, outer_scr, *, tiles_per_seq):
    @pl.when(pl.program_id(0) % tiles_per_seq == 0)
    def _():
        acc[...] = jnp.zeros_like(acc)

    h = _rms_norm(x_ref[...], gain_ref[...]).astype(BF16)
    half_h_ref[...] = h * 0.5
    cos = cos_ref[...]
    sin = sin_ref[...]

    def rotary(a, scale):
        heads = []
        for hd in range(a.shape[1] // RET_QK_DIM):
            ah = a[:, hd * RET_QK_DIM:(hd + 1) * RET_QK_DIM]
            heads.append(ah * cos + pltpu.roll(ah, RET_QK_DIM // 2, axis=1) * sin)
        rotated = jnp.concatenate(heads, axis=1)
        return rotated if scale is None else rotated * scale

    groups = [
        (qk_ref, 0, RET_QK_W, lambda a: rotary(a, None)),
        (qk_ref, RET_QK_W, RET_QK_W, lambda a: rotary(a, RET_QK_DIM ** -0.5)),
        (v_ref, 0, RET_V_W, lambda a: a),
        (g_ref, 0, RET_V_W, _swish),
        (p_ref, 0, POOL_W, lambda a: a),
        (qm_ref, 0, MEM_Q_W, lambda a: a),
    ]
    chunk = RET_CHUNK
    n_chunks = x_ref.shape[0] // chunk
    outer_jobs = list(range(n_chunks))

    def outer_job():
        ci = outer_jobs.pop(0)
        rows = slice(ci * chunk, (ci + 1) * chunk)
        for hd, outer in enumerate(_chunk_outer(qk_ref[rows, RET_QK_W:], v_ref, rows, zeta_f_ref[0])):
            outer_scr[ci, hd] = outer

    def scan():
        for ci in range(n_chunks):
            for hd in range(RET_HEADS):
                state_f_ref[ci, hd] = acc[hd].astype(BF16)
            _decay_and_add(acc, outer_scr[ci], dec_ref[0, 0])

    w_col = 0
    scanned = False
    for out_ref, out_col, width, epilogue in groups:
        for c in range(0, width, IN_PROJ_CHUNK):
            a = _dot(h, w_ref[:, w_col + c:w_col + c + IN_PROJ_CHUNK])
            out_ref[:, out_col + c:out_col + c + IN_PROJ_CHUNK] = epilogue(a).astype(out_ref.dtype)
            if out_ref is not qk_ref and out_ref is not v_ref:
                if outer_jobs:
                    outer_job()
                elif not scanned:
                    scan()
                    scanned = True
        w_col += width
    assert scanned and not outer_jobs


def _in_proj(x2, gains, w_in, cos, sin, zeta_f, dec, layer, seq):
    t, d = x2.shape
    tm = ROW_TILE
    c = RET_CHUNK
    pos_tiles = seq // tm
    w_cols = GATE_COL0

    def rows(width):
        return pl.BlockSpec((tm, width), lambda i: (i, 0))

    pos_spec = pl.BlockSpec((tm, RET_QK_DIM), lambda i: (i % pos_tiles, 0))
    w_spec = pl.BlockSpec((None, d, w_cols), lambda i: (layer, 0, 0), pipeline_mode=pl.Buffered(1))
    state_shape = (RET_HEADS, RET_QK_DIM, RET_V_DIM)
    return pl.pallas_call(
        functools.partial(_in_proj_kernel, tiles_per_seq=pos_tiles),
        grid=(t // tm,),
        in_specs=[rows(d), _layer_resident(gains.shape, layer), w_spec, pos_spec, pos_spec,
                  pl.BlockSpec((1, c, RET_QK_W), lambda i: (layer, 0, 0), pipeline_mode=pl.Buffered(1)),
                  pl.BlockSpec((1, 2, 1, RET_QK_W), lambda i: (layer, 0, 0, 0), pipeline_mode=pl.Buffered(1))],
        out_specs=[rows(2 * RET_QK_W), rows(RET_V_W), rows(RET_V_W), rows(POOL_W), rows(MEM_Q_W),
                   pl.BlockSpec((tm // c,) + state_shape, lambda i: (i, 0, 0, 0)), rows(d)],
        out_shape=[jax.ShapeDtypeStruct((t, 2 * RET_QK_W), BF16),
                   jax.ShapeDtypeStruct((t, RET_V_W), BF16),
                   jax.ShapeDtypeStruct((t, RET_V_W), BF16),
                   jax.ShapeDtypeStruct((t, POOL_W), F32),
                   jax.ShapeDtypeStruct((t, MEM_Q_W), BF16),
                   jax.ShapeDtypeStruct((t // c,) + state_shape, BF16),
                   jax.ShapeDtypeStruct((t, d), BF16)],
        scratch_shapes=[pltpu.VMEM(state_shape, F32), pltpu.VMEM((tm // c,) + state_shape, F32)],
        compiler_params=_params("arbitrary"),
        name="in_proj",
    )(x2, gains, w_in, cos, sin, zeta_f, dec)


def _window_sums(padded):
    length = padded.shape[0]
    halo = POOL_HALO
    ts = length - 2 * halo

    def ahead(a, k):
        return pltpu.roll(a, length - k, axis=0)

    def behind(a, k):
        return pltpu.roll(a, k, axis=0)

    sums = []
    for gi, w in enumerate(POOL_WINDOWS):
        a = padded[:, gi * POOL_GROUP:(gi + 1) * POOL_GROUP]
        span = 1
        while 2 * span < w:
            a = a + ahead(a, span)
            span *= 2
        assert 2 * span == w and span <= halo
        sums.append((a + behind(a, span))[halo:halo + ts, :])
    return sums


def _mix_kernel(x_ref, half_h_ref, qk_ref, v_ref, g_ref, p_ref, p_prev_ref, p_next_ref, inv_count_ref, qm_ref,
                state_f_ref, kv_ref, dmat_ref, xi_f_ref, xi_b_ref, zeta_b_ref, dec_ref,
                w_gate_ret_ref, w_gate_pool_ref, w_gate_mem_ref,
                w_ret_o_ref, w_pool_ref, w_mem_o_ref, w_out_ref,
                out_ref, ret_scr, gate_scr, acc_b):
    ts = x_ref.shape[0]
    c = RET_CHUNK
    n_tiles = pl.num_programs(1)
    tile = n_tiles - 1 - pl.program_id(1)

    @pl.when(pl.program_id(1) == 0)
    def _():
        acc_b[...] = jnp.zeros_like(acc_b)

    d = x_ref.shape[1]
    fill_cols = list(range(0, d, MIX_FILL_CHUNK))

    def out_proj_chunk(lhs, w_ref, col):
        return _dot(lhs, w_ref[:, col:col + MIX_FILL_CHUNK]).astype(BF16)

    gate_w_refs = (w_gate_ret_ref, w_gate_pool_ref, w_gate_mem_ref)
    gate_jobs = [(b, col) for b in range(N_BRANCHES) for col in fill_cols]

    def gate_job():
        b, col = gate_jobs.pop(0)
        z_half = _dot(half_h_ref[...], gate_w_refs[b][:, col:col + MIX_FILL_CHUNK])
        gate_scr[:, b * d + col:b * d + col + MIX_FILL_CHUNK] = jnp.tanh(z_half).astype(BF16) * 0.5 + 0.5

    xi_f = xi_f_ref[0]
    xi_b = xi_b_ref[0]
    def normalise_and_gate(o, rows, v_cols):
        mu = jnp.mean(o, axis=-1, keepdims=True)
        cen = o - mu
        var = jnp.mean(cen * cen, axis=-1, keepdims=True)
        o_n = cen * lax.rsqrt(var + EPS)
        ret_scr[rows, v_cols] = o_n.astype(BF16) * g_ref[rows, v_cols]

    pending = None
    for ci in reversed(range(ts // c)):
        rows = slice(ci * c, (ci + 1) * c)
        q = qk_ref[rows, :RET_QK_W]
        q_f = q * xi_f
        q_b = q * xi_b
        for h in range(RET_HEADS):
            qk_cols = slice(h * RET_QK_DIM, (h + 1) * RET_QK_DIM)
            v_cols = slice(h * RET_V_DIM, (h + 1) * RET_V_DIM)
            k_h = qk_ref[rows, RET_QK_W + h * RET_QK_DIM:RET_QK_W + (h + 1) * RET_QK_DIM]
            s = _dot_nt(q[:, qk_cols], k_h) * dmat_ref[0, h]
            q_fb = jnp.concatenate([q_f[:, qk_cols], q_b[:, qk_cols]], axis=1)
            state = jnp.concatenate([state_f_ref[ci, h], acc_b[h].astype(BF16)], axis=0)
            o = _dot(s.astype(BF16), v_ref[rows, v_cols]) + _dot(q_fb, state)
            if pending is not None:
                normalise_and_gate(*pending)
                if gate_jobs:
                    gate_job()
            pending = (o, rows, v_cols)
        _decay_and_add(acc_b, _chunk_outer(qk_ref[rows, RET_QK_W:], v_ref, rows, zeta_b_ref[0]), dec_ref[0, 1])
    normalise_and_gate(*pending)
    if gate_jobs:
        gate_job()

    p = p_ref[...]
    padded = jnp.concatenate([jnp.where(tile > 0, p_prev_ref[...], 0.0), p,
                              jnp.where(tile < n_tiles - 1, p_next_ref[...], 0.0)], axis=0)
    ret_lhs = ret_scr[...]
    groups, o_ret = [], []
    for gi, win in enumerate(_window_sums(padded)):
        cols = slice(gi * POOL_GROUP, (gi + 1) * POOL_GROUP)
        groups.append((win * inv_count_ref[:, cols] - p[:, cols]).astype(BF16))
        if gi < len(fill_cols):
            o_ret.append(out_proj_chunk(ret_lhs, w_ret_o_ref, fill_cols[gi]))
    o_ret += [out_proj_chunk(ret_lhs, w_ret_o_ref, col) for col in fill_cols[len(o_ret):]]
    pool_lhs = jnp.concatenate(groups, axis=1)

    exp2_scale = (MEM_HEAD_DIM ** -0.5) * LOG2_E
    heads, o_pool = [], []
    for h in range(MEM_HEADS):
        cols = slice(h * MEM_HEAD_DIM, (h + 1) * MEM_HEAD_DIM)
        k_h = kv_ref[0, :, cols]
        v_h = kv_ref[0, :, MEM_Q_W + h * MEM_HEAD_DIM:MEM_Q_W + (h + 1) * MEM_HEAD_DIM]
        s = _dot_nt(qm_ref[:, cols], k_h)
        e = jnp.exp2((s - jnp.max(s, axis=-1, keepdims=True)) * exp2_scale)
        o = _dot(e.astype(BF16), v_h) / jnp.sum(e, axis=-1, keepdims=True)
        heads.append(o.astype(BF16))
        if gate_jobs:
            gate_job()
        if h < len(fill_cols):
            o_pool.append(out_proj_chunk(pool_lhs, w_pool_ref, fill_cols[h]))
    while gate_jobs:
        gate_job()
    o_pool += [out_proj_chunk(pool_lhs, w_pool_ref, col) for col in fill_cols[len(o_pool):]]
    mem_lhs = jnp.concatenate(heads, axis=1)

    merged = []
    for j, col in enumerate(fill_cols):
        o_mem = out_proj_chunk(mem_lhs, w_mem_o_ref, col)
        gates = [gate_scr[:, b * d + col:b * d + col + MIX_FILL_CHUNK] for b in range(N_BRANCHES)]
        merged.append(gates[0] * o_ret[j] + gates[1] * o_pool[j] + gates[2] * o_mem)
    out_ref[...] = x_ref[...] + _dot(jnp.concatenate(merged, axis=1), w_out_ref[...])


def _mix(x2, half_h, qk, v, g, p, inv_count, qm, states_f, kv, dmat, xi_f, xi_b, zeta_b, dec, w_in, w_ret_o, w_pool,
         w_mem_o, w_out, layer, batch, seq):
    t, d = x2.shape
    assert GATE_COL0 % d == 0

    def gate_weight(branch):
        return pl.BlockSpec((None, d, d), lambda b, i: (layer, 0, GATE_COL0 // d + branch),
                            pipeline_mode=pl.Buffered(1))

    ts = MIX_TILE
    c = RET_CHUNK
    nt = seq // ts
    halo = POOL_HALO
    halo_per_tile = ts // halo
    last_halo = t // halo - 1

    def tile_row(b, i):
        return b * nt + nt - 1 - i

    def rows(width):
        return pl.BlockSpec((ts, width), lambda b, i: (tile_row(b, i), 0))

    p_prev = pl.BlockSpec((halo, POOL_W), lambda b, i: (jnp.maximum(tile_row(b, i) * halo_per_tile - 1, 0), 0))
    p_next = pl.BlockSpec((halo, POOL_W),
                          lambda b, i: (jnp.minimum((tile_row(b, i) + 1) * halo_per_tile, last_halo), 0))
    state = pl.BlockSpec((ts // c,) + states_f.shape[1:], lambda b, i: (tile_row(b, i), 0, 0, 0))
    kv_spec = pl.BlockSpec((1,) + kv.shape[2:], lambda b, i: (layer * batch + b, 0, 0))
    dmat_spec = pl.BlockSpec((1, RET_HEADS, c, c), lambda b, i: (layer, 0, 0, 0), pipeline_mode=pl.Buffered(1))
    table_spec = pl.BlockSpec((1, c, RET_QK_W), lambda b, i: (layer, 0, 0), pipeline_mode=pl.Buffered(1))
    dec_spec = pl.BlockSpec((1, 2, 1, RET_QK_W), lambda b, i: (layer, 0, 0, 0), pipeline_mode=pl.Buffered(1))
    return pl.pallas_call(
        _mix_kernel,
        grid=(batch, nt),
        in_specs=[rows(d), rows(d), rows(2 * RET_QK_W), rows(RET_V_W), rows(RET_V_W),
                  rows(POOL_W), p_prev, p_next, pl.BlockSpec((ts, POOL_W), lambda b, i: (nt - 1 - i, 0)),
                  rows(MEM_Q_W), state, kv_spec, dmat_spec, table_spec, table_spec, table_spec, dec_spec,
                  gate_weight(0), gate_weight(1), gate_weight(2),
                  _layer_resident(w_ret_o.shape, layer), _layer_resident(w_pool.shape, layer),
                  _layer_resident(w_mem_o.shape, layer), _layer_resident(w_out.shape, layer)],
        out_specs=rows(d),
        out_shape=jax.ShapeDtypeStruct((t, d), F32),
        scratch_shapes=[pltpu.VMEM((ts, RET_V_W), BF16), pltpu.VMEM((ts, N_BRANCHES * d), BF16),
                        pltpu.VMEM((RET_HEADS, RET_QK_DIM, RET_V_DIM), F32)],
        compiler_params=_params("arbitrary", "arbitrary"),
        name="mix",
    )(x2, half_h, qk, v, g, p, p, p, inv_count, qm, states_f, kv.reshape((-1,) + kv.shape[2:]), dmat, xi_f, xi_b,
      zeta_b, dec, w_in, w_in, w_in, w_ret_o, w_pool, w_mem_o, w_out)


def _mlp_kernel(x_ref, gain_ref, w1_ref, w2_ref, final_gain_ref, out_ref, hid_scr, *, final_norm):
    x = x_ref[...]
    h = (x * gain_ref[...]).astype(BF16)
    for col in range(0, w1_ref.shape[1], COL_CHUNK):
        hid = jnp.maximum(_dot(h, w1_ref[:, col:col + COL_CHUNK]), 0.0)
        hid_scr[:, col:col + COL_CHUNK] = (hid * hid).astype(BF16)
    acc = _dot(hid_scr[...], w2_ref[...])
    out = x + acc / (jnp.mean(x * x, axis=-1, keepdims=True) + EPS)
    out_ref[...] = _rms_norm(out, final_gain_ref[...]) if final_norm else out


def _mlp(x2, gains, w1, w2, final_gain, layer, final_norm):
    t, d = x2.shape
    tm = MLP_TILE
    rows = pl.BlockSpec((tm, d), lambda i: (i, 0))
    return pl.pallas_call(
        functools.partial(_mlp_kernel, final_norm=final_norm),
        grid=(t // tm,),
        in_specs=[rows, _layer_resident(gains.shape, layer), _layer_resident(w1.shape, layer),
                  _layer_resident(w2.shape, layer), _resident((1, d))],
        out_specs=rows,
        out_shape=jax.ShapeDtypeStruct((t, d), F32),
        scratch_shapes=[pltpu.VMEM((tm, w1.shape[2]), BF16)],
        compiler_params=_params("arbitrary"),
        name="mlp",
    )(x2, gains, w1, w2, final_gain.reshape(1, d))


def _pool_weight_kernel(w_grp_ref, scale_ref, w_o_ref, out_ref):
    for gi in range(len(POOL_WINDOWS)):
        rows = slice(gi * POOL_GROUP, (gi + 1) * POOL_GROUP)
        scaled = w_grp_ref[0, gi] * scale_ref[0, :, rows]
        out_ref[0, rows, :] = jnp.dot(scaled, w_o_ref[0, rows, :], preferred_element_type=F32,
                                      precision=lax.Precision.HIGHEST).astype(BF16)


def _pool_weights(w_pool_grp, pool_scale, w_pool_o):
    depth, groups, group_w, _ = w_pool_grp.shape
    d = w_pool_o.shape[2]
    return pl.pallas_call(
        _pool_weight_kernel,
        grid=(depth,),
        in_specs=[pl.BlockSpec((1, groups, group_w, group_w), lambda l: (l, 0, 0, 0)),
                  pl.BlockSpec((1, 1, POOL_W), lambda l: (l, 0, 0)),
                  pl.BlockSpec((1, POOL_W, d), lambda l: (l, 0, 0))],
        out_specs=pl.BlockSpec((1, POOL_W, d), lambda l: (l, 0, 0)),
        out_shape=jax.ShapeDtypeStruct((depth, POOL_W, d), BF16),
        compiler_params=_params("arbitrary"),
        name="pool_weights",
    )(w_pool_grp, pool_scale.reshape(depth, 1, POOL_W), w_pool_o)


def _rotary_tables(seq):
    inv = ROPE_BASE ** (-jnp.arange(0, RET_QK_DIM, 2, dtype=F32) / RET_QK_DIM)
    ang = jnp.arange(seq, dtype=F32)[:, None] * inv[None, :]
    cos, sin = jnp.cos(ang), jnp.sin(ang)
    return jnp.concatenate([cos, cos], axis=1), jnp.concatenate([-sin, sin], axis=1)


def _pool_inv_counts(seq):
    pos = jnp.arange(seq)
    cols = []
    for w in POOL_WINDOWS:
        count = jnp.minimum(pos + w // 2, seq) - jnp.maximum(pos - w // 2, 0)
        cols.append(jnp.broadcast_to((1.0 / count.astype(F32))[:, None], (seq, POOL_GROUP)))
    return jnp.concatenate(cols, axis=1)


def kernel(x, mem, w_in, ret_decay_logit, w_ret_o, w_pool_grp, pool_scale, w_pool_o, w_mem_kv, w_mem_o,
           w_out, w_ff1, w_ff2, norm1_g, norm2_g, mem_norm_g, final_norm_g):
    batch, seq, d = x.shape
    depth = w_in.shape[0]
    assert seq % MIX_TILE == 0 and seq % ROW_TILE == 0 and MIX_TILE % RET_CHUNK == 0
    assert d % COL_CHUNK == 0 and POOL_HALO % 8 == 0

    cos, sin = _rotary_tables(seq)
    inv_count = _pool_inv_counts(seq)
    dmat, xi_f, xi_b, zeta_f, zeta_b, dec = _decay_tables(ret_decay_logit)
    kv = _mem_kv(mem, mem_norm_g, w_mem_kv.astype(BF16))

    w_pool = _pool_weights(w_pool_grp, pool_scale, w_pool_o)
    w_in_b = w_in.astype(BF16)
    w_ret_o_b = w_ret_o.astype(BF16)
    w_mem_o_b = w_mem_o.astype(BF16)
    w_out_b = w_out.astype(BF16)
    w_ff1_b = w_ff1.astype(BF16)
    w_ff2_b = w_ff2.astype(BF16)
    gains1 = norm1_g.reshape(depth, 1, d)
    gains2 = norm2_g.reshape(depth, 1, d)

    x2 = x.reshape(batch * seq, d)
    for l in range(depth):
        qk, v, g, p, qm, states_f, half_h = _in_proj(x2, gains1, w_in_b, cos, sin, zeta_f, dec, l, seq)
        x2 = _mix(x2, half_h, qk, v, g, p, inv_count, qm, states_f, kv, dmat, xi_f, xi_b, zeta_b, dec, w_in_b,
                  w_ret_o_b, w_pool, w_mem_o_b, w_out_b, l, batch, seq)
        x2 = _mlp(x2, gains2, w_ff1_b, w_ff2_b, final_norm_g, l, l == depth - 1)
    return x2.reshape(batch, seq, d)
```

```python
import functools

import jax
import jax.numpy as jnp
from jax import lax
from jax.experimental import pallas as pl
from jax.experimental.pallas import tpu as pltpu

F32 = jnp.float32
BF16 = jnp.bfloat16

RET_HEADS = 4
RET_QK_DIM = 128
RET_V_DIM = 256
MEM_HEADS = 4
MEM_HEAD_DIM = 128
POOL_WINDOWS = (2, 4, 8, 16)
POOL_GROUP = 128
N_BRANCHES = 3
ROPE_BASE = 10000.0
EPS = 1e-6
LOG2_E = 1.4426950408889634

RET_QK_W = RET_HEADS * RET_QK_DIM
RET_V_W = RET_HEADS * RET_V_DIM
MEM_Q_W = MEM_HEADS * MEM_HEAD_DIM
POOL_W = POOL_GROUP * len(POOL_WINDOWS)
POOL_HALO = max(POOL_WINDOWS) // 2
GATE_COL0 = 2 * RET_QK_W + 2 * RET_V_W + POOL_W + MEM_Q_W

V7X_VMEM_BYTES = 64 * 1024 * 1024
VMEM_LIMIT_BYTES = V7X_VMEM_BYTES - 4 * 1024 * 1024

RET_CHUNK = 256
ROW_TILE = 1024
MLP_TILE = 1024
MIX_TILE = 512
COL_CHUNK = 512
IN_PROJ_CHUNK = 256
MIX_FILL_CHUNK = 256


def _params(*semantics):
    return pltpu.CompilerParams(dimension_semantics=semantics, vmem_limit_bytes=VMEM_LIMIT_BYTES)


def _resident(shape):
    zeros = (0,) * len(shape)
    return pl.BlockSpec(shape, lambda *_: zeros, pipeline_mode=pl.Buffered(1))


def _layer_resident(stacked_shape, layer):
    index = (layer,) + (0,) * (len(stacked_shape) - 1)
    return pl.BlockSpec((None,) + tuple(stacked_shape[1:]), lambda *_: index, pipeline_mode=pl.Buffered(1))


def _rms_norm(x, gain):
    return x * lax.rsqrt(jnp.mean(x * x, axis=-1, keepdims=True) + EPS) * gain


def _swish(x):
    half = 0.5 * x
    return half + half * jnp.tanh(half)


_dot = functools.partial(jnp.dot, preferred_element_type=F32)
_dot_nt = functools.partial(lax.dot_general, dimension_numbers=(((1,), (1,)), ((), ())), preferred_element_type=F32)
_dot_tn = functools.partial(lax.dot_general, dimension_numbers=(((0,), (0,)), ((), ())), preferred_element_type=F32)


def _decay_tables_kernel(logit_ref, dmat_ref, xi_f_ref, xi_b_ref, zeta_f_ref, zeta_b_ref, dec_ref):
    c = RET_CHUNK
    logit = logit_ref[0]
    log_g = jnp.minimum(logit, 0.0) - jnp.log1p(jnp.exp(-jnp.abs(logit)))
    lg_f, lg_b = log_g[0], log_g[1]
    row = lax.broadcasted_iota(jnp.int32, (c, RET_QK_W), 0).astype(F32)
    xi_f_ref[0] = jnp.exp(lg_f * (row + 1.0)).astype(BF16)
    xi_b_ref[0] = jnp.exp(lg_b * (c - row)).astype(BF16)
    zeta_f_ref[0] = jnp.exp(lg_f * (c - 1.0 - row)).astype(BF16)
    zeta_b_ref[0] = jnp.exp(lg_b * row).astype(BF16)
    dec_ref[0, 0] = jnp.exp(lg_f * c)
    dec_ref[0, 1] = jnp.exp(lg_b * c)
    i = lax.broadcasted_iota(jnp.int32, (c, c), 0)
    j = lax.broadcasted_iota(jnp.int32, (c, c), 1)
    diff = (i - j).astype(F32)
    for h in range(RET_HEADS):
        lf = lg_f[:, h * RET_QK_DIM:h * RET_QK_DIM + 1]
        lb = lg_b[:, h * RET_QK_DIM:h * RET_QK_DIM + 1]
        fwd = jnp.exp(lf * jnp.maximum(diff, 0.0))
        bwd = jnp.exp(lb * jnp.maximum(-diff, 0.0))
        dmat_ref[0, h] = jnp.where(diff >= 0.0, fwd, bwd)


def _decay_tables(ret_decay_logit):
    depth = ret_decay_logit.shape[0]
    c = RET_CHUNK
    logit = jnp.repeat(ret_decay_logit.astype(F32), RET_QK_DIM, axis=-1)[:, :, None, :]
    vec = jax.ShapeDtypeStruct((depth, c, RET_QK_W), BF16)
    vec_spec = pl.BlockSpec((1, c, RET_QK_W), lambda l: (l, 0, 0))
    return pl.pallas_call(
        _decay_tables_kernel,
        grid=(depth,),
        in_specs=[pl.BlockSpec((1, 2, 1, RET_QK_W), lambda l: (l, 0, 0, 0))],
        out_specs=[pl.BlockSpec((1, RET_HEADS, c, c), lambda l: (l, 0, 0, 0)),
                   vec_spec, vec_spec, vec_spec, vec_spec,
                   pl.BlockSpec((1, 2, 1, RET_QK_W), lambda l: (l, 0, 0, 0))],
        out_shape=[jax.ShapeDtypeStruct((depth, RET_HEADS, c, c), F32), vec, vec, vec, vec,
                   jax.ShapeDtypeStruct((depth, 2, 1, RET_QK_W), F32)],
        compiler_params=_params("arbitrary"),
        name="decay_tables",
    )(logit)


def _mem_kv_kernel(mem_ref, gain_ref, w_ref, kv_ref):
    mem_n = _rms_norm(mem_ref[...], gain_ref[...]).astype(BF16)
    for l in range(w_ref.shape[0]):
        kv_ref[l] = _dot(mem_n, w_ref[l]).astype(BF16)


def _mem_kv(mem, mem_norm_g, w_mem_kv):
    b, m, d = mem.shape
    depth, _, kvw = w_mem_kv.shape
    rows = b * m
    tm = min(MLP_TILE, rows)
    assert rows % tm == 0
    kv = pl.pallas_call(
        _mem_kv_kernel,
        grid=(rows // tm,),
        in_specs=[pl.BlockSpec((tm, d), lambda i: (i, 0)),
                  _resident((1, d)),
                  _resident((depth, d, kvw))],
        out_specs=pl.BlockSpec((depth, tm, kvw), lambda i: (0, i, 0)),
        out_shape=jax.ShapeDtypeStruct((depth, rows, kvw), BF16),
        compiler_params=_params("arbitrary"),
        name="mem_kv",
    )(mem.reshape(rows, d), mem_norm_g.reshape(1, d), w_mem_kv)
    return kv.reshape(depth, b, m, kvw)


def _chunk_outer(k, v_ref, rows, zeta):
    kz = k * zeta
    return [_dot_tn(kz[:, h * RET_QK_DIM:(h + 1) * RET_QK_DIM], v_ref[rows, h * RET_V_DIM:(h + 1) * RET_V_DIM])
            for h in range(RET_HEADS)]


def _decay_and_add(acc, outer, dec):
    for h in range(RET_HEADS):
        acc[h] = acc[h] * dec[:, h * RET_QK_DIM:h * RET_QK_DIM + 1] + outer[h]


def _in_proj_kernel(x_ref, gain_ref, w_ref, cos_ref, sin_ref, zeta_f_ref, dec_ref,
                    qk_ref, v_ref, g_ref, p_ref, qm_ref, state_f_ref, half_h_ref, acc, outer_scr, *, tiles_per_seq):
    @pl.when(pl.program_id(0) % tiles_per_seq == 0)
    def _():
        acc[...] = jnp.zeros_like(acc)

    h = _rms_norm(x_ref[...], gain_ref[...]).astype(BF16)
    half_h_ref[...] = h * 0.5
    cos = cos_ref[...]
    sin = sin_ref[...]

    def rotary(a, scale):
        heads = []
        for hd in range(a.shape[1] // RET_QK_DIM):
            ah = a[:, hd * RET_QK_DIM:(hd + 1) * RET_QK_DIM]
            heads.append(ah * cos + pltpu.roll(ah, RET_QK_DIM // 2, axis=1) * sin)
        rotated = jnp.concatenate(heads, axis=1)
        return rotated if scale is None else rotated * scale

    groups = [
        (qk_ref, 0, RET_QK_W, lambda a: rotary(a, None)),
        (qk_ref, RET_QK_W, RET_QK_W, lambda a: rotary(a, RET_QK_DIM ** -0.5)),
        (v_ref, 0, RET_V_W, lambda a: a),
        (g_ref, 0, RET_V_W, _swish),
        (p_ref, 0, POOL_W, lambda a: a),
        (qm_ref, 0, MEM_Q_W, lambda a: a),
    ]
    chunk = RET_CHUNK
    n_chunks = x_ref.shape[0] // chunk
    outer_jobs = list(range(n_chunks))

    def outer_job():
        ci = outer_jobs.pop(0)
        rows = slice(ci * chunk, (ci + 1) * chunk)
        for hd, outer in enumerate(_chunk_outer(qk_ref[rows, RET_QK_W:], v_ref, rows, zeta_f_ref[0])):
            outer_scr[ci, hd] = outer

    def scan():
        for ci in range(n_chunks):
            for hd in range(RET_HEADS):
                state_f_ref[ci, hd] = acc[hd].astype(BF16)
            _decay_and_add(acc, outer_scr[ci], dec_ref[0, 0])

    w_col = 0
    scanned = False
    for out_ref, out_col, width, epilogue in groups:
        for c in range(0, width, IN_PROJ_CHUNK):
            a = _dot(h, w_ref[:, w_col + c:w_col + c + IN_PROJ_CHUNK])
            out_ref[:, out_col + c:out_col + c + IN_PROJ_CHUNK] = epilogue(a).astype(out_ref.dtype)
            if out_ref is not qk_ref and out_ref is not v_ref:
                if outer_jobs:
                    outer_job()
                elif not scanned:
                    scan()
                    scanned = True
        w_col += width
    assert scanned and not outer_jobs


def _in_proj(x2, gains, w_in, cos, sin, zeta_f, dec, layer, seq):
    t, d = x2.shape
    tm = ROW_TILE
    c = RET_CHUNK
    pos_tiles = seq // tm
    w_cols = GATE_COL0

    def rows(width):
        return pl.BlockSpec((tm, width), lambda i: (i, 0))

    pos_spec = pl.BlockSpec((tm, RET_QK_DIM), lambda i: (i % pos_tiles, 0))
    w_spec = pl.BlockSpec((None, d, w_cols), lambda i: (layer, 0, 0), pipeline_mode=pl.Buffered(1))
    state_shape = (RET_HEADS, RET_QK_DIM, RET_V_DIM)
    return pl.pallas_call(
        functools.partial(_in_proj_kernel, tiles_per_seq=pos_tiles),
        grid=(t // tm,),
        in_specs=[rows(d), _layer_resident(gains.shape, layer), w_spec, pos_spec, pos_spec,
                  pl.BlockSpec((1, c, RET_QK_W), lambda i: (layer, 0, 0), pipeline_mode=pl.Buffered(1)),
                  pl.BlockSpec((1, 2, 1, RET_QK_W), lambda i: (layer, 0, 0, 0), pipeline_mode=pl.Buffered(1))],
        out_specs=[rows(2 * RET_QK_W), rows(RET_V_W), rows(RET_V_W), rows(POOL_W), rows(MEM_Q_W),
                   pl.BlockSpec((tm // c,) + state_shape, lambda i: (i, 0, 0, 0)), rows(d)],
        out_shape=[jax.ShapeDtypeStruct((t, 2 * RET_QK_W), BF16),
                   jax.ShapeDtypeStruct((t, RET_V_W), BF16),
                   jax.ShapeDtypeStruct((t, RET_V_W), BF16),
                   jax.ShapeDtypeStruct((t, POOL_W), F32),
                   jax.ShapeDtypeStruct((t, MEM_Q_W), BF16),
                   jax.ShapeDtypeStruct((t // c,) + state_shape, BF16),
                   jax.ShapeDtypeStruct((t, d), BF16)],
        scratch_shapes=[pltpu.VMEM(state_shape, F32), pltpu.VMEM((tm // c,) + state_shape, F32)],
        compiler_params=_params("arbitrary"),
        name="in_proj",
    )(x2, gains, w_in, cos, sin, zeta_f, dec)


def _window_sums(padded):
    length = padded.shape[0]
    halo = POOL_HALO
    ts = length - 2 * halo

    def ahead(a, k):
        return pltpu.roll(a, length - k, axis=0)

    def behind(a, k):
        return pltpu.roll(a, k, axis=0)

    sums = []
    for gi, w in enumerate(POOL_WINDOWS):
        a = padded[:, gi * POOL_GROUP:(gi + 1) * POOL_GROUP]
        span = 1
        while 2 * span < w:
            a = a + ahead(a, span)
            span *= 2
        assert 2 * span == w and span <= halo
        sums.append((a + behind(a, span))[halo:halo + ts, :])
    return sums


def _mix_kernel(x_ref, half_h_ref, qk_ref, v_ref, g_ref, p_ref, p_prev_ref, p_next_ref, inv_count_ref, qm_ref,
                state_f_ref, kv_ref, dmat_ref, xi_f_ref, xi_b_ref, zeta_b_ref, dec_ref,
                w_gate_ret_ref, w_gate_pool_ref, w_gate_mem_ref,
                w_ret_o_ref, w_pool_ref, w_mem_o_ref, w_out_ref,
                out_ref, ret_scr, gate_scr, acc_b):
    ts = x_ref.shape[0]
    c = RET_CHUNK
    n_tiles = pl.num_programs(1)
    tile = n_tiles - 1 - pl.program_id(1)

    @pl.when(pl.program_id(1) == 0)
    def _():
        acc_b[...] = jnp.zeros_like(acc_b)

    d = x_ref.shape[1]
    fill_cols = list(range(0, d, MIX_FILL_CHUNK))

    def out_proj_chunk(lhs, w_ref, col):
        return _dot(lhs, w_ref[:, col:col + MIX_FILL_CHUNK]).astype(BF16)

    gate_w_refs = (w_gate_ret_ref, w_gate_pool_ref, w_gate_mem_ref)
    gate_jobs = [(b, col) for b in range(N_BRANCHES) for col in fill_cols]

    def gate_job():
        b, col = gate_jobs.pop(0)
        z_half = _dot(half_h_ref[...], gate_w_refs[b][:, col:col + MIX_FILL_CHUNK])
        gate_scr[:, b * d + col:b * d + col + MIX_FILL_CHUNK] = jnp.tanh(z_half).astype(BF16) * 0.5 + 0.5

    xi_f = xi_f_ref[0]
    xi_b = xi_b_ref[0]
    def normalise_and_gate(o, rows, v_cols):
        mu = jnp.mean(o, axis=-1, keepdims=True)
        cen = o - mu
        var = jnp.mean(cen * cen, axis=-1, keepdims=True)
        o_n = cen * lax.rsqrt(var + EPS)
        ret_scr[rows, v_cols] = o_n.astype(BF16) * g_ref[rows, v_cols]

    pending = None
    for ci in reversed(range(ts // c)):
        rows = slice(ci * c, (ci + 1) * c)
        q = qk_ref[rows, :RET_QK_W]
        q_f = q * xi_f
        q_b = q * xi_b
        for h in range(RET_HEADS):
            qk_cols = slice(h * RET_QK_DIM, (h + 1) * RET_QK_DIM)
            v_cols = slice(h * RET_V_DIM, (h + 1) * RET_V_DIM)
            k_h = qk_ref[rows, RET_QK_W + h * RET_QK_DIM:RET_QK_W + (h + 1) * RET_QK_DIM]
            s = _dot_nt(q[:, qk_cols], k_h) * dmat_ref[0, h]
            q_fb = jnp.concatenate([q_f[:, qk_cols], q_b[:, qk_cols]], axis=1)
            state = jnp.concatenate([state_f_ref[ci, h], acc_b[h].astype(BF16)], axis=0)
            o = _dot(s.astype(BF16), v_ref[rows, v_cols]) + _dot(q_fb, state)
            if pending is not None:
                normalise_and_gate(*pending)
                if gate_jobs:
                    gate_job()
            pending = (o, rows, v_cols)
        _decay_and_add(acc_b, _chunk_outer(qk_ref[rows, RET_QK_W:], v_ref, rows, zeta_b_ref[0]), dec_ref[0, 1])
    normalise_and_gate(*pending)
    if gate_jobs:
        gate_job()

    p = p_ref[...]
    padded = jnp.concatenate([jnp.where(tile > 0, p_prev_ref[...], 0.0), p,
                              jnp.where(tile < n_tiles - 1, p_next_ref[...], 0.0)], axis=0)
    ret_lhs = ret_scr[...]
    groups, o_ret = [], []
    for gi, win in enumerate(_window_sums(padded)):
        cols = slice(gi * POOL_GROUP, (gi + 1) * POOL_GROUP)
        groups.append((win * inv_count_ref[:, cols] - p[:, cols]).astype(BF16))
        if gi < len(fill_cols):
            o_ret.append(out_proj_chunk(ret_lhs, w_ret_o_ref, fill_cols[gi]))
    o_ret += [out_proj_chunk(ret_lhs, w_ret_o_ref, col) for col in fill_cols[len(o_ret):]]
    pool_lhs = jnp.concatenate(groups, axis=1)

    exp2_scale = (MEM_HEAD_DIM ** -0.5) * LOG2_E
    heads, o_pool = [], []

    def scores(h):
        cols = slice(h * MEM_HEAD_DIM, (h + 1) * MEM_HEAD_DIM)
        return _dot_nt(qm_ref[:, cols], kv_ref[0, :, cols])

    def attend(h, s):
        v_h = kv_ref[0, :, MEM_Q_W + h * MEM_HEAD_DIM:MEM_Q_W + (h + 1) * MEM_HEAD_DIM]
        e = jnp.exp2((s - jnp.max(s, axis=-1, keepdims=True)) * exp2_scale)
        o = _dot(e.astype(BF16), v_h) / jnp.sum(e, axis=-1, keepdims=True)
        heads.append(o.astype(BF16))
        if gate_jobs:
            gate_job()
        if h < len(fill_cols):
            o_pool.append(out_proj_chunk(pool_lhs, w_pool_ref, fill_cols[h]))

    s_next = scores(0)
    for h in range(MEM_HEADS):
        s_cur = s_next
        if h + 1 < MEM_HEADS:
            s_next = scores(h + 1)
        attend(h, s_cur)
    while gate_jobs:
        gate_job()
    o_pool += [out_proj_chunk(pool_lhs, w_pool_ref, col) for col in fill_cols[len(o_pool):]]
    mem_lhs = jnp.concatenate(heads, axis=1)

    merged = []
    for j, col in enumerate(fill_cols):
        o_mem = out_proj_chunk(mem_lhs, w_mem_o_ref, col)
        gates = [gate_scr[:, b * d + col:b * d + col + MIX_FILL_CHUNK] for b in range(N_BRANCHES)]
        merged.append(gates[0] * o_ret[j] + gates[1] * o_pool[j] + gates[2] * o_mem)
    out_ref[...] = x_ref[...] + _dot(jnp.concatenate(merged, axis=1), w_out_ref[...])


def _mix(x2, half_h, qk, v, g, p, inv_count, qm, states_f, kv, dmat, xi_f, xi_b, zeta_b, dec, w_in, w_ret_o, w_pool,
         w_mem_o, w_out, layer, batch, seq):
    t, d = x2.shape
    assert GATE_COL0 % d == 0

    def gate_weight(branch):
        return pl.BlockSpec((None, d, d), lambda b, i: (layer, 0, GATE_COL0 // d + branch),
                            pipeline_mode=pl.Buffered(1))

    ts = MIX_TILE
    c = RET_CHUNK
    nt = seq // ts
    halo = POOL_HALO
    halo_per_tile = ts // halo
    last_halo = t // halo - 1

    def tile_row(b, i):
        return b * nt + nt - 1 - i

    def rows(width):
        return pl.BlockSpec((ts, width), lambda b, i: (tile_row(b, i), 0))

    p_prev = pl.BlockSpec((halo, POOL_W), lambda b, i: (jnp.maximum(tile_row(b, i) * halo_per_tile - 1, 0), 0))
    p_next = pl.BlockSpec((halo, POOL_W),
                          lambda b, i: (jnp.minimum((tile_row(b, i) + 1) * halo_per_tile, last_halo), 0))
    state = pl.BlockSpec((ts // c,) + states_f.shape[1:], lambda b, i: (tile_row(b, i), 0, 0, 0))
    kv_spec = pl.BlockSpec((1,) + kv.shape[2:], lambda b, i: (layer * batch + b, 0, 0))
    dmat_spec = pl.BlockSpec((1, RET_HEADS, c, c), lambda b, i: (layer, 0, 0, 0), pipeline_mode=pl.Buffered(1))
    table_spec = pl.BlockSpec((1, c, RET_QK_W), lambda b, i: (layer, 0, 0), pipeline_mode=pl.Buffered(1))
    dec_spec = pl.BlockSpec((1, 2, 1, RET_QK_W), lambda b, i: (layer, 0, 0, 0), pipeline_mode=pl.Buffered(1))
    return pl.pallas_call(
        _mix_kernel,
        grid=(batch, nt),
        in_specs=[rows(d), rows(d), rows(2 * RET_QK_W), rows(RET_V_W), rows(RET_V_W),
                  rows(POOL_W), p_prev, p_next, pl.BlockSpec((ts, POOL_W), lambda b, i: (nt - 1 - i, 0)),
                  rows(MEM_Q_W), state, kv_spec, dmat_spec, table_spec, table_spec, table_spec, dec_spec,
                  gate_weight(0), gate_weight(1), gate_weight(2),
                  _layer_resident(w_ret_o.shape, layer), _layer_resident(w_pool.shape, layer),
                  _layer_resident(w_mem_o.shape, layer), _layer_resident(w_out.shape, layer)],
        out_specs=rows(d),
        out_shape=jax.ShapeDtypeStruct((t, d), F32),
        scratch_shapes=[pltpu.VMEM((ts, RET_V_W), BF16), pltpu.VMEM((ts, N_BRANCHES * d), BF16),
                        pltpu.VMEM((RET_HEADS, RET_QK_DIM, RET_V_DIM), F32)],
        compiler_params=_params("arbitrary", "arbitrary"),
        name="mix",
    )(x2, half_h, qk, v, g, p, p, p, inv_count, qm, states_f, kv.reshape((-1,) + kv.shape[2:]), dmat, xi_f, xi_b,
      zeta_b, dec, w_in, w_in, w_in, w_ret_o, w_pool, w_mem_o, w_out)


def _mlp_kernel(x_ref, gain_ref, w1_ref, w2_ref, final_gain_ref, out_ref, hid_scr, *, final_norm):
    x = x_ref[...]
    h = (x * gain_ref[...]).astype(BF16)
    for col in range(0, w1_ref.shape[1], COL_CHUNK):
        hid = jnp.maximum(_dot(h, w1_ref[:, col:col + COL_CHUNK]), 0.0)
        hid_scr[:, col:col + COL_CHUNK] = (hid * hid).astype(BF16)
    acc = _dot(hid_scr[...], w2_ref[...])
    out = x + acc / (jnp.mean(x * x, axis=-1, keepdims=True) + EPS)
    out_ref[...] = _rms_norm(out, final_gain_ref[...]) if final_norm else out


def _mlp(x2, gains, w1, w2, final_gain, layer, final_norm):
    t, d = x2.shape
    tm = MLP_TILE
    rows = pl.BlockSpec((tm, d), lambda i: (i, 0))
    return pl.pallas_call(
        functools.partial(_mlp_kernel, final_norm=final_norm),
        grid=(t // tm,),
        in_specs=[rows, _layer_resident(gains.shape, layer), _layer_resident(w1.shape, layer),
                  _layer_resident(w2.shape, layer), _resident((1, d))],
        out_specs=rows,
        out_shape=jax.ShapeDtypeStruct((t, d), F32),
        scratch_shapes=[pltpu.VMEM((tm, w1.shape[2]), BF16)],
        compiler_params=_params("arbitrary"),
        name="mlp",
    )(x2, gains, w1, w2, final_gain.reshape(1, d))


def _pool_weight_kernel(w_grp_ref, scale_ref, w_o_ref, out_ref):
    for gi in range(len(POOL_WINDOWS)):
        rows = slice(gi * POOL_GROUP, (gi + 1) * POOL_GROUP)
        scaled = w_grp_ref[0, gi] * scale_ref[0, :, rows]
        out_ref[0, rows, :] = jnp.dot(scaled, w_o_ref[0, rows, :], preferred_element_type=F32,
                                      precision=lax.Precision.HIGHEST).astype(BF16)


def _pool_weights(w_pool_grp, pool_scale, w_pool_o):
    depth, groups, group_w, _ = w_pool_grp.shape
    d = w_pool_o.shape[2]
    return pl.pallas_call(
        _pool_weight_kernel,
        grid=(depth,),
        in_specs=[pl.BlockSpec((1, groups, group_w, group_w), lambda l: (l, 0, 0, 0)),
                  pl.BlockSpec((1, 1, POOL_W), lambda l: (l, 0, 0)),
                  pl.BlockSpec((1, POOL_W, d), lambda l: (l, 0, 0))],
        out_specs=pl.BlockSpec((1, POOL_W, d), lambda l: (l, 0, 0)),
        out_shape=jax.ShapeDtypeStruct((depth, POOL_W, d), BF16),
        compiler_params=_params("arbitrary"),
        name="pool_weights",
    )(w_pool_grp, pool_scale.reshape(depth, 1, POOL_W), w_pool_o)


def _rotary_tables(seq):
    inv = ROPE_BASE ** (-jnp.arange(0, RET_QK_DIM, 2, dtype=F32) / RET_QK_DIM)
    ang = jnp.arange(seq, dtype=F32)[:, None] * inv[None, :]
    cos, sin = jnp.cos(ang), jnp.sin(ang)
    return jnp.concatenate([cos, cos], axis=1), jnp.concatenate([-sin, sin], axis=1)


def _pool_inv_counts(seq):
    pos = jnp.arange(seq)
    cols = []
    for w in POOL_WINDOWS:
        count = jnp.minimum(pos + w // 2, seq) - jnp.maximum(pos - w // 2, 0)
        cols.append(jnp.broadcast_to((1.0 / count.astype(F32))[:, None], (seq, POOL_GROUP)))
    return jnp.concatenate(cols, axis=1)


def kernel(x, mem, w_in, ret_decay_logit, w_ret_o, w_pool_grp, pool_scale, w_pool_o, w_mem_kv, w_mem_o,
           w_out, w_ff1, w_ff2, norm1_g, norm2_g, mem_norm_g, final_norm_g):
    batch, seq, d = x.shape
    depth = w_in.shape[0]
    assert seq % MIX_TILE == 0 and seq % ROW_TILE == 0 and MIX_TILE % RET_CHUNK == 0
    assert d % COL_CHUNK == 0 and POOL_HALO % 8 == 0

    cos, sin = _rotary_tables(seq)
    inv_count = _pool_inv_counts(seq)
    dmat, xi_f, xi_b, zeta_f, zeta_b, dec = _decay_tables(ret_decay_logit)
    kv = _mem_kv(mem, mem_norm_g, w_mem_kv.astype(BF16))

    w_pool = _pool_weights(w_pool_grp, pool_scale, w_pool_o)
    w_in_b = w_in.astype(BF16)
    w_ret_o_b = w_ret_o.astype(BF16)
    w_mem_o_b = w_mem_o.astype(BF16)
    w_out_b = w_out.astype(BF16)
    w_ff1_b = w_ff1.astype(BF16)
    w_ff2_b = w_ff2.astype(BF16)
    gains1 = norm1_g.reshape(depth, 1, d)
    gains2 = norm2_g.reshape(depth, 1, d)

    x2 = x.reshape(batch * seq, d)
    for l in range(depth):
        qk, v, g, p, qm, states_f, half_h = _in_proj(x2, gains1, w_in_b, cos, sin, zeta_f, dec, l, seq)
        x2 = _mix(x2, half_h, qk, v, g, p, inv_count, qm, states_f, kv, dmat, xi_f, xi_b, zeta_b, dec, w_in_b,
                  w_ret_o_b, w_pool, w_mem_o_b, w_out_b, l, batch, seq)
        x2 = _mlp(x2, gains2, w_ff1_b, w_ff2_b, final_norm_g, l, l == depth - 1)
    return x2.reshape(batch, seq, d)
```

```python
import functools

import jax
import jax.numpy as jnp
import numpy as np
from jax import lax
from jax.experimental import pallas as pl
from jax.experimental.pallas import tpu as pltpu

F32 = jnp.float32
BF16 = jnp.bfloat16

RET_HEADS = 4
RET_QK_DIM = 128
RET_V_DIM = 256
MEM_HEADS = 4
MEM_HEAD_DIM = 128
POOL_WINDOWS = (2, 4, 8, 16)
POOL_GROUP = 128
N_BRANCHES = 3
ROPE_BASE = 10000.0
EPS = 1e-6
LOG2_E = 1.4426950408889634

RET_QK_W = RET_HEADS * RET_QK_DIM
RET_V_W = RET_HEADS * RET_V_DIM
MEM_Q_W = MEM_HEADS * MEM_HEAD_DIM
POOL_W = POOL_GROUP * len(POOL_WINDOWS)
POOL_HALO = max(POOL_WINDOWS) // 2
GATE_COL0 = 2 * RET_QK_W + 2 * RET_V_W + POOL_W + MEM_Q_W

V7X_VMEM_BYTES = 64 * 1024 * 1024
VMEM_LIMIT_BYTES = V7X_VMEM_BYTES - 4 * 1024 * 1024

RET_CHUNK = 256
ROW_TILE = 1024
MLP_TILE = 1024
MIX_TILE = 512
COL_CHUNK = 512
IN_PROJ_CHUNK = 256
MIX_FILL_CHUNK = 256


def _params(*semantics):
    return pltpu.CompilerParams(dimension_semantics=semantics, vmem_limit_bytes=VMEM_LIMIT_BYTES)


def _resident(shape):
    zeros = (0,) * len(shape)
    return pl.BlockSpec(shape, lambda *_: zeros, pipeline_mode=pl.Buffered(1))


def _layer_resident(stacked_shape, layer):
    index = (layer,) + (0,) * (len(stacked_shape) - 1)
    return pl.BlockSpec((None,) + tuple(stacked_shape[1:]), lambda *_: index, pipeline_mode=pl.Buffered(1))


def _rms_norm(x, gain):
    return x * lax.rsqrt(jnp.mean(x * x, axis=-1, keepdims=True) + EPS) * gain


def _swish(x):
    half = 0.5 * x
    return half + half * jnp.tanh(half)


_dot = functools.partial(jnp.dot, preferred_element_type=F32)
_dot_nt = functools.partial(lax.dot_general, dimension_numbers=(((1,), (1,)), ((), ())), preferred_element_type=F32)
_dot_tn = functools.partial(lax.dot_general, dimension_numbers=(((0,), (0,)), ((), ())), preferred_element_type=F32)


def _decay_tables_kernel(logit_ref, dmat_ref, xi_f_ref, xi_b_ref, zeta_f_ref, zeta_b_ref, dec_ref):
    c = RET_CHUNK
    logit = logit_ref[0]
    log_g = jnp.minimum(logit, 0.0) - jnp.log1p(jnp.exp(-jnp.abs(logit)))
    lg_f, lg_b = log_g[0], log_g[1]
    row = lax.broadcasted_iota(jnp.int32, (c, RET_QK_W), 0).astype(F32)
    xi_f_ref[0] = jnp.exp(lg_f * (row + 1.0)).astype(BF16)
    xi_b_ref[0] = jnp.exp(lg_b * (c - row)).astype(BF16)
    zeta_f_ref[0] = jnp.exp(lg_f * (c - 1.0 - row)).astype(BF16)
    zeta_b_ref[0] = jnp.exp(lg_b * row).astype(BF16)
    dec_ref[0, 0] = jnp.exp(lg_f * c)
    dec_ref[0, 1] = jnp.exp(lg_b * c)
    i = lax.broadcasted_iota(jnp.int32, (c, c), 0)
    j = lax.broadcasted_iota(jnp.int32, (c, c), 1)
    diff = (i - j).astype(F32)
    for h in range(RET_HEADS):
        lf = lg_f[:, h * RET_QK_DIM:h * RET_QK_DIM + 1]
        lb = lg_b[:, h * RET_QK_DIM:h * RET_QK_DIM + 1]
        fwd = jnp.exp(lf * jnp.maximum(diff, 0.0))
        bwd = jnp.exp(lb * jnp.maximum(-diff, 0.0))
        dmat_ref[0, h] = jnp.where(diff >= 0.0, fwd, bwd)


def _decay_tables(ret_decay_logit):
    depth = ret_decay_logit.shape[0]
    c = RET_CHUNK
    logit = jnp.repeat(ret_decay_logit.astype(F32), RET_QK_DIM, axis=-1)[:, :, None, :]
    vec = jax.ShapeDtypeStruct((depth, c, RET_QK_W), BF16)
    vec_spec = pl.BlockSpec((1, c, RET_QK_W), lambda l: (l, 0, 0))
    return pl.pallas_call(
        _decay_tables_kernel,
        grid=(depth,),
        in_specs=[pl.BlockSpec((1, 2, 1, RET_QK_W), lambda l: (l, 0, 0, 0))],
        out_specs=[pl.BlockSpec((1, RET_HEADS, c, c), lambda l: (l, 0, 0, 0)),
                   vec_spec, vec_spec, vec_spec, vec_spec,
                   pl.BlockSpec((1, 2, 1, RET_QK_W), lambda l: (l, 0, 0, 0))],
        out_shape=[jax.ShapeDtypeStruct((depth, RET_HEADS, c, c), F32), vec, vec, vec, vec,
                   jax.ShapeDtypeStruct((depth, 2, 1, RET_QK_W), F32)],
        compiler_params=_params("arbitrary"),
        name="decay_tables",
    )(logit)


def _mem_kv_kernel(mem_ref, gain_ref, w_ref, kv_ref):
    mem_n = _rms_norm(mem_ref[...], gain_ref[...]).astype(BF16)
    for l in range(w_ref.shape[0]):
        kv_ref[l] = _dot(mem_n, w_ref[l]).astype(BF16)


def _mem_kv(mem, mem_norm_g, w_mem_kv):
    b, m, d = mem.shape
    depth, _, kvw = w_mem_kv.shape
    rows = b * m
    tm = min(MLP_TILE, rows)
    assert rows % tm == 0
    kv = pl.pallas_call(
        _mem_kv_kernel,
        grid=(rows // tm,),
        in_specs=[pl.BlockSpec((tm, d), lambda i: (i, 0)),
                  _resident((1, d)),
                  _resident((depth, d, kvw))],
        out_specs=pl.BlockSpec((depth, tm, kvw), lambda i: (0, i, 0)),
        out_shape=jax.ShapeDtypeStruct((depth, rows, kvw), BF16),
        compiler_params=_params("arbitrary"),
        name="mem_kv",
    )(mem.reshape(rows, d), mem_norm_g.reshape(1, d), w_mem_kv)
    return kv.reshape(depth, b, m, kvw)


def _chunk_outer(k, v_ref, rows, zeta):
    kz = k * zeta
    return [_dot_tn(kz[:, h * RET_QK_DIM:(h + 1) * RET_QK_DIM], v_ref[rows, h * RET_V_DIM:(h + 1) * RET_V_DIM])
            for h in range(RET_HEADS)]


def _decay_and_add(acc, outer, dec):
    for h in range(RET_HEADS):
        acc[h] = acc[h] * dec[:, h * RET_QK_DIM:h * RET_QK_DIM + 1] + outer[h]


def _in_proj_kernel(x_ref, gain_ref, w_ref, cos_ref, sin_ref, zeta_f_ref, dec_ref,
                    qk_ref, v_ref, g_ref, p_ref, qm_ref, state_f_ref, half_h_ref, acc, outer_scr, *, tiles_per_seq):
    @pl.when(pl.program_id(0) % tiles_per_seq == 0)
    def _():
        acc[...] = jnp.zeros_like(acc)

    h = _rms_norm(x_ref[...], gain_ref[...]).astype(BF16)
    half_h_ref[...] = h * 0.5
    cos = cos_ref[...]
    sin = sin_ref[...]

    def rotary(a, scale):
        heads = []
        for hd in range(a.shape[1] // RET_QK_DIM):
            ah = a[:, hd * RET_QK_DIM:(hd + 1) * RET_QK_DIM]
            heads.append(ah * cos + pltpu.roll(ah, RET_QK_DIM // 2, axis=1) * sin)
        rotated = jnp.concatenate(heads, axis=1)
        return rotated if scale is None else rotated * scale

    groups = [
        (qk_ref, 0, RET_QK_W, lambda a: rotary(a, None)),
        (qk_ref, RET_QK_W, RET_QK_W, lambda a: rotary(a, RET_QK_DIM ** -0.5)),
        (v_ref, 0, RET_V_W, lambda a: a),
        (g_ref, 0, RET_V_W, _swish),
        (p_ref, 0, POOL_W, lambda a: a),
        (qm_ref, 0, MEM_Q_W, lambda a: a),
    ]
    chunk = RET_CHUNK
    n_chunks = x_ref.shape[0] // chunk
    outer_jobs = list(range(n_chunks))

    def outer_job():
        ci = outer_jobs.pop(0)
        rows = slice(ci * chunk, (ci + 1) * chunk)
        for hd, outer in enumerate(_chunk_outer(qk_ref[rows, RET_QK_W:], v_ref, rows, zeta_f_ref[0])):
            outer_scr[ci, hd] = outer

    def scan():
        for ci in range(n_chunks):
            for hd in range(RET_HEADS):
                state_f_ref[ci, hd] = acc[hd].astype(BF16)
            _decay_and_add(acc, outer_scr[ci], dec_ref[0, 0])

    w_col = 0
    scanned = False
    for out_ref, out_col, width, epilogue in groups:
        for c in range(0, width, IN_PROJ_CHUNK):
            a = _dot(h, w_ref[:, w_col + c:w_col + c + IN_PROJ_CHUNK])
            out_ref[:, out_col + c:out_col + c + IN_PROJ_CHUNK] = epilogue(a).astype(out_ref.dtype)
            if out_ref is not qk_ref and out_ref is not v_ref:
                if outer_jobs:
                    outer_job()
                elif not scanned:
                    scan()
                    scanned = True
        w_col += width
    assert scanned and not outer_jobs


def _in_proj(x2, gains, w_in, cos, sin, zeta_f, dec, layer, seq):
    t, d = x2.shape
    tm = ROW_TILE
    c = RET_CHUNK
    pos_tiles = seq // tm
    w_cols = GATE_COL0

    def rows(width):
        return pl.BlockSpec((tm, width), lambda i: (i, 0))

    pos_spec = pl.BlockSpec((tm, RET_QK_DIM), lambda i: (i % pos_tiles, 0))
    w_spec = pl.BlockSpec((None, d, w_cols), lambda i: (layer, 0, 0), pipeline_mode=pl.Buffered(1))
    state_shape = (RET_HEADS, RET_QK_DIM, RET_V_DIM)
    return pl.pallas_call(
        functools.partial(_in_proj_kernel, tiles_per_seq=pos_tiles),
        grid=(t // tm,),
        in_specs=[rows(d), _layer_resident(gains.shape, layer), w_spec, pos_spec, pos_spec,
                  pl.BlockSpec((1, c, RET_QK_W), lambda i: (layer, 0, 0), pipeline_mode=pl.Buffered(1)),
                  pl.BlockSpec((1, 2, 1, RET_QK_W), lambda i: (layer, 0, 0, 0), pipeline_mode=pl.Buffered(1))],
        out_specs=[rows(2 * RET_QK_W), rows(RET_V_W), rows(RET_V_W), rows(POOL_W), rows(MEM_Q_W),
                   pl.BlockSpec((tm // c,) + state_shape, lambda i: (i, 0, 0, 0)), rows(d)],
        out_shape=[jax.ShapeDtypeStruct((t, 2 * RET_QK_W), BF16),
                   jax.ShapeDtypeStruct((t, RET_V_W), BF16),
                   jax.ShapeDtypeStruct((t, RET_V_W), BF16),
                   jax.ShapeDtypeStruct((t, POOL_W), F32),
                   jax.ShapeDtypeStruct((t, MEM_Q_W), BF16),
                   jax.ShapeDtypeStruct((t // c,) + state_shape, BF16),
                   jax.ShapeDtypeStruct((t, d), BF16)],
        scratch_shapes=[pltpu.VMEM(state_shape, F32), pltpu.VMEM((tm // c,) + state_shape, F32)],
        compiler_params=_params("arbitrary"),
        name="in_proj",
    )(x2, gains, w_in, cos, sin, zeta_f, dec)


def _window_sums(padded):
    length = padded.shape[0]
    halo = POOL_HALO
    ts = length - 2 * halo

    def ahead(a, k):
        return pltpu.roll(a, length - k, axis=0)

    def behind(a, k):
        return pltpu.roll(a, k, axis=0)

    sums = []
    for gi, w in enumerate(POOL_WINDOWS):
        a = padded[:, gi * POOL_GROUP:(gi + 1) * POOL_GROUP]
        span = 1
        while 2 * span < w:
            a = a + ahead(a, span)
            span *= 2
        assert 2 * span == w and span <= halo
        sums.append((a + behind(a, span))[halo:halo + ts, :])
    return sums


def _mix_kernel(x_ref, half_h_ref, qk_ref, v_ref, g_ref, p_ref, p_prev_ref, p_next_ref, inv_count_ref, qm_ref,
                state_f_ref, kv_ref, dmat_ref, xi_f_ref, xi_b_ref, zeta_b_ref, dec_ref,
                w_gate_ret_ref, w_gate_pool_ref, w_gate_mem_ref,
                w_ret_o_ref, w_pool_ref, w_mem_o_ref, w_out_ref,
                out_ref, ret_scr, gate_scr, acc_b):
    ts = x_ref.shape[0]
    c = RET_CHUNK
    n_tiles = pl.num_programs(1)
    tile = n_tiles - 1 - pl.program_id(1)

    @pl.when(pl.program_id(1) == 0)
    def _():
        acc_b[...] = jnp.zeros_like(acc_b)

    d = x_ref.shape[1]
    fill_cols = list(range(0, d, MIX_FILL_CHUNK))

    def out_proj_chunk(lhs, w_ref, col):
        return _dot(lhs, w_ref[:, col:col + MIX_FILL_CHUNK]).astype(BF16)

    gate_w_refs = (w_gate_ret_ref, w_gate_pool_ref, w_gate_mem_ref)
    gate_jobs = [(b, col) for b in range(N_BRANCHES) for col in fill_cols]

    def gate_job():
        b, col = gate_jobs.pop(0)
        z_half = _dot(half_h_ref[...], gate_w_refs[b][:, col:col + MIX_FILL_CHUNK])
        gate_scr[:, b * d + col:b * d + col + MIX_FILL_CHUNK] = jnp.tanh(z_half).astype(BF16) * 0.5 + 0.5

    xi_f = xi_f_ref[0]
    xi_b = xi_b_ref[0]
    def normalise_and_gate(o, rows, v_cols):
        mu = jnp.mean(o, axis=-1, keepdims=True)
        cen = o - mu
        var = jnp.mean(cen * cen, axis=-1, keepdims=True)
        o_n = cen * lax.rsqrt(var + EPS)
        ret_scr[rows, v_cols] = o_n.astype(BF16) * g_ref[rows, v_cols]

    pending = None
    for ci in reversed(range(ts // c)):
        rows = slice(ci * c, (ci + 1) * c)
        q = qk_ref[rows, :RET_QK_W]
        q_f = q * xi_f
        q_b = q * xi_b
        for h in range(RET_HEADS):
            qk_cols = slice(h * RET_QK_DIM, (h + 1) * RET_QK_DIM)
            v_cols = slice(h * RET_V_DIM, (h + 1) * RET_V_DIM)
            k_h = qk_ref[rows, RET_QK_W + h * RET_QK_DIM:RET_QK_W + (h + 1) * RET_QK_DIM]
            s = _dot_nt(q[:, qk_cols], k_h) * dmat_ref[0, h]
            q_fb = jnp.concatenate([q_f[:, qk_cols], q_b[:, qk_cols]], axis=1)
            state = jnp.concatenate([state_f_ref[ci, h], acc_b[h].astype(BF16)], axis=0)
            o = _dot(s.astype(BF16), v_ref[rows, v_cols]) + _dot(q_fb, state)
            if pending is not None:
                normalise_and_gate(*pending)
                if gate_jobs:
                    gate_job()
            pending = (o, rows, v_cols)
        _decay_and_add(acc_b, _chunk_outer(qk_ref[rows, RET_QK_W:], v_ref, rows, zeta_b_ref[0]), dec_ref[0, 1])
    normalise_and_gate(*pending)
    if gate_jobs:
        gate_job()

    p = p_ref[...]
    padded = jnp.concatenate([jnp.where(tile > 0, p_prev_ref[...], 0.0), p,
                              jnp.where(tile < n_tiles - 1, p_next_ref[...], 0.0)], axis=0)
    ret_lhs = ret_scr[...]
    groups, o_ret = [], []
    for gi, win in enumerate(_window_sums(padded)):
        cols = slice(gi * POOL_GROUP, (gi + 1) * POOL_GROUP)
        groups.append((win * inv_count_ref[:, cols] - p[:, cols]).astype(BF16))
        if gi < len(fill_cols):
            o_ret.append(out_proj_chunk(ret_lhs, w_ret_o_ref, fill_cols[gi]))
    o_ret += [out_proj_chunk(ret_lhs, w_ret_o_ref, col) for col in fill_cols[len(o_ret):]]
    pool_lhs = jnp.concatenate(groups, axis=1)

    exp2_scale = (MEM_HEAD_DIM ** -0.5) * LOG2_E
    heads, o_pool = [], []

    def scores(h):
        cols = slice(h * MEM_HEAD_DIM, (h + 1) * MEM_HEAD_DIM)
        return _dot_nt(qm_ref[:, cols], kv_ref[0, :, cols])

    def attend(h, s):
        v_h = kv_ref[0, :, MEM_Q_W + h * MEM_HEAD_DIM:MEM_Q_W + (h + 1) * MEM_HEAD_DIM]
        e = jnp.exp2((s - jnp.max(s, axis=-1, keepdims=True)) * exp2_scale)
        o = _dot(e.astype(BF16), v_h) / jnp.sum(e, axis=-1, keepdims=True)
        heads.append(o.astype(BF16))
        if gate_jobs:
            gate_job()
        if h < len(fill_cols):
            o_pool.append(out_proj_chunk(pool_lhs, w_pool_ref, fill_cols[h]))

    s_next = scores(0)
    for h in range(MEM_HEADS):
        s_cur = s_next
        if h + 1 < MEM_HEADS:
            s_next = scores(h + 1)
        attend(h, s_cur)
    while gate_jobs:
        gate_job()
    o_pool += [out_proj_chunk(pool_lhs, w_pool_ref, col) for col in fill_cols[len(o_pool):]]
    mem_lhs = jnp.concatenate(heads, axis=1)

    merged = []
    for j, col in enumerate(fill_cols):
        o_mem = out_proj_chunk(mem_lhs, w_mem_o_ref, col)
        gates = [gate_scr[:, b * d + col:b * d + col + MIX_FILL_CHUNK] for b in range(N_BRANCHES)]
        merged.append(gates[0] * o_ret[j] + gates[1] * o_pool[j] + gates[2] * o_mem)
    out_ref[...] = x_ref[...] + _dot(jnp.concatenate(merged, axis=1), w_out_ref[...])


def _mix(x2, half_h, qk, v, g, p, inv_count, qm, states_f, kv, dmat, xi_f, xi_b, zeta_b, dec, w_in, w_ret_o, w_pool,
         w_mem_o, w_out, layer, batch, seq):
    t, d = x2.shape
    assert GATE_COL0 % d == 0

    def gate_weight(branch):
        return pl.BlockSpec((None, d, d), lambda b, i: (layer, 0, GATE_COL0 // d + branch),
                            pipeline_mode=pl.Buffered(1))

    ts = MIX_TILE
    c = RET_CHUNK
    nt = seq // ts
    halo = POOL_HALO
    halo_per_tile = ts // halo
    last_halo = t // halo - 1

    def tile_row(b, i):
        return b * nt + nt - 1 - i

    def rows(width):
        return pl.BlockSpec((ts, width), lambda b, i: (tile_row(b, i), 0))

    p_prev = pl.BlockSpec((halo, POOL_W), lambda b, i: (jnp.maximum(tile_row(b, i) * halo_per_tile - 1, 0), 0))
    p_next = pl.BlockSpec((halo, POOL_W),
                          lambda b, i: (jnp.minimum((tile_row(b, i) + 1) * halo_per_tile, last_halo), 0))
    state = pl.BlockSpec((ts // c,) + states_f.shape[1:], lambda b, i: (tile_row(b, i), 0, 0, 0))
    kv_spec = pl.BlockSpec((1,) + kv.shape[2:], lambda b, i: (layer * batch + b, 0, 0))
    dmat_spec = pl.BlockSpec((1, RET_HEADS, c, c), lambda b, i: (layer, 0, 0, 0), pipeline_mode=pl.Buffered(1))
    table_spec = pl.BlockSpec((1, c, RET_QK_W), lambda b, i: (layer, 0, 0), pipeline_mode=pl.Buffered(1))
    dec_spec = pl.BlockSpec((1, 2, 1, RET_QK_W), lambda b, i: (layer, 0, 0, 0), pipeline_mode=pl.Buffered(1))
    return pl.pallas_call(
        _mix_kernel,
        grid=(batch, nt),
        in_specs=[rows(d), rows(d), rows(2 * RET_QK_W), rows(RET_V_W), rows(RET_V_W),
                  rows(POOL_W), p_prev, p_next, pl.BlockSpec((ts, POOL_W), lambda b, i: (nt - 1 - i, 0)),
                  rows(MEM_Q_W), state, kv_spec, dmat_spec, table_spec, table_spec, table_spec, dec_spec,
                  gate_weight(0), gate_weight(1), gate_weight(2),
                  _layer_resident(w_ret_o.shape, layer), _layer_resident(w_pool.shape, layer),
                  _layer_resident(w_mem_o.shape, layer), _layer_resident(w_out.shape, layer)],
        out_specs=rows(d),
        out_shape=jax.ShapeDtypeStruct((t, d), F32),
        scratch_shapes=[pltpu.VMEM((ts, RET_V_W), BF16), pltpu.VMEM((ts, N_BRANCHES * d), BF16),
                        pltpu.VMEM((RET_HEADS, RET_QK_DIM, RET_V_DIM), F32)],
        compiler_params=_params("arbitrary", "arbitrary"),
        name="mix",
    )(x2, half_h, qk, v, g, p, p, p, inv_count, qm, states_f, kv.reshape((-1,) + kv.shape[2:]), dmat, xi_f, xi_b,
      zeta_b, dec, w_in, w_in, w_in, w_ret_o, w_pool, w_mem_o, w_out)


def _mlp_kernel(x_ref, gain_ref, w1_ref, w2_ref, final_gain_ref, out_ref, hid_scr, *, final_norm):
    x = x_ref[...]
    h = (x * gain_ref[...]).astype(BF16)
    for col in range(0, w1_ref.shape[1], COL_CHUNK):
        hid = jnp.maximum(_dot(h, w1_ref[:, col:col + COL_CHUNK]), 0.0)
        hid_scr[:, col:col + COL_CHUNK] = (hid * hid).astype(BF16)
    acc = _dot(hid_scr[...], w2_ref[...])
    out = x + acc / (jnp.mean(x * x, axis=-1, keepdims=True) + EPS)
    out_ref[...] = _rms_norm(out, final_gain_ref[...]) if final_norm else out


def _mlp(x2, gains, w1, w2, final_gain, layer, final_norm):
    t, d = x2.shape
    tm = MLP_TILE
    rows = pl.BlockSpec((tm, d), lambda i: (i, 0))
    return pl.pallas_call(
        functools.partial(_mlp_kernel, final_norm=final_norm),
        grid=(t // tm,),
        in_specs=[rows, _layer_resident(gains.shape, layer), _layer_resident(w1.shape, layer),
                  _layer_resident(w2.shape, layer), _resident((1, d))],
        out_specs=rows,
        out_shape=jax.ShapeDtypeStruct((t, d), F32),
        scratch_shapes=[pltpu.VMEM((tm, w1.shape[2]), BF16)],
        compiler_params=_params("arbitrary"),
        name="mlp",
    )(x2, gains, w1, w2, final_gain.reshape(1, d))


def _pool_weight_kernel(w_grp_ref, scale_ref, w_o_ref, out_ref):
    for gi in range(len(POOL_WINDOWS)):
        rows = slice(gi * POOL_GROUP, (gi + 1) * POOL_GROUP)
        scaled = w_grp_ref[0, gi] * scale_ref[0, :, rows]
        out_ref[0, rows, :] = jnp.dot(scaled, w_o_ref[0, rows, :], preferred_element_type=F32,
                                      precision=lax.Precision.HIGHEST).astype(BF16)


def _pool_weights(w_pool_grp, pool_scale, w_pool_o):
    depth, groups, group_w, _ = w_pool_grp.shape
    d = w_pool_o.shape[2]
    return pl.pallas_call(
        _pool_weight_kernel,
        grid=(depth,),
        in_specs=[pl.BlockSpec((1, groups, group_w, group_w), lambda l: (l, 0, 0, 0)),
                  pl.BlockSpec((1, 1, POOL_W), lambda l: (l, 0, 0)),
                  pl.BlockSpec((1, POOL_W, d), lambda l: (l, 0, 0))],
        out_specs=pl.BlockSpec((1, POOL_W, d), lambda l: (l, 0, 0)),
        out_shape=jax.ShapeDtypeStruct((depth, POOL_W, d), BF16),
        compiler_params=_params("arbitrary"),
        name="pool_weights",
    )(w_pool_grp, pool_scale.reshape(depth, 1, POOL_W), w_pool_o)


def _rotary_tables(seq):
    inv = ROPE_BASE ** (-np.arange(0, RET_QK_DIM, 2, dtype=np.float64) / RET_QK_DIM)
    ang = np.arange(seq, dtype=np.float64)[:, None] * inv[None, :]
    cos, sin = np.cos(ang), np.sin(ang)
    return (jnp.asarray(np.concatenate([cos, cos], axis=1), F32),
            jnp.asarray(np.concatenate([-sin, sin], axis=1), F32))


def _pool_inv_counts(seq):
    pos = np.arange(seq)
    cols = []
    for w in POOL_WINDOWS:
        count = np.minimum(pos + w // 2, seq) - np.maximum(pos - w // 2, 0)
        cols.append(np.broadcast_to((1.0 / count)[:, None], (seq, POOL_GROUP)))
    return jnp.asarray(np.concatenate(cols, axis=1), F32)


def kernel(x, mem, w_in, ret_decay_logit, w_ret_o, w_pool_grp, pool_scale, w_pool_o, w_mem_kv, w_mem_o,
           w_out, w_ff1, w_ff2, norm1_g, norm2_g, mem_norm_g, final_norm_g):
    batch, seq, d = x.shape
    depth = w_in.shape[0]
    assert seq % MIX_TILE == 0 and seq % ROW_TILE == 0 and MIX_TILE % RET_CHUNK == 0
    assert d % COL_CHUNK == 0 and POOL_HALO % 8 == 0

    cos, sin = _rotary_tables(seq)
    inv_count = _pool_inv_counts(seq)
    dmat, xi_f, xi_b, zeta_f, zeta_b, dec = _decay_tables(ret_decay_logit)
    kv = _mem_kv(mem, mem_norm_g, w_mem_kv.astype(BF16))

    w_pool = _pool_weights(w_pool_grp, pool_scale, w_pool_o)
    w_in_b = w_in.astype(BF16)
    w_ret_o_b = w_ret_o.astype(BF16)
    w_mem_o_b = w_mem_o.astype(BF16)
    w_out_b = w_out.astype(BF16)
    w_ff1_b = w_ff1.astype(BF16)
    w_ff2_b = w_ff2.astype(BF16)
    gains1 = norm1_g.reshape(depth, 1, d)
    gains2 = norm2_g.reshape(depth, 1, d)

    x2 = x.reshape(batch * seq, d)
    for l in range(depth):
        qk, v, g, p, qm, states_f, half_h = _in_proj(x2, gains1, w_in_b, cos, sin, zeta_f, dec, l, seq)
        x2 = _mix(x2, half_h, qk, v, g, p, inv_count, qm, states_f, kv, dmat, xi_f, xi_b, zeta_b, dec, w_in_b,
                  w_ret_o_b, w_pool, w_mem_o_b, w_out_b, l, batch, seq)
        x2 = _mlp(x2, gains2, w_ff1_b, w_ff2_b, final_norm_g, l, l == depth - 1)
    return x2.reshape(batch, seq, d)
```

```python
import functools

import jax
import jax.numpy as jnp
import numpy as np
from jax import lax
from jax.experimental import pallas as pl
from jax.experimental.pallas import tpu as pltpu

F32 = jnp.float32
BF16 = jnp.bfloat16

RET_HEADS = 4
RET_QK_DIM = 128
RET_V_DIM = 256
MEM_HEADS = 4
MEM_HEAD_DIM = 128
POOL_WINDOWS = (2, 4, 8, 16)
POOL_GROUP = 128
N_BRANCHES = 3
ROPE_BASE = 10000.0
EPS = 1e-6
LOG2_E = 1.4426950408889634

RET_QK_W = RET_HEADS * RET_QK_DIM
RET_V_W = RET_HEADS * RET_V_DIM
MEM_Q_W = MEM_HEADS * MEM_HEAD_DIM
POOL_W = POOL_GROUP * len(POOL_WINDOWS)
POOL_HALO = max(POOL_WINDOWS) // 2
GATE_COL0 = 2 * RET_QK_W + 2 * RET_V_W + POOL_W + MEM_Q_W

V7X_VMEM_BYTES = 64 * 1024 * 1024
VMEM_LIMIT_BYTES = V7X_VMEM_BYTES - 4 * 1024 * 1024

RET_CHUNK = 256
ROW_TILE = 1024
MLP_TILE = 1024
MIX_TILE = 512
COL_CHUNK = 512
IN_PROJ_CHUNK = 256
IN_PROJ_ROWS = 512
MLP_ROWS = 512
MIX_FILL_CHUNK = 256


def _params(*semantics):
    return pltpu.CompilerParams(dimension_semantics=semantics, vmem_limit_bytes=VMEM_LIMIT_BYTES)


def _resident(shape):
    zeros = (0,) * len(shape)
    return pl.BlockSpec(shape, lambda *_: zeros, pipeline_mode=pl.Buffered(1))


def _layer_resident(stacked_shape, layer):
    index = (layer,) + (0,) * (len(stacked_shape) - 1)
    return pl.BlockSpec((None,) + tuple(stacked_shape[1:]), lambda *_: index, pipeline_mode=pl.Buffered(1))


def _rms_norm(x, gain):
    return x * lax.rsqrt(jnp.mean(x * x, axis=-1, keepdims=True) + EPS) * gain


def _swish(x):
    half = 0.5 * x
    return half + half * jnp.tanh(half)


_dot = functools.partial(jnp.dot, preferred_element_type=F32)
_dot_nt = functools.partial(lax.dot_general, dimension_numbers=(((1,), (1,)), ((), ())), preferred_element_type=F32)
_dot_tn = functools.partial(lax.dot_general, dimension_numbers=(((0,), (0,)), ((), ())), preferred_element_type=F32)


def _decay_tables_kernel(logit_ref, dmat_ref, xi_f_ref, xi_b_ref, zeta_f_ref, zeta_b_ref, dec_ref):
    c = RET_CHUNK
    logit = logit_ref[0]
    log_g = jnp.minimum(logit, 0.0) - jnp.log1p(jnp.exp(-jnp.abs(logit)))
    lg_f, lg_b = log_g[0], log_g[1]
    row = lax.broadcasted_iota(jnp.int32, (c, RET_QK_W), 0).astype(F32)
    xi_f_ref[0] = jnp.exp(lg_f * (row + 1.0)).astype(BF16)
    xi_b_ref[0] = jnp.exp(lg_b * (c - row)).astype(BF16)
    zeta_f_ref[0] = jnp.exp(lg_f * (c - 1.0 - row)).astype(BF16)
    zeta_b_ref[0] = jnp.exp(lg_b * row).astype(BF16)
    dec_ref[0, 0] = jnp.exp(lg_f * c)
    dec_ref[0, 1] = jnp.exp(lg_b * c)
    i = lax.broadcasted_iota(jnp.int32, (c, c), 0)
    j = lax.broadcasted_iota(jnp.int32, (c, c), 1)
    diff = (i - j).astype(F32)
    for h in range(RET_HEADS):
        lf = lg_f[:, h * RET_QK_DIM:h * RET_QK_DIM + 1]
        lb = lg_b[:, h * RET_QK_DIM:h * RET_QK_DIM + 1]
        fwd = jnp.exp(lf * jnp.maximum(diff, 0.0))
        bwd = jnp.exp(lb * jnp.maximum(-diff, 0.0))
        dmat_ref[0, h] = jnp.where(diff >= 0.0, fwd, bwd)


def _decay_tables(ret_decay_logit):
    depth = ret_decay_logit.shape[0]
    c = RET_CHUNK
    logit = jnp.repeat(ret_decay_logit.astype(F32), RET_QK_DIM, axis=-1)[:, :, None, :]
    vec = jax.ShapeDtypeStruct((depth, c, RET_QK_W), BF16)
    vec_spec = pl.BlockSpec((1, c, RET_QK_W), lambda l: (l, 0, 0))
    return pl.pallas_call(
        _decay_tables_kernel,
        grid=(depth,),
        in_specs=[pl.BlockSpec((1, 2, 1, RET_QK_W), lambda l: (l, 0, 0, 0))],
        out_specs=[pl.BlockSpec((1, RET_HEADS, c, c), lambda l: (l, 0, 0, 0)),
                   vec_spec, vec_spec, vec_spec, vec_spec,
                   pl.BlockSpec((1, 2, 1, RET_QK_W), lambda l: (l, 0, 0, 0))],
        out_shape=[jax.ShapeDtypeStruct((depth, RET_HEADS, c, c), F32), vec, vec, vec, vec,
                   jax.ShapeDtypeStruct((depth, 2, 1, RET_QK_W), F32)],
        compiler_params=_params("arbitrary"),
        name="decay_tables",
    )(logit)


def _mem_kv_kernel(mem_ref, gain_ref, w_ref, kv_ref):
    mem_n = _rms_norm(mem_ref[...], gain_ref[...]).astype(BF16)
    for l in range(w_ref.shape[0]):
        kv_ref[l] = _dot(mem_n, w_ref[l]).astype(BF16)


def _mem_kv(mem, mem_norm_g, w_mem_kv):
    b, m, d = mem.shape
    depth, _, kvw = w_mem_kv.shape
    rows = b * m
    tm = min(MLP_TILE, rows)
    assert rows % tm == 0
    kv = pl.pallas_call(
        _mem_kv_kernel,
        grid=(rows // tm,),
        in_specs=[pl.BlockSpec((tm, d), lambda i: (i, 0)),
                  _resident((1, d)),
                  _resident((depth, d, kvw))],
        out_specs=pl.BlockSpec((depth, tm, kvw), lambda i: (0, i, 0)),
        out_shape=jax.ShapeDtypeStruct((depth, rows, kvw), BF16),
        compiler_params=_params("arbitrary"),
        name="mem_kv",
    )(mem.reshape(rows, d), mem_norm_g.reshape(1, d), w_mem_kv)
    return kv.reshape(depth, b, m, kvw)


def _chunk_outer(k, v_ref, rows, zeta):
    kz = k * zeta
    return [_dot_tn(kz[:, h * RET_QK_DIM:(h + 1) * RET_QK_DIM], v_ref[rows, h * RET_V_DIM:(h + 1) * RET_V_DIM])
            for h in range(RET_HEADS)]


def _decay_and_add(acc, outer, dec):
    for h in range(RET_HEADS):
        acc[h] = acc[h] * dec[:, h * RET_QK_DIM:h * RET_QK_DIM + 1] + outer[h]


def _in_proj_kernel(x_ref, gain_ref, w_ref, cos_ref, sin_ref, zeta_f_ref, dec_ref,
                    qk_ref, v_ref, g_ref, p_ref, qm_ref, state_f_ref, half_h_ref, acc, outer_scr, *, tiles_per_seq):
    @pl.when(pl.program_id(0) % tiles_per_seq == 0)
    def _():
        acc[...] = jnp.zeros_like(acc)

    chunk = RET_CHUNK
    tm = x_ref.shape[0]
    n_chunks = tm // chunk
    outer_jobs = list(range(n_chunks))

    def outer_job():
        ci = outer_jobs.pop(0)
        rows = slice(ci * chunk, (ci + 1) * chunk)
        for hd, outer in enumerate(_chunk_outer(qk_ref[rows, RET_QK_W:], v_ref, rows, zeta_f_ref[0])):
            outer_scr[ci, hd] = outer

    def scan():
        for ci in range(n_chunks):
            for hd in range(RET_HEADS):
                state_f_ref[ci, hd] = acc[hd].astype(BF16)
            _decay_and_add(acc, outer_scr[ci], dec_ref[0, 0])

    scanned = False
    for r0 in range(0, tm, IN_PROJ_ROWS):
        rs = slice(r0, r0 + IN_PROJ_ROWS)
        ready_chunks = (r0 + IN_PROJ_ROWS) // chunk
        h = _rms_norm(x_ref[rs, :], gain_ref[...]).astype(BF16)
        half_h_ref[rs, :] = h * 0.5
        cos = cos_ref[rs, :]
        sin = sin_ref[rs, :]

        def rotary(a, scale, cos=cos, sin=sin):
            heads = []
            for hd in range(a.shape[1] // RET_QK_DIM):
                ah = a[:, hd * RET_QK_DIM:(hd + 1) * RET_QK_DIM]
                heads.append(ah * cos + pltpu.roll(ah, RET_QK_DIM // 2, axis=1) * sin)
            rotated = jnp.concatenate(heads, axis=1)
            return rotated if scale is None else rotated * scale

        groups = [
            (qk_ref, 0, RET_QK_W, lambda a, rotary=rotary: rotary(a, None)),
            (qk_ref, RET_QK_W, RET_QK_W, lambda a, rotary=rotary: rotary(a, RET_QK_DIM ** -0.5)),
            (v_ref, 0, RET_V_W, lambda a: a),
            (g_ref, 0, RET_V_W, _swish),
            (p_ref, 0, POOL_W, lambda a: a),
            (qm_ref, 0, MEM_Q_W, lambda a: a),
        ]
        w_col = 0
        for out_ref, out_col, width, epilogue in groups:
            for c in range(0, width, IN_PROJ_CHUNK):
                a = _dot(h, w_ref[:, w_col + c:w_col + c + IN_PROJ_CHUNK])
                out_ref[rs, out_col + c:out_col + c + IN_PROJ_CHUNK] = epilogue(a).astype(out_ref.dtype)
                if out_ref is not qk_ref and out_ref is not v_ref:
                    if outer_jobs and outer_jobs[0] < ready_chunks:
                        outer_job()
                    elif not outer_jobs and not scanned:
                        scan()
                        scanned = True
            w_col += width
    assert scanned and not outer_jobs


def _in_proj(x2, gains, w_in, cos, sin, zeta_f, dec, layer, seq):
    t, d = x2.shape
    tm = ROW_TILE
    c = RET_CHUNK
    pos_tiles = seq // tm
    w_cols = GATE_COL0

    def rows(width):
        return pl.BlockSpec((tm, width), lambda i: (i, 0))

    pos_spec = pl.BlockSpec((tm, RET_QK_DIM), lambda i: (i % pos_tiles, 0))
    w_spec = pl.BlockSpec((None, d, w_cols), lambda i: (layer, 0, 0), pipeline_mode=pl.Buffered(1))
    state_shape = (RET_HEADS, RET_QK_DIM, RET_V_DIM)
    return pl.pallas_call(
        functools.partial(_in_proj_kernel, tiles_per_seq=pos_tiles),
        grid=(t // tm,),
        in_specs=[rows(d), _layer_resident(gains.shape, layer), w_spec, pos_spec, pos_spec,
                  pl.BlockSpec((1, c, RET_QK_W), lambda i: (layer, 0, 0), pipeline_mode=pl.Buffered(1)),
                  pl.BlockSpec((1, 2, 1, RET_QK_W), lambda i: (layer, 0, 0, 0), pipeline_mode=pl.Buffered(1))],
        out_specs=[rows(2 * RET_QK_W), rows(RET_V_W), rows(RET_V_W), rows(POOL_W), rows(MEM_Q_W),
                   pl.BlockSpec((tm // c,) + state_shape, lambda i: (i, 0, 0, 0)), rows(d)],
        out_shape=[jax.ShapeDtypeStruct((t, 2 * RET_QK_W), BF16),
                   jax.ShapeDtypeStruct((t, RET_V_W), BF16),
                   jax.ShapeDtypeStruct((t, RET_V_W), BF16),
                   jax.ShapeDtypeStruct((t, POOL_W), F32),
                   jax.ShapeDtypeStruct((t, MEM_Q_W), BF16),
                   jax.ShapeDtypeStruct((t // c,) + state_shape, BF16),
                   jax.ShapeDtypeStruct((t, d), BF16)],
        scratch_shapes=[pltpu.VMEM(state_shape, F32), pltpu.VMEM((tm // c,) + state_shape, F32)],
        compiler_params=_params("arbitrary"),
        name="in_proj",
    )(x2, gains, w_in, cos, sin, zeta_f, dec)


def _window_sums(padded):
    length = padded.shape[0]
    halo = POOL_HALO
    ts = length - 2 * halo

    def ahead(a, k):
        return pltpu.roll(a, length - k, axis=0)

    def behind(a, k):
        return pltpu.roll(a, k, axis=0)

    sums = []
    for gi, w in enumerate(POOL_WINDOWS):
        a = padded[:, gi * POOL_GROUP:(gi + 1) * POOL_GROUP]
        span = 1
        while 2 * span < w:
            a = a + ahead(a, span)
            span *= 2
        assert 2 * span == w and span <= halo
        sums.append((a + behind(a, span))[halo:halo + ts, :])
    return sums


def _mix_kernel(x_ref, half_h_ref, qk_ref, v_ref, g_ref, p_ref, p_prev_ref, p_next_ref, inv_count_ref, qm_ref,
                state_f_ref, kv_ref, dmat_ref, xi_f_ref, xi_b_ref, zeta_b_ref, dec_ref,
                w_gate_ret_ref, w_gate_pool_ref, w_gate_mem_ref,
                w_ret_o_ref, w_pool_ref, w_mem_o_ref, w_out_ref,
                out_ref, ret_scr, gate_scr, acc_b):
    ts = x_ref.shape[0]
    c = RET_CHUNK
    n_tiles = pl.num_programs(1)
    tile = n_tiles - 1 - pl.program_id(1)

    @pl.when(pl.program_id(1) == 0)
    def _():
        acc_b[...] = jnp.zeros_like(acc_b)

    d = x_ref.shape[1]
    fill_cols = list(range(0, d, MIX_FILL_CHUNK))

    def out_proj_chunk(lhs, w_ref, col):
        return _dot(lhs, w_ref[:, col:col + MIX_FILL_CHUNK]).astype(BF16)

    gate_w_refs = (w_gate_ret_ref, w_gate_pool_ref, w_gate_mem_ref)
    gate_jobs = [(b, col) for b in range(N_BRANCHES) for col in fill_cols]

    def gate_job():
        b, col = gate_jobs.pop(0)
        z_half = _dot(half_h_ref[...], gate_w_refs[b][:, col:col + MIX_FILL_CHUNK])
        gate_scr[:, b * d + col:b * d + col + MIX_FILL_CHUNK] = jnp.tanh(z_half).astype(BF16) * 0.5 + 0.5

    xi_f = xi_f_ref[0]
    xi_b = xi_b_ref[0]
    def normalise_and_gate(o, rows, v_cols):
        mu = jnp.mean(o, axis=-1, keepdims=True)
        cen = o - mu
        var = jnp.mean(cen * cen, axis=-1, keepdims=True)
        o_n = cen * lax.rsqrt(var + EPS)
        ret_scr[rows, v_cols] = o_n.astype(BF16) * g_ref[rows, v_cols]

    pending = None
    for ci in reversed(range(ts // c)):
        rows = slice(ci * c, (ci + 1) * c)
        q = qk_ref[rows, :RET_QK_W]
        q_f = q * xi_f
        q_b = q * xi_b
        for h in range(RET_HEADS):
            qk_cols = slice(h * RET_QK_DIM, (h + 1) * RET_QK_DIM)
            v_cols = slice(h * RET_V_DIM, (h + 1) * RET_V_DIM)
            k_h = qk_ref[rows, RET_QK_W + h * RET_QK_DIM:RET_QK_W + (h + 1) * RET_QK_DIM]
            s = _dot_nt(q[:, qk_cols], k_h) * dmat_ref[0, h]
            q_fb = jnp.concatenate([q_f[:, qk_cols], q_b[:, qk_cols]], axis=1)
            state = jnp.concatenate([state_f_ref[ci, h], acc_b[h].astype(BF16)], axis=0)
            o = _dot(s.astype(BF16), v_ref[rows, v_cols]) + _dot(q_fb, state)
            if pending is not None:
                normalise_and_gate(*pending)
                if gate_jobs:
                    gate_job()
            pending = (o, rows, v_cols)
        _decay_and_add(acc_b, _chunk_outer(qk_ref[rows, RET_QK_W:], v_ref, rows, zeta_b_ref[0]), dec_ref[0, 1])
    normalise_and_gate(*pending)
    if gate_jobs:
        gate_job()

    p = p_ref[...]
    padded = jnp.concatenate([jnp.where(tile > 0, p_prev_ref[...], 0.0), p,
                              jnp.where(tile < n_tiles - 1, p_next_ref[...], 0.0)], axis=0)
    ret_lhs = ret_scr[...]
    groups, o_ret = [], []
    for gi, win in enumerate(_window_sums(padded)):
        cols = slice(gi * POOL_GROUP, (gi + 1) * POOL_GROUP)
        groups.append((win * inv_count_ref[:, cols] - p[:, cols]).astype(BF16))
        if gi < len(fill_cols):
            o_ret.append(out_proj_chunk(ret_lhs, w_ret_o_ref, fill_cols[gi]))
    o_ret += [out_proj_chunk(ret_lhs, w_ret_o_ref, col) for col in fill_cols[len(o_ret):]]
    pool_lhs = jnp.concatenate(groups, axis=1)

    exp2_scale = (MEM_HEAD_DIM ** -0.5) * LOG2_E
    heads, o_pool = [], []

    def scores(h):
        cols = slice(h * MEM_HEAD_DIM, (h + 1) * MEM_HEAD_DIM)
        return _dot_nt(qm_ref[:, cols], kv_ref[0, :, cols])

    def attend(h, s):
        v_h = kv_ref[0, :, MEM_Q_W + h * MEM_HEAD_DIM:MEM_Q_W + (h + 1) * MEM_HEAD_DIM]
        e = jnp.exp2((s - jnp.max(s, axis=-1, keepdims=True)) * exp2_scale)
        o = _dot(e.astype(BF16), v_h) / jnp.sum(e, axis=-1, keepdims=True)
        heads.append(o.astype(BF16))
        if gate_jobs:
            gate_job()
        if h < len(fill_cols):
            o_pool.append(out_proj_chunk(pool_lhs, w_pool_ref, fill_cols[h]))

    s_next = scores(0)
    for h in range(MEM_HEADS):
        s_cur = s_next
        if h + 1 < MEM_HEADS:
            s_next = scores(h + 1)
        attend(h, s_cur)
    while gate_jobs:
        gate_job()
    o_pool += [out_proj_chunk(pool_lhs, w_pool_ref, col) for col in fill_cols[len(o_pool):]]
    mem_lhs = jnp.concatenate(heads, axis=1)

    merged = []
    for j, col in enumerate(fill_cols):
        o_mem = out_proj_chunk(mem_lhs, w_mem_o_ref, col)
        gates = [gate_scr[:, b * d + col:b * d + col + MIX_FILL_CHUNK] for b in range(N_BRANCHES)]
        merged.append(gates[0] * o_ret[j] + gates[1] * o_pool[j] + gates[2] * o_mem)
    out_ref[...] = x_ref[...] + _dot(jnp.concatenate(merged, axis=1), w_out_ref[...])


def _mix(x2, half_h, qk, v, g, p, inv_count, qm, states_f, kv, dmat, xi_f, xi_b, zeta_b, dec, w_in, w_ret_o, w_pool,
         w_mem_o, w_out, layer, batch, seq):
    t, d = x2.shape
    assert GATE_COL0 % d == 0

    def gate_weight(branch):
        return pl.BlockSpec((None, d, d), lambda b, i: (layer, 0, GATE_COL0 // d + branch),
                            pipeline_mode=pl.Buffered(1))

    ts = MIX_TILE
    c = RET_CHUNK
    nt = seq // ts
    halo = POOL_HALO
    halo_per_tile = ts // halo
    last_halo = t // halo - 1

    def tile_row(b, i):
        return b * nt + nt - 1 - i

    def rows(width):
        return pl.BlockSpec((ts, width), lambda b, i: (tile_row(b, i), 0))

    p_prev = pl.BlockSpec((halo, POOL_W), lambda b, i: (jnp.maximum(tile_row(b, i) * halo_per_tile - 1, 0), 0))
    p_next = pl.BlockSpec((halo, POOL_W),
                          lambda b, i: (jnp.minimum((tile_row(b, i) + 1) * halo_per_tile, last_halo), 0))
    state = pl.BlockSpec((ts // c,) + states_f.shape[1:], lambda b, i: (tile_row(b, i), 0, 0, 0))
    kv_spec = pl.BlockSpec((1,) + kv.shape[2:], lambda b, i: (layer * batch + b, 0, 0))
    dmat_spec = pl.BlockSpec((1, RET_HEADS, c, c), lambda b, i: (layer, 0, 0, 0), pipeline_mode=pl.Buffered(1))
    table_spec = pl.BlockSpec((1, c, RET_QK_W), lambda b, i: (layer, 0, 0), pipeline_mode=pl.Buffered(1))
    dec_spec = pl.BlockSpec((1, 2, 1, RET_QK_W), lambda b, i: (layer, 0, 0, 0), pipeline_mode=pl.Buffered(1))
    return pl.pallas_call(
        _mix_kernel,
        grid=(batch, nt),
        in_specs=[rows(d), rows(d), rows(2 * RET_QK_W), rows(RET_V_W), rows(RET_V_W),
                  rows(POOL_W), p_prev, p_next, pl.BlockSpec((ts, POOL_W), lambda b, i: (nt - 1 - i, 0)),
                  rows(MEM_Q_W), state, kv_spec, dmat_spec, table_spec, table_spec, table_spec, dec_spec,
                  gate_weight(0), gate_weight(1), gate_weight(2),
                  _layer_resident(w_ret_o.shape, layer), _layer_resident(w_pool.shape, layer),
                  _layer_resident(w_mem_o.shape, layer), _layer_resident(w_out.shape, layer)],
        out_specs=rows(d),
        out_shape=jax.ShapeDtypeStruct((t, d), F32),
        scratch_shapes=[pltpu.VMEM((ts, RET_V_W), BF16), pltpu.VMEM((ts, N_BRANCHES * d), BF16),
                        pltpu.VMEM((RET_HEADS, RET_QK_DIM, RET_V_DIM), F32)],
        compiler_params=_params("arbitrary", "arbitrary"),
        name="mix",
    )(x2, half_h, qk, v, g, p, p, p, inv_count, qm, states_f, kv.reshape((-1,) + kv.shape[2:]), dmat, xi_f, xi_b,
      zeta_b, dec, w_in, w_in, w_in, w_ret_o, w_pool, w_mem_o, w_out)


def _mlp_kernel(x_ref, gain_ref, w1_ref, w2_ref, final_gain_ref, out_ref, hid_scr, *, final_norm):
    for r0 in range(0, x_ref.shape[0], MLP_ROWS):
        rs = slice(r0, r0 + MLP_ROWS)
        x = x_ref[rs, :]
        h = (x * gain_ref[...]).astype(BF16)
        for col in range(0, w1_ref.shape[1], COL_CHUNK):
            hid = jnp.maximum(_dot(h, w1_ref[:, col:col + COL_CHUNK]), 0.0)
            hid_scr[rs, col:col + COL_CHUNK] = (hid * hid).astype(BF16)
        acc = _dot(hid_scr[rs, :], w2_ref[...])
        out = x + acc / (jnp.mean(x * x, axis=-1, keepdims=True) + EPS)
        out_ref[rs, :] = _rms_norm(out, final_gain_ref[...]) if final_norm else out


def _mlp(x2, gains, w1, w2, final_gain, layer, final_norm):
    t, d = x2.shape
    tm = MLP_TILE
    rows = pl.BlockSpec((tm, d), lambda i: (i, 0))
    return pl.pallas_call(
        functools.partial(_mlp_kernel, final_norm=final_norm),
        grid=(t // tm,),
        in_specs=[rows, _layer_resident(gains.shape, layer), _layer_resident(w1.shape, layer),
                  _layer_resident(w2.shape, layer), _resident((1, d))],
        out_specs=rows,
        out_shape=jax.ShapeDtypeStruct((t, d), F32),
        scratch_shapes=[pltpu.VMEM((tm, w1.shape[2]), BF16)],
        compiler_params=_params("arbitrary"),
        name="mlp",
    )(x2, gains, w1, w2, final_gain.reshape(1, d))


def _pool_weight_kernel(w_grp_ref, scale_ref, w_o_ref, out_ref):
    for gi in range(len(POOL_WINDOWS)):
        rows = slice(gi * POOL_GROUP, (gi + 1) * POOL_GROUP)
        scaled = w_grp_ref[0, gi] * scale_ref[0, :, rows]
        out_ref[0, rows, :] = jnp.dot(scaled, w_o_ref[0, rows, :], preferred_element_type=F32,
                                      precision=lax.Precision.HIGHEST).astype(BF16)


def _pool_weights(w_pool_grp, pool_scale, w_pool_o):
    depth, groups, group_w, _ = w_pool_grp.shape
    d = w_pool_o.shape[2]
    return pl.pallas_call(
        _pool_weight_kernel,
        grid=(depth,),
        in_specs=[pl.BlockSpec((1, groups, group_w, group_w), lambda l: (l, 0, 0, 0)),
                  pl.BlockSpec((1, 1, POOL_W), lambda l: (l, 0, 0)),
                  pl.BlockSpec((1, POOL_W, d), lambda l: (l, 0, 0))],
        out_specs=pl.BlockSpec((1, POOL_W, d), lambda l: (l, 0, 0)),
        out_shape=jax.ShapeDtypeStruct((depth, POOL_W, d), BF16),
        compiler_params=_params("arbitrary"),
        name="pool_weights",
    )(w_pool_grp, pool_scale.reshape(depth, 1, POOL_W), w_pool_o)


def _rotary_tables(seq):
    inv = ROPE_BASE ** (-np.arange(0, RET_QK_DIM, 2, dtype=np.float64) / RET_QK_DIM)
    ang = np.arange(seq, dtype=np.float64)[:, None] * inv[None, :]
    cos, sin = np.cos(ang), np.sin(ang)
    return (jnp.asarray(np.concatenate([cos, cos], axis=1), F32),
            jnp.asarray(np.concatenate([-sin, sin], axis=1), F32))


def _pool_inv_counts(seq):
    pos = np.arange(seq)
    cols = []
    for w in POOL_WINDOWS:
        count = np.minimum(pos + w // 2, seq) - np.maximum(pos - w // 2, 0)
        cols.append(np.broadcast_to((1.0 / count)[:, None], (seq, POOL_GROUP)))
    return jnp.asarray(np.concatenate(cols, axis=1), F32)


def kernel(x, mem, w_in, ret_decay_logit, w_ret_o, w_pool_grp, pool_scale, w_pool_o, w_mem_kv, w_mem_o,
           w_out, w_ff1, w_ff2, norm1_g, norm2_g, mem_norm_g, final_norm_g):
    batch, seq, d = x.shape
    depth = w_in.shape[0]
    assert seq % MIX_TILE == 0 and seq % ROW_TILE == 0 and MIX_TILE % RET_CHUNK == 0
    assert d % COL_CHUNK == 0 and POOL_HALO % 8 == 0

    cos, sin = _rotary_tables(seq)
    inv_count = _pool_inv_counts(seq)
    dmat, xi_f, xi_b, zeta_f, zeta_b, dec = _decay_tables(ret_decay_logit)
    kv = _mem_kv(mem, mem_norm_g, w_mem_kv.astype(BF16))

    w_pool = _pool_weights(w_pool_grp, pool_scale, w_pool_o)
    w_in_b = w_in.astype(BF16)
    w_ret_o_b = w_ret_o.astype(BF16)
    w_mem_o_b = w_mem_o.astype(BF16)
    w_out_b = w_out.astype(BF16)
    w_ff1_b = w_ff1.astype(BF16)
    w_ff2_b = w_ff2.astype(BF16)
    gains1 = norm1_g.reshape(depth, 1, d)
    gains2 = norm2_g.reshape(depth, 1, d)

    x2 = x.reshape(batch * seq, d)
    for l in range(depth):
        qk, v, g, p, qm, states_f, half_h = _in_proj(x2, gains1, w_in_b, cos, sin, zeta_f, dec, l, seq)
        x2 = _mix(x2, half_h, qk, v, g, p, inv_count, qm, states_f, kv, dmat, xi_f, xi_b, zeta_b, dec, w_in_b,
                  w_ret_o_b, w_pool, w_mem_o_b, w_out_b, l, batch, seq)
        x2 = _mlp(x2, gains2, w_ff1_b, w_ff2_b, final_norm_g, l, l == depth - 1)
    return x2.reshape(batch, seq, d)
```

```python
import functools

import jax
import jax.numpy as jnp
import numpy as np
from jax import lax
from jax.experimental import pallas as pl
from jax.experimental.pallas import tpu as pltpu

F32 = jnp.float32
BF16 = jnp.bfloat16

RET_HEADS = 4
RET_QK_DIM = 128
RET_V_DIM = 256
MEM_HEADS = 4
MEM_HEAD_DIM = 128
POOL_WINDOWS = (2, 4, 8, 16)
POOL_GROUP = 128
N_BRANCHES = 3
ROPE_BASE = 10000.0
EPS = 1e-6
LOG2_E = 1.4426950408889634

RET_QK_W = RET_HEADS * RET_QK_DIM
RET_V_W = RET_HEADS * RET_V_DIM
MEM_Q_W = MEM_HEADS * MEM_HEAD_DIM
POOL_W = POOL_GROUP * len(POOL_WINDOWS)
POOL_HALO = max(POOL_WINDOWS) // 2
GATE_COL0 = 2 * RET_QK_W + 2 * RET_V_W + POOL_W + MEM_Q_W

V7X_VMEM_BYTES = 64 * 1024 * 1024
VMEM_LIMIT_BYTES = V7X_VMEM_BYTES - 4 * 1024 * 1024

RET_CHUNK = 256
ROW_TILE = 1024
MLP_TILE = 1024
MIX_TILE = 512
COL_CHUNK = 512
IN_PROJ_CHUNK = 256
IN_PROJ_ROWS = 512
MLP_ROWS = 512
MIX_FILL_CHUNK = 256


def _params(*semantics):
    return pltpu.CompilerParams(dimension_semantics=semantics, vmem_limit_bytes=VMEM_LIMIT_BYTES)


def _resident(shape):
    zeros = (0,) * len(shape)
    return pl.BlockSpec(shape, lambda *_: zeros, pipeline_mode=pl.Buffered(1))


def _layer_resident(stacked_shape, layer):
    index = (layer,) + (0,) * (len(stacked_shape) - 1)
    return pl.BlockSpec((None,) + tuple(stacked_shape[1:]), lambda *_: index, pipeline_mode=pl.Buffered(1))


def _rms_norm(x, gain):
    return x * lax.rsqrt(jnp.mean(x * x, axis=-1, keepdims=True) + EPS) * gain


def _swish(x):
    half = 0.5 * x
    return half + half * jnp.tanh(half)


_dot = functools.partial(jnp.dot, preferred_element_type=F32)
_dot_nt = functools.partial(lax.dot_general, dimension_numbers=(((1,), (1,)), ((), ())), preferred_element_type=F32)
_dot_tn = functools.partial(lax.dot_general, dimension_numbers=(((0,), (0,)), ((), ())), preferred_element_type=F32)


def _decay_tables_kernel(logit_ref, dmat_ref, xi_f_ref, xi_b_ref, zeta_f_ref, zeta_b_ref, dec_ref):
    c = RET_CHUNK
    logit = logit_ref[0]
    log_g = jnp.minimum(logit, 0.0) - jnp.log1p(jnp.exp(-jnp.abs(logit)))
    lg_f, lg_b = log_g[0], log_g[1]
    row = lax.broadcasted_iota(jnp.int32, (c, RET_QK_W), 0).astype(F32)
    xi_f_ref[0] = jnp.exp(lg_f * (row + 1.0)).astype(BF16)
    xi_b_ref[0] = jnp.exp(lg_b * (c - row)).astype(BF16)
    zeta_f_ref[0] = jnp.exp(lg_f * (c - 1.0 - row)).astype(BF16)
    zeta_b_ref[0] = jnp.exp(lg_b * row).astype(BF16)
    dec_ref[0, 0] = jnp.exp(lg_f * c)
    dec_ref[0, 1] = jnp.exp(lg_b * c)
    i = lax.broadcasted_iota(jnp.int32, (c, c), 0)
    j = lax.broadcasted_iota(jnp.int32, (c, c), 1)
    diff = (i - j).astype(F32)
    for h in range(RET_HEADS):
        lf = lg_f[:, h * RET_QK_DIM:h * RET_QK_DIM + 1]
        lb = lg_b[:, h * RET_QK_DIM:h * RET_QK_DIM + 1]
        fwd = jnp.exp(lf * jnp.maximum(diff, 0.0))
        bwd = jnp.exp(lb * jnp.maximum(-diff, 0.0))
        dmat_ref[0, h] = jnp.where(diff >= 0.0, fwd, bwd)


def _decay_tables(ret_decay_logit):
    depth = ret_decay_logit.shape[0]
    c = RET_CHUNK
    logit = jnp.repeat(ret_decay_logit.astype(F32), RET_QK_DIM, axis=-1)[:, :, None, :]
    vec = jax.ShapeDtypeStruct((depth, c, RET_QK_W), BF16)
    vec_spec = pl.BlockSpec((1, c, RET_QK_W), lambda l: (l, 0, 0))
    return pl.pallas_call(
        _decay_tables_kernel,
        grid=(depth,),
        in_specs=[pl.BlockSpec((1, 2, 1, RET_QK_W), lambda l: (l, 0, 0, 0))],
        out_specs=[pl.BlockSpec((1, RET_HEADS, c, c), lambda l: (l, 0, 0, 0)),
                   vec_spec, vec_spec, vec_spec, vec_spec,
                   pl.BlockSpec((1, 2, 1, RET_QK_W), lambda l: (l, 0, 0, 0))],
        out_shape=[jax.ShapeDtypeStruct((depth, RET_HEADS, c, c), F32), vec, vec, vec, vec,
                   jax.ShapeDtypeStruct((depth, 2, 1, RET_QK_W), F32)],
        compiler_params=_params("arbitrary"),
        name="decay_tables",
    )(logit)


def _mem_kv_kernel(mem_ref, gain_ref, w_ref, kv_ref):
    mem_n = _rms_norm(mem_ref[...], gain_ref[...]).astype(BF16)
    for l in range(w_ref.shape[0]):
        kv_ref[l] = _dot(mem_n, w_ref[l]).astype(BF16)


def _mem_kv(mem, mem_norm_g, w_mem_kv):
    b, m, d = mem.shape
    depth, _, kvw = w_mem_kv.shape
    rows = b * m
    tm = min(MLP_TILE, rows)
    assert rows % tm == 0
    kv = pl.pallas_call(
        _mem_kv_kernel,
        grid=(rows // tm,),
        in_specs=[pl.BlockSpec((tm, d), lambda i: (i, 0)),
                  _resident((1, d)),
                  _resident((depth, d, kvw))],
        out_specs=pl.BlockSpec((depth, tm, kvw), lambda i: (0, i, 0)),
        out_shape=jax.ShapeDtypeStruct((depth, rows, kvw), BF16),
        compiler_params=_params("arbitrary"),
        name="mem_kv",
    )(mem.reshape(rows, d), mem_norm_g.reshape(1, d), w_mem_kv)
    return kv.reshape(depth, b, m, kvw)


def _chunk_outer(k, v_ref, rows, zeta):
    kz = k * zeta
    return [_dot_tn(kz[:, h * RET_QK_DIM:(h + 1) * RET_QK_DIM], v_ref[rows, h * RET_V_DIM:(h + 1) * RET_V_DIM])
            for h in range(RET_HEADS)]


def _decay_and_add(acc, outer, dec):
    for h in range(RET_HEADS):
        acc[h] = acc[h] * dec[:, h * RET_QK_DIM:h * RET_QK_DIM + 1] + outer[h]


def _in_proj_kernel(x_ref, gain_ref, w_ref, cos_ref, sin_ref, zeta_f_ref, dec_ref,
                    qk_ref, v_ref, g_ref, p_ref, qm_ref, state_f_ref, half_h_ref, acc, outer_scr, *, tiles_per_seq):
    @pl.when(pl.program_id(0) % tiles_per_seq == 0)
    def _():
        acc[...] = jnp.zeros_like(acc)

    chunk = RET_CHUNK
    tm = x_ref.shape[0]
    n_chunks = tm // chunk
    outer_jobs = list(range(n_chunks))

    def outer_job():
        ci = outer_jobs.pop(0)
        rows = slice(ci * chunk, (ci + 1) * chunk)
        for hd, outer in enumerate(_chunk_outer(qk_ref[rows, RET_QK_W:], v_ref, rows, zeta_f_ref[0])):
            outer_scr[ci, hd] = outer

    def scan():
        for ci in range(n_chunks):
            for hd in range(RET_HEADS):
                state_f_ref[ci, hd] = acc[hd].astype(BF16)
            _decay_and_add(acc, outer_scr[ci], dec_ref[0, 0])

    scanned = False
    for r0 in range(0, tm, IN_PROJ_ROWS):
        rs = slice(r0, r0 + IN_PROJ_ROWS)
        ready_chunks = (r0 + IN_PROJ_ROWS) // chunk
        h = _rms_norm(x_ref[rs, :], gain_ref[...]).astype(BF16)
        half_h_ref[rs, :] = h * 0.5
        cos = cos_ref[rs, :]
        sin = sin_ref[rs, :]

        def rotary(a, scale, cos=cos, sin=sin):
            heads = []
            for hd in range(a.shape[1] // RET_QK_DIM):
                ah = a[:, hd * RET_QK_DIM:(hd + 1) * RET_QK_DIM]
                heads.append(ah * cos + pltpu.roll(ah, RET_QK_DIM // 2, axis=1) * sin)
            rotated = jnp.concatenate(heads, axis=1)
            return rotated if scale is None else rotated * scale

        groups = [
            (qk_ref, 0, RET_QK_W, lambda a, rotary=rotary: rotary(a, None)),
            (qk_ref, RET_QK_W, RET_QK_W, lambda a, rotary=rotary: rotary(a, RET_QK_DIM ** -0.5)),
            (v_ref, 0, RET_V_W, lambda a: a),
            (g_ref, 0, RET_V_W, _swish),
            (p_ref, 0, POOL_W, lambda a: a),
            (qm_ref, 0, MEM_Q_W, lambda a: a),
        ]
        w_col = 0
        for out_ref, out_col, width, epilogue in groups:
            for c in range(0, width, IN_PROJ_CHUNK):
                a = _dot(h, w_ref[:, w_col + c:w_col + c + IN_PROJ_CHUNK])
                out_ref[rs, out_col + c:out_col + c + IN_PROJ_CHUNK] = epilogue(a).astype(out_ref.dtype)
                if out_ref is not qk_ref and out_ref is not v_ref:
                    if outer_jobs and outer_jobs[0] < ready_chunks:
                        outer_job()
                    elif not outer_jobs and not scanned:
                        scan()
                        scanned = True
            w_col += width
    assert scanned and not outer_jobs


def _in_proj(x2, gains, w_in, cos, sin, zeta_f, dec, layer, seq):
    t, d = x2.shape
    tm = ROW_TILE
    c = RET_CHUNK
    pos_tiles = seq // tm
    w_cols = GATE_COL0

    def rows(width):
        return pl.BlockSpec((tm, width), lambda i: (i, 0))

    pos_spec = pl.BlockSpec((tm, RET_QK_DIM), lambda i: (i % pos_tiles, 0))
    w_spec = pl.BlockSpec((None, d, w_cols), lambda i: (layer, 0, 0), pipeline_mode=pl.Buffered(1))
    state_shape = (RET_HEADS, RET_QK_DIM, RET_V_DIM)
    return pl.pallas_call(
        functools.partial(_in_proj_kernel, tiles_per_seq=pos_tiles),
        grid=(t // tm,),
        in_specs=[rows(d), _layer_resident(gains.shape, layer), w_spec, pos_spec, pos_spec,
                  pl.BlockSpec((1, c, RET_QK_W), lambda i: (layer, 0, 0), pipeline_mode=pl.Buffered(1)),
                  pl.BlockSpec((1, 2, 1, RET_QK_W), lambda i: (layer, 0, 0, 0), pipeline_mode=pl.Buffered(1))],
        out_specs=[rows(2 * RET_QK_W), rows(RET_V_W), rows(RET_V_W), rows(POOL_W), rows(MEM_Q_W),
                   pl.BlockSpec((tm // c,) + state_shape, lambda i: (i, 0, 0, 0)), rows(d)],
        out_shape=[jax.ShapeDtypeStruct((t, 2 * RET_QK_W), BF16),
                   jax.ShapeDtypeStruct((t, RET_V_W), BF16),
                   jax.ShapeDtypeStruct((t, RET_V_W), BF16),
                   jax.ShapeDtypeStruct((t, POOL_W), F32),
                   jax.ShapeDtypeStruct((t, MEM_Q_W), BF16),
                   jax.ShapeDtypeStruct((t // c,) + state_shape, BF16),
                   jax.ShapeDtypeStruct((t, d), BF16)],
        scratch_shapes=[pltpu.VMEM(state_shape, F32), pltpu.VMEM((tm // c,) + state_shape, F32)],
        compiler_params=_params("arbitrary"),
        name="in_proj",
    )(x2, gains, w_in, cos, sin, zeta_f, dec)


def _window_sums(padded):
    length = padded.shape[0]
    halo = POOL_HALO
    ts = length - 2 * halo

    def ahead(a, k):
        return pltpu.roll(a, length - k, axis=0)

    def behind(a, k):
        return pltpu.roll(a, k, axis=0)

    sums = []
    for gi, w in enumerate(POOL_WINDOWS):
        a = padded[:, gi * POOL_GROUP:(gi + 1) * POOL_GROUP]
        span = 1
        while 2 * span < w:
            a = a + ahead(a, span)
            span *= 2
        assert 2 * span == w and span <= halo
        sums.append((a + behind(a, span))[halo:halo + ts, :])
    return sums


def _mix_kernel(x_ref, half_h_ref, qk_ref, v_ref, g_ref, p_ref, p_prev_ref, p_next_ref, inv_count_ref, qm_ref,
                state_f_ref, kv_ref, dmat_ref, xi_f_ref, xi_b_ref, zeta_b_ref, dec_ref,
                w_gate_ret_ref, w_gate_pool_ref, w_gate_mem_ref,
                w_ret_o_ref, w_pool_ref, w_mem_o_ref, w_out_ref,
                out_ref, ret_scr, gate_scr, acc_b):
    ts = x_ref.shape[0]
    c = RET_CHUNK
    n_tiles = pl.num_programs(1)
    tile = n_tiles - 1 - pl.program_id(1)

    @pl.when(pl.program_id(1) == 0)
    def _():
        acc_b[...] = jnp.zeros_like(acc_b)

    d = x_ref.shape[1]
    fill_cols = list(range(0, d, MIX_FILL_CHUNK))

    def out_proj_chunk(lhs, w_ref, col):
        return _dot(lhs, w_ref[:, col:col + MIX_FILL_CHUNK]).astype(BF16)

    gate_w_refs = (w_gate_ret_ref, w_gate_pool_ref, w_gate_mem_ref)
    gate_jobs = [(b, col) for b in range(N_BRANCHES) for col in fill_cols]

    def gate_job():
        b, col = gate_jobs.pop(0)
        z_half = _dot(half_h_ref[...], gate_w_refs[b][:, col:col + MIX_FILL_CHUNK])
        gate_scr[:, b * d + col:b * d + col + MIX_FILL_CHUNK] = jnp.tanh(z_half).astype(BF16) * 0.5 + 0.5

    xi_f = xi_f_ref[0]
    xi_b = xi_b_ref[0]
    def normalise_and_gate(o, rows, v_cols):
        mu = jnp.mean(o, axis=-1, keepdims=True)
        cen = o - mu
        var = jnp.mean(cen * cen, axis=-1, keepdims=True)
        o_n = cen * lax.rsqrt(var + EPS)
        ret_scr[rows, v_cols] = o_n.astype(BF16) * g_ref[rows, v_cols]

    pending = None
    for ci in reversed(range(ts // c)):
        rows = slice(ci * c, (ci + 1) * c)
        q = qk_ref[rows, :RET_QK_W]
        q_f = q * xi_f
        q_b = q * xi_b
        for h in range(RET_HEADS):
            qk_cols = slice(h * RET_QK_DIM, (h + 1) * RET_QK_DIM)
            v_cols = slice(h * RET_V_DIM, (h + 1) * RET_V_DIM)
            k_h = qk_ref[rows, RET_QK_W + h * RET_QK_DIM:RET_QK_W + (h + 1) * RET_QK_DIM]
            s = _dot_nt(q[:, qk_cols], k_h) * dmat_ref[0, h]
            q_fb = jnp.concatenate([q_f[:, qk_cols], q_b[:, qk_cols]], axis=1)
            state = jnp.concatenate([state_f_ref[ci, h], acc_b[h].astype(BF16)], axis=0)
            o = _dot(s.astype(BF16), v_ref[rows, v_cols]) + _dot(q_fb, state)
            if pending is not None:
                normalise_and_gate(*pending)
                if gate_jobs:
                    gate_job()
            pending = (o, rows, v_cols)
        _decay_and_add(acc_b, _chunk_outer(qk_ref[rows, RET_QK_W:], v_ref, rows, zeta_b_ref[0]), dec_ref[0, 1])
    normalise_and_gate(*pending)
    if gate_jobs:
        gate_job()

    p = p_ref[...]
    padded = jnp.concatenate([jnp.where(tile > 0, p_prev_ref[...], 0.0), p,
                              jnp.where(tile < n_tiles - 1, p_next_ref[...], 0.0)], axis=0)
    ret_lhs = ret_scr[...]
    groups, o_ret = [], []
    for gi, win in enumerate(_window_sums(padded)):
        cols = slice(gi * POOL_GROUP, (gi + 1) * POOL_GROUP)
        groups.append((win * inv_count_ref[:, cols] - p[:, cols]).astype(BF16))
        if gi < len(fill_cols):
            o_ret.append(out_proj_chunk(ret_lhs, w_ret_o_ref, fill_cols[gi]))
    o_ret += [out_proj_chunk(ret_lhs, w_ret_o_ref, col) for col in fill_cols[len(o_ret):]]
    pool_lhs = jnp.concatenate(groups, axis=1)

    exp2_scale = (MEM_HEAD_DIM ** -0.5) * LOG2_E
    heads, o_pool = [], []

    def scores(h):
        cols = slice(h * MEM_HEAD_DIM, (h + 1) * MEM_HEAD_DIM)
        return _dot_nt(qm_ref[:, cols], kv_ref[0, :, cols])

    def attend(h, s):
        v_h = kv_ref[0, :, MEM_Q_W + h * MEM_HEAD_DIM:MEM_Q_W + (h + 1) * MEM_HEAD_DIM]
        e = jnp.exp2((s - jnp.max(s, axis=-1, keepdims=True)) * exp2_scale)
        o = _dot(e.astype(BF16), v_h) / jnp.sum(e, axis=-1, keepdims=True)
        heads.append(o.astype(BF16))
        if gate_jobs:
            gate_job()
        if h < len(fill_cols):
            o_pool.append(out_proj_chunk(pool_lhs, w_pool_ref, fill_cols[h]))

    s_next = scores(0)
    for h in range(MEM_HEADS):
        s_cur = s_next
        if h + 1 < MEM_HEADS:
            s_next = scores(h + 1)
        attend(h, s_cur)
    while gate_jobs:
        gate_job()
    o_pool += [out_proj_chunk(pool_lhs, w_pool_ref, col) for col in fill_cols[len(o_pool):]]
    mem_lhs = jnp.concatenate(heads, axis=1)

    merged = []
    for j, col in enumerate(fill_cols):
        o_mem = out_proj_chunk(mem_lhs, w_mem_o_ref, col)
        gates = [gate_scr[:, b * d + col:b * d + col + MIX_FILL_CHUNK] for b in range(N_BRANCHES)]
        merged.append(gates[0] * o_ret[j] + gates[1] * o_pool[j] + gates[2] * o_mem)
    out_ref[...] = x_ref[...] + _dot(jnp.concatenate(merged, axis=1), w_out_ref[...])


def _mix(x2, half_h, qk, v, g, p, inv_count, qm, states_f, kv, dmat, xi_f, xi_b, zeta_b, dec, w_in, w_ret_o, w_pool,
         w_mem_o, w_out, layer, batch, seq):
    t, d = x2.shape
    assert GATE_COL0 % d == 0

    def gate_weight(branch):
        return pl.BlockSpec((None, d, d), lambda b, i: (layer, 0, GATE_COL0 // d + branch),
                            pipeline_mode=pl.Buffered(1))

    ts = MIX_TILE
    c = RET_CHUNK
    nt = seq // ts
    halo = POOL_HALO
    halo_per_tile = ts // halo
    last_halo = t // halo - 1

    def tile_row(b, i):
        return b * nt + nt - 1 - i

    def rows(width):
        return pl.BlockSpec((ts, width), lambda b, i: (tile_row(b, i), 0))

    p_prev = pl.BlockSpec((halo, POOL_W), lambda b, i: (jnp.maximum(tile_row(b, i) * halo_per_tile - 1, 0), 0))
    p_next = pl.BlockSpec((halo, POOL_W),
                          lambda b, i: (jnp.minimum((tile_row(b, i) + 1) * halo_per_tile, last_halo), 0))
    state = pl.BlockSpec((ts // c,) + states_f.shape[1:], lambda b, i: (tile_row(b, i), 0, 0, 0))
    kv_spec = pl.BlockSpec((1,) + kv.shape[2:], lambda b, i: (layer * batch + b, 0, 0))
    dmat_spec = pl.BlockSpec((1, RET_HEADS, c, c), lambda b, i: (layer, 0, 0, 0), pipeline_mode=pl.Buffered(1))
    table_spec = pl.BlockSpec((1, c, RET_QK_W), lambda b, i: (layer, 0, 0), pipeline_mode=pl.Buffered(1))
    dec_spec = pl.BlockSpec((1, 2, 1, RET_QK_W), lambda b, i: (layer, 0, 0, 0), pipeline_mode=pl.Buffered(1))
    return pl.pallas_call(
        _mix_kernel,
        grid=(batch, nt),
        in_specs=[rows(d), rows(d), rows(2 * RET_QK_W), rows(RET_V_W), rows(RET_V_W),
                  rows(POOL_W), p_prev, p_next, pl.BlockSpec((ts, POOL_W), lambda b, i: (nt - 1 - i, 0)),
                  rows(MEM_Q_W), state, kv_spec, dmat_spec, table_spec, table_spec, table_spec, dec_spec,
                  gate_weight(0), gate_weight(1), gate_weight(2),
                  _layer_resident(w_ret_o.shape, layer), _layer_resident(w_pool.shape, layer),
                  _layer_resident(w_mem_o.shape, layer), _layer_resident(w_out.shape, layer)],
        out_specs=rows(d),
        out_shape=jax.ShapeDtypeStruct((t, d), F32),
        scratch_shapes=[pltpu.VMEM((ts, RET_V_W), BF16), pltpu.VMEM((ts, N_BRANCHES * d), BF16),
                        pltpu.VMEM((RET_HEADS, RET_QK_DIM, RET_V_DIM), F32)],
        compiler_params=_params("arbitrary", "arbitrary"),
        name="mix",
    )(x2, half_h, qk, v, g, p, p, p, inv_count, qm, states_f, kv.reshape((-1,) + kv.shape[2:]), dmat, xi_f, xi_b,
      zeta_b, dec, w_in, w_in, w_in, w_ret_o, w_pool, w_mem_o, w_out)


def _mlp_kernel(x_ref, gain_ref, w1_ref, w2_ref, final_gain_ref, out_ref, hid_scr, *, final_norm):
    block_rows = x_ref.shape[0] if final_norm else MLP_ROWS
    for r0 in range(0, x_ref.shape[0], block_rows):
        rs = slice(r0, r0 + block_rows)
        x = x_ref[rs, :]
        h = (x * gain_ref[...]).astype(BF16)
        for col in range(0, w1_ref.shape[1], COL_CHUNK):
            hid = jnp.maximum(_dot(h, w1_ref[:, col:col + COL_CHUNK]), 0.0)
            hid_scr[rs, col:col + COL_CHUNK] = (hid * hid).astype(BF16)
        acc = _dot(hid_scr[rs, :], w2_ref[...])
        out = x + acc / (jnp.mean(x * x, axis=-1, keepdims=True) + EPS)
        out_ref[rs, :] = _rms_norm(out, final_gain_ref[...]) if final_norm else out


def _mlp(x2, gains, w1, w2, final_gain, layer, final_norm):
    t, d = x2.shape
    tm = MLP_TILE
    rows = pl.BlockSpec((tm, d), lambda i: (i, 0))
    return pl.pallas_call(
        functools.partial(_mlp_kernel, final_norm=final_norm),
        grid=(t // tm,),
        in_specs=[rows, _layer_resident(gains.shape, layer), _layer_resident(w1.shape, layer),
                  _layer_resident(w2.shape, layer), _resident((1, d))],
        out_specs=rows,
        out_shape=jax.ShapeDtypeStruct((t, d), F32),
        scratch_shapes=[pltpu.VMEM((tm, w1.shape[2]), BF16)],
        compiler_params=_params("arbitrary"),
        name="mlp",
    )(x2, gains, w1, w2, final_gain.reshape(1, d))


def _pool_weight_kernel(w_grp_ref, scale_ref, w_o_ref, out_ref):
    for gi in range(len(POOL_WINDOWS)):
        rows = slice(gi * POOL_GROUP, (gi + 1) * POOL_GROUP)
        scaled = w_grp_ref[0, gi] * scale_ref[0, :, rows]
        out_ref[0, rows, :] = jnp.dot(scaled, w_o_ref[0, rows, :], preferred_element_type=F32,
                                      precision=lax.Precision.HIGHEST).astype(BF16)


def _pool_weights(w_pool_grp, pool_scale, w_pool_o):
    depth, groups, group_w, _ = w_pool_grp.shape
    d = w_pool_o.shape[2]
    return pl.pallas_call(
        _pool_weight_kernel,
        grid=(depth,),
        in_specs=[pl.BlockSpec((1, groups, group_w, group_w), lambda l: (l, 0, 0, 0)),
                  pl.BlockSpec((1, 1, POOL_W), lambda l: (l, 0, 0)),
                  pl.BlockSpec((1, POOL_W, d), lambda l: (l, 0, 0))],
        out_specs=pl.BlockSpec((1, POOL_W, d), lambda l: (l, 0, 0)),
        out_shape=jax.ShapeDtypeStruct((depth, POOL_W, d), BF16),
        compiler_params=_params("arbitrary"),
        name="pool_weights",
    )(w_pool_grp, pool_scale.reshape(depth, 1, POOL_W), w_pool_o)


def _rotary_tables(seq):
    inv = ROPE_BASE ** (-np.arange(0, RET_QK_DIM, 2, dtype=np.float64) / RET_QK_DIM)
    ang = np.arange(seq, dtype=np.float64)[:, None] * inv[None, :]
    cos, sin = np.cos(ang), np.sin(ang)
    return (jnp.asarray(np.concatenate([cos, cos], axis=1), F32),
            jnp.asarray(np.concatenate([-sin, sin], axis=1), F32))


def _pool_inv_counts(seq):
    pos = np.arange(seq)
    cols = []
    for w in POOL_WINDOWS:
        count = np.minimum(pos + w // 2, seq) - np.maximum(pos - w // 2, 0)
        cols.append(np.broadcast_to((1.0 / count)[:, None], (seq, POOL_GROUP)))
    return jnp.asarray(np.concatenate(cols, axis=1), F32)


def kernel(x, mem, w_in, ret_decay_logit, w_ret_o, w_pool_grp, pool_scale, w_pool_o, w_mem_kv, w_mem_o,
           w_out, w_ff1, w_ff2, norm1_g, norm2_g, mem_norm_g, final_norm_g):
    batch, seq, d = x.shape
    depth = w_in.shape[0]
    assert seq % MIX_TILE == 0 and seq % ROW_TILE == 0 and MIX_TILE % RET_CHUNK == 0
    assert d % COL_CHUNK == 0 and POOL_HALO % 8 == 0

    cos, sin = _rotary_tables(seq)
    inv_count = _pool_inv_counts(seq)
    dmat, xi_f, xi_b, zeta_f, zeta_b, dec = _decay_tables(ret_decay_logit)
    kv = _mem_kv(mem, mem_norm_g, w_mem_kv.astype(BF16))

    w_pool = _pool_weights(w_pool_grp, pool_scale, w_pool_o)
    w_in_b = w_in.astype(BF16)
    w_ret_o_b = w_ret_o.astype(BF16)
    w_mem_o_b = w_mem_o.astype(BF16)
    w_out_b = w_out.astype(BF16)
    w_ff1_b = w_ff1.astype(BF16)
    w_ff2_b = w_ff2.astype(BF16)
    gains1 = norm1_g.reshape(depth, 1, d)
    gains2 = norm2_g.reshape(depth, 1, d)

    x2 = x.reshape(batch * seq, d)
    for l in range(depth):
        qk, v, g, p, qm, states_f, half_h = _in_proj(x2, gains1, w_in_b, cos, sin, zeta_f, dec, l, seq)
        x2 = _mix(x2, half_h, qk, v, g, p, inv_count, qm, states_f, kv, dmat, xi_f, xi_b, zeta_b, dec, w_in_b,
                  w_ret_o_b, w_pool, w_mem_o_b, w_out_b, l, batch, seq)
        x2 = _mlp(x2, gains2, w_ff1_b, w_ff2_b, final_norm_g, l, l == depth - 1)
    return x2.reshape(batch, seq, d)
```

```python
import functools

import jax
import jax.numpy as jnp
import numpy as np
from jax import lax
from jax.experimental import pallas as pl
from jax.experimental.pallas import tpu as pltpu

F32 = jnp.float32
BF16 = jnp.bfloat16

RET_HEADS = 4
RET_QK_DIM = 128
RET_V_DIM = 256
MEM_HEADS = 4
MEM_HEAD_DIM = 128
POOL_WINDOWS = (2, 4, 8, 16)
POOL_GROUP = 128
N_BRANCHES = 3
ROPE_BASE = 10000.0
EPS = 1e-6
LOG2_E = 1.4426950408889634

RET_QK_W = RET_HEADS * RET_QK_DIM
RET_V_W = RET_HEADS * RET_V_DIM
MEM_Q_W = MEM_HEADS * MEM_HEAD_DIM
POOL_W = POOL_GROUP * len(POOL_WINDOWS)
POOL_HALO = max(POOL_WINDOWS) // 2
GATE_COL0 = 2 * RET_QK_W + 2 * RET_V_W + POOL_W + MEM_Q_W

V7X_VMEM_BYTES = 64 * 1024 * 1024
VMEM_LIMIT_BYTES = V7X_VMEM_BYTES - 4 * 1024 * 1024

RET_CHUNK = 256
ROW_TILE = 1024
MLP_TILE = 1024
MIX_TILE = 512
COL_CHUNK = 512
IN_PROJ_CHUNK = 256
IN_PROJ_ROWS = 512
MLP_ROWS = 512
MIX_FILL_CHUNK = 256


def _params(*semantics):
    return pltpu.CompilerParams(dimension_semantics=semantics, vmem_limit_bytes=VMEM_LIMIT_BYTES)


def _resident(shape):
    zeros = (0,) * len(shape)
    return pl.BlockSpec(shape, lambda *_: zeros, pipeline_mode=pl.Buffered(1))


def _layer_resident(stacked_shape, layer):
    index = (layer,) + (0,) * (len(stacked_shape) - 1)
    return pl.BlockSpec((None,) + tuple(stacked_shape[1:]), lambda *_: index, pipeline_mode=pl.Buffered(1))


def _rms_norm(x, gain):
    return x * lax.rsqrt(jnp.mean(x * x, axis=-1, keepdims=True) + EPS) * gain


def _swish(x):
    half = 0.5 * x
    return half + half * jnp.tanh(half)


_dot = functools.partial(jnp.dot, preferred_element_type=F32)
_dot_nt = functools.partial(lax.dot_general, dimension_numbers=(((1,), (1,)), ((), ())), preferred_element_type=F32)
_dot_tn = functools.partial(lax.dot_general, dimension_numbers=(((0,), (0,)), ((), ())), preferred_element_type=F32)


def _decay_tables_kernel(logit_ref, dmat_ref, xi_f_ref, xi_b_ref, zeta_f_ref, zeta_b_ref, dec_ref):
    c = RET_CHUNK
    logit = logit_ref[0]
    log_g = jnp.minimum(logit, 0.0) - jnp.log1p(jnp.exp(-jnp.abs(logit)))
    lg_f, lg_b = log_g[0], log_g[1]
    row = lax.broadcasted_iota(jnp.int32, (c, RET_QK_W), 0).astype(F32)
    xi_f_ref[0] = jnp.exp(lg_f * (row + 1.0)).astype(BF16)
    xi_b_ref[0] = jnp.exp(lg_b * (c - row)).astype(BF16)
    zeta_f_ref[0] = jnp.exp(lg_f * (c - 1.0 - row)).astype(BF16)
    zeta_b_ref[0] = jnp.exp(lg_b * row).astype(BF16)
    dec_ref[0, 0] = jnp.exp(lg_f * c)
    dec_ref[0, 1] = jnp.exp(lg_b * c)
    i = lax.broadcasted_iota(jnp.int32, (c, c), 0)
    j = lax.broadcasted_iota(jnp.int32, (c, c), 1)
    diff = (i - j).astype(F32)
    for h in range(RET_HEADS):
        lf = lg_f[:, h * RET_QK_DIM:h * RET_QK_DIM + 1]
        lb = lg_b[:, h * RET_QK_DIM:h * RET_QK_DIM + 1]
        fwd = jnp.exp(lf * jnp.maximum(diff, 0.0))
        bwd = jnp.exp(lb * jnp.maximum(-diff, 0.0))
        dmat_ref[0, h] = jnp.where(diff >= 0.0, fwd, bwd)


def _decay_tables(ret_decay_logit):
    depth = ret_decay_logit.shape[0]
    c = RET_CHUNK
    logit = jnp.repeat(ret_decay_logit.astype(F32), RET_QK_DIM, axis=-1)[:, :, None, :]
    vec = jax.ShapeDtypeStruct((depth, c, RET_QK_W), BF16)
    vec_spec = pl.BlockSpec((1, c, RET_QK_W), lambda l: (l, 0, 0))
    return pl.pallas_call(
        _decay_tables_kernel,
        grid=(depth,),
        in_specs=[pl.BlockSpec((1, 2, 1, RET_QK_W), lambda l: (l, 0, 0, 0))],
        out_specs=[pl.BlockSpec((1, RET_HEADS, c, c), lambda l: (l, 0, 0, 0)),
                   vec_spec, vec_spec, vec_spec, vec_spec,
                   pl.BlockSpec((1, 2, 1, RET_QK_W), lambda l: (l, 0, 0, 0))],
        out_shape=[jax.ShapeDtypeStruct((depth, RET_HEADS, c, c), F32), vec, vec, vec, vec,
                   jax.ShapeDtypeStruct((depth, 2, 1, RET_QK_W), F32)],
        compiler_params=_params("arbitrary"),
        name="decay_tables",
    )(logit)


def _mem_kv_kernel(mem_ref, gain_ref, w_ref, kv_ref):
    mem_n = _rms_norm(mem_ref[...], gain_ref[...]).astype(BF16)
    for l in range(w_ref.shape[0]):
        kv_ref[l] = _dot(mem_n, w_ref[l]).astype(BF16)


def _mem_kv(mem, mem_norm_g, w_mem_kv):
    b, m, d = mem.shape
    depth, _, kvw = w_mem_kv.shape
    rows = b * m
    tm = min(MLP_TILE, rows)
    assert rows % tm == 0
    kv = pl.pallas_call(
        _mem_kv_kernel,
        grid=(rows // tm,),
        in_specs=[pl.BlockSpec((tm, d), lambda i: (i, 0)),
                  _resident((1, d)),
                  _resident((depth, d, kvw))],
        out_specs=pl.BlockSpec((depth, tm, kvw), lambda i: (0, i, 0)),
        out_shape=jax.ShapeDtypeStruct((depth, rows, kvw), BF16),
        compiler_params=_params("arbitrary"),
        name="mem_kv",
    )(mem.reshape(rows, d), mem_norm_g.reshape(1, d), w_mem_kv)
    return kv.reshape(depth, b, m, kvw)


def _chunk_outer(k, v_ref, rows, zeta):
    kz = k * zeta
    return [_dot_tn(kz[:, h * RET_QK_DIM:(h + 1) * RET_QK_DIM], v_ref[rows, h * RET_V_DIM:(h + 1) * RET_V_DIM])
            for h in range(RET_HEADS)]


def _decay_and_add(acc, outer, dec):
    for h in range(RET_HEADS):
        acc[h] = acc[h] * dec[:, h * RET_QK_DIM:h * RET_QK_DIM + 1] + outer[h]


def _in_proj_kernel(x_ref, gain_ref, w_ref, cos_ref, sin_ref, zeta_f_ref, dec_ref,
                    qk_ref, v_ref, g_ref, p_ref, qm_ref, state_f_ref, half_h_ref, acc, outer_scr, *, tiles_per_seq,
                    normalised):
    @pl.when(pl.program_id(0) % tiles_per_seq == 0)
    def _():
        acc[...] = jnp.zeros_like(acc)

    chunk = RET_CHUNK
    tm = x_ref.shape[0]
    n_chunks = tm // chunk
    outer_jobs = list(range(n_chunks))

    def outer_job():
        ci = outer_jobs.pop(0)
        rows = slice(ci * chunk, (ci + 1) * chunk)
        for hd, outer in enumerate(_chunk_outer(qk_ref[rows, RET_QK_W:], v_ref, rows, zeta_f_ref[0])):
            outer_scr[ci, hd] = outer

    def scan():
        for ci in range(n_chunks):
            for hd in range(RET_HEADS):
                state_f_ref[ci, hd] = acc[hd].astype(BF16)
            _decay_and_add(acc, outer_scr[ci], dec_ref[0, 0])

    scanned = False
    for r0 in range(0, tm, IN_PROJ_ROWS):
        rs = slice(r0, r0 + IN_PROJ_ROWS)
        ready_chunks = (r0 + IN_PROJ_ROWS) // chunk
        h = x_ref[rs, :] if normalised else _rms_norm(x_ref[rs, :], gain_ref[...]).astype(BF16)
        half_h_ref[rs, :] = h * 0.5
        cos = cos_ref[rs, :]
        sin = sin_ref[rs, :]

        def rotary(a, scale, cos=cos, sin=sin):
            heads = []
            for hd in range(a.shape[1] // RET_QK_DIM):
                ah = a[:, hd * RET_QK_DIM:(hd + 1) * RET_QK_DIM]
                heads.append(ah * cos + pltpu.roll(ah, RET_QK_DIM // 2, axis=1) * sin)
            rotated = jnp.concatenate(heads, axis=1)
            return rotated if scale is None else rotated * scale

        groups = [
            (qk_ref, 0, RET_QK_W, lambda a, rotary=rotary: rotary(a, None)),
            (qk_ref, RET_QK_W, RET_QK_W, lambda a, rotary=rotary: rotary(a, RET_QK_DIM ** -0.5)),
            (v_ref, 0, RET_V_W, lambda a: a),
            (g_ref, 0, RET_V_W, _swish),
            (p_ref, 0, POOL_W, lambda a: a),
            (qm_ref, 0, MEM_Q_W, lambda a: a),
        ]
        w_col = 0
        for out_ref, out_col, width, epilogue in groups:
            for c in range(0, width, IN_PROJ_CHUNK):
                a = _dot(h, w_ref[:, w_col + c:w_col + c + IN_PROJ_CHUNK])
                out_ref[rs, out_col + c:out_col + c + IN_PROJ_CHUNK] = epilogue(a).astype(out_ref.dtype)
                if out_ref is not qk_ref and out_ref is not v_ref:
                    if outer_jobs and outer_jobs[0] < ready_chunks:
                        outer_job()
                    elif not outer_jobs and not scanned:
                        scan()
                        scanned = True
            w_col += width
    assert scanned and not outer_jobs


def _in_proj(x2, gains, w_in, cos, sin, zeta_f, dec, layer, seq):
    t, d = x2.shape
    tm = ROW_TILE
    c = RET_CHUNK
    pos_tiles = seq // tm
    w_cols = GATE_COL0

    def rows(width):
        return pl.BlockSpec((tm, width), lambda i: (i, 0))

    pos_spec = pl.BlockSpec((tm, RET_QK_DIM), lambda i: (i % pos_tiles, 0))
    w_spec = pl.BlockSpec((None, d, w_cols), lambda i: (layer, 0, 0), pipeline_mode=pl.Buffered(1))
    state_shape = (RET_HEADS, RET_QK_DIM, RET_V_DIM)
    return pl.pallas_call(
        functools.partial(_in_proj_kernel, tiles_per_seq=pos_tiles, normalised=x2.dtype == BF16),
        grid=(t // tm,),
        in_specs=[rows(d), _layer_resident(gains.shape, layer), w_spec, pos_spec, pos_spec,
                  pl.BlockSpec((1, c, RET_QK_W), lambda i: (layer, 0, 0), pipeline_mode=pl.Buffered(1)),
                  pl.BlockSpec((1, 2, 1, RET_QK_W), lambda i: (layer, 0, 0, 0), pipeline_mode=pl.Buffered(1))],
        out_specs=[rows(2 * RET_QK_W), rows(RET_V_W), rows(RET_V_W), rows(POOL_W), rows(MEM_Q_W),
                   pl.BlockSpec((tm // c,) + state_shape, lambda i: (i, 0, 0, 0)), rows(d)],
        out_shape=[jax.ShapeDtypeStruct((t, 2 * RET_QK_W), BF16),
                   jax.ShapeDtypeStruct((t, RET_V_W), BF16),
                   jax.ShapeDtypeStruct((t, RET_V_W), BF16),
                   jax.ShapeDtypeStruct((t, POOL_W), F32),
                   jax.ShapeDtypeStruct((t, MEM_Q_W), BF16),
                   jax.ShapeDtypeStruct((t // c,) + state_shape, BF16),
                   jax.ShapeDtypeStruct((t, d), BF16)],
        scratch_shapes=[pltpu.VMEM(state_shape, F32), pltpu.VMEM((tm // c,) + state_shape, F32)],
        compiler_params=_params("arbitrary"),
        name="in_proj",
    )(x2, gains, w_in, cos, sin, zeta_f, dec)


def _window_sums(padded):
    length = padded.shape[0]
    halo = POOL_HALO
    ts = length - 2 * halo

    def ahead(a, k):
        return pltpu.roll(a, length - k, axis=0)

    def behind(a, k):
        return pltpu.roll(a, k, axis=0)

    sums = []
    for gi, w in enumerate(POOL_WINDOWS):
        a = padded[:, gi * POOL_GROUP:(gi + 1) * POOL_GROUP]
        span = 1
        while 2 * span < w:
            a = a + ahead(a, span)
            span *= 2
        assert 2 * span == w and span <= halo
        sums.append((a + behind(a, span))[halo:halo + ts, :])
    return sums


def _mix_kernel(x_ref, half_h_ref, qk_ref, v_ref, g_ref, p_ref, p_prev_ref, p_next_ref, inv_count_ref, qm_ref,
                state_f_ref, kv_ref, dmat_ref, xi_f_ref, xi_b_ref, zeta_b_ref, dec_ref,
                w_gate_ret_ref, w_gate_pool_ref, w_gate_mem_ref,
                w_ret_o_ref, w_pool_ref, w_mem_o_ref, w_out_ref,
                out_ref, ret_scr, gate_scr, acc_b):
    ts = x_ref.shape[0]
    c = RET_CHUNK
    n_tiles = pl.num_programs(1)
    tile = n_tiles - 1 - pl.program_id(1)

    @pl.when(pl.program_id(1) == 0)
    def _():
        acc_b[...] = jnp.zeros_like(acc_b)

    d = x_ref.shape[1]
    fill_cols = list(range(0, d, MIX_FILL_CHUNK))

    def out_proj_chunk(lhs, w_ref, col):
        return _dot(lhs, w_ref[:, col:col + MIX_FILL_CHUNK]).astype(BF16)

    gate_w_refs = (w_gate_ret_ref, w_gate_pool_ref, w_gate_mem_ref)
    gate_jobs = [(b, col) for b in range(N_BRANCHES) for col in fill_cols]

    def gate_job():
        b, col = gate_jobs.pop(0)
        z_half = _dot(half_h_ref[...], gate_w_refs[b][:, col:col + MIX_FILL_CHUNK])
        gate_scr[:, b * d + col:b * d + col + MIX_FILL_CHUNK] = jnp.tanh(z_half).astype(BF16) * 0.5 + 0.5

    xi_f = xi_f_ref[0]
    xi_b = xi_b_ref[0]
    def normalise_and_gate(o, rows, v_cols):
        mu = jnp.mean(o, axis=-1, keepdims=True)
        cen = o - mu
        var = jnp.mean(cen * cen, axis=-1, keepdims=True)
        o_n = cen * lax.rsqrt(var + EPS)
        ret_scr[rows, v_cols] = o_n.astype(BF16) * g_ref[rows, v_cols]

    pending = None
    for ci in reversed(range(ts // c)):
        rows = slice(ci * c, (ci + 1) * c)
        q = qk_ref[rows, :RET_QK_W]
        q_f = q * xi_f
        q_b = q * xi_b
        for h in range(RET_HEADS):
            qk_cols = slice(h * RET_QK_DIM, (h + 1) * RET_QK_DIM)
            v_cols = slice(h * RET_V_DIM, (h + 1) * RET_V_DIM)
            k_h = qk_ref[rows, RET_QK_W + h * RET_QK_DIM:RET_QK_W + (h + 1) * RET_QK_DIM]
            s = _dot_nt(q[:, qk_cols], k_h) * dmat_ref[0, h]
            q_fb = jnp.concatenate([q_f[:, qk_cols], q_b[:, qk_cols]], axis=1)
            state = jnp.concatenate([state_f_ref[ci, h], acc_b[h].astype(BF16)], axis=0)
            o = _dot(s.astype(BF16), v_ref[rows, v_cols]) + _dot(q_fb, state)
            if pending is not None:
                normalise_and_gate(*pending)
                if gate_jobs:
                    gate_job()
            pending = (o, rows, v_cols)
        _decay_and_add(acc_b, _chunk_outer(qk_ref[rows, RET_QK_W:], v_ref, rows, zeta_b_ref[0]), dec_ref[0, 1])
    normalise_and_gate(*pending)
    if gate_jobs:
        gate_job()

    p = p_ref[...]
    padded = jnp.concatenate([jnp.where(tile > 0, p_prev_ref[...], 0.0), p,
                              jnp.where(tile < n_tiles - 1, p_next_ref[...], 0.0)], axis=0)
    ret_lhs = ret_scr[...]
    groups, o_ret = [], []
    for gi, win in enumerate(_window_sums(padded)):
        cols = slice(gi * POOL_GROUP, (gi + 1) * POOL_GROUP)
        groups.append((win * inv_count_ref[:, cols] - p[:, cols]).astype(BF16))
        if gi < len(fill_cols):
            o_ret.append(out_proj_chunk(ret_lhs, w_ret_o_ref, fill_cols[gi]))
    o_ret += [out_proj_chunk(ret_lhs, w_ret_o_ref, col) for col in fill_cols[len(o_ret):]]
    pool_lhs = jnp.concatenate(groups, axis=1)

    exp2_scale = (MEM_HEAD_DIM ** -0.5) * LOG2_E
    heads, o_pool = [], []

    def scores(h):
        cols = slice(h * MEM_HEAD_DIM, (h + 1) * MEM_HEAD_DIM)
        return _dot_nt(qm_ref[:, cols], kv_ref[0, :, cols])

    def attend(h, s):
        v_h = kv_ref[0, :, MEM_Q_W + h * MEM_HEAD_DIM:MEM_Q_W + (h + 1) * MEM_HEAD_DIM]
        e = jnp.exp2((s - jnp.max(s, axis=-1, keepdims=True)) * exp2_scale)
        o = _dot(e.astype(BF16), v_h) / jnp.sum(e, axis=-1, keepdims=True)
        heads.append(o.astype(BF16))
        if gate_jobs:
            gate_job()
        if h < len(fill_cols):
            o_pool.append(out_proj_chunk(pool_lhs, w_pool_ref, fill_cols[h]))

    s_next = scores(0)
    for h in range(MEM_HEADS):
        s_cur = s_next
        if h + 1 < MEM_HEADS:
            s_next = scores(h + 1)
        attend(h, s_cur)
    while gate_jobs:
        gate_job()
    o_pool += [out_proj_chunk(pool_lhs, w_pool_ref, col) for col in fill_cols[len(o_pool):]]
    mem_lhs = jnp.concatenate(heads, axis=1)

    merged = []
    for j, col in enumerate(fill_cols):
        o_mem = out_proj_chunk(mem_lhs, w_mem_o_ref, col)
        gates = [gate_scr[:, b * d + col:b * d + col + MIX_FILL_CHUNK] for b in range(N_BRANCHES)]
        merged.append(gates[0] * o_ret[j] + gates[1] * o_pool[j] + gates[2] * o_mem)
    out_ref[...] = x_ref[...] + _dot(jnp.concatenate(merged, axis=1), w_out_ref[...])


def _mix(x2, half_h, qk, v, g, p, inv_count, qm, states_f, kv, dmat, xi_f, xi_b, zeta_b, dec, w_in, w_ret_o, w_pool,
         w_mem_o, w_out, layer, batch, seq):
    t, d = x2.shape
    assert GATE_COL0 % d == 0

    def gate_weight(branch):
        return pl.BlockSpec((None, d, d), lambda b, i: (layer, 0, GATE_COL0 // d + branch),
                            pipeline_mode=pl.Buffered(1))

    ts = MIX_TILE
    c = RET_CHUNK
    nt = seq // ts
    halo = POOL_HALO
    halo_per_tile = ts // halo
    last_halo = t // halo - 1

    def tile_row(b, i):
        return b * nt + nt - 1 - i

    def rows(width):
        return pl.BlockSpec((ts, width), lambda b, i: (tile_row(b, i), 0))

    p_prev = pl.BlockSpec((halo, POOL_W), lambda b, i: (jnp.maximum(tile_row(b, i) * halo_per_tile - 1, 0), 0))
    p_next = pl.BlockSpec((halo, POOL_W),
                          lambda b, i: (jnp.minimum((tile_row(b, i) + 1) * halo_per_tile, last_halo), 0))
    state = pl.BlockSpec((ts // c,) + states_f.shape[1:], lambda b, i: (tile_row(b, i), 0, 0, 0))
    kv_spec = pl.BlockSpec((1,) + kv.shape[2:], lambda b, i: (layer * batch + b, 0, 0))
    dmat_spec = pl.BlockSpec((1, RET_HEADS, c, c), lambda b, i: (layer, 0, 0, 0), pipeline_mode=pl.Buffered(1))
    table_spec = pl.BlockSpec((1, c, RET_QK_W), lambda b, i: (layer, 0, 0), pipeline_mode=pl.Buffered(1))
    dec_spec = pl.BlockSpec((1, 2, 1, RET_QK_W), lambda b, i: (layer, 0, 0, 0), pipeline_mode=pl.Buffered(1))
    return pl.pallas_call(
        _mix_kernel,
        grid=(batch, nt),
        in_specs=[rows(d), rows(d), rows(2 * RET_QK_W), rows(RET_V_W), rows(RET_V_W),
                  rows(POOL_W), p_prev, p_next, pl.BlockSpec((ts, POOL_W), lambda b, i: (nt - 1 - i, 0)),
                  rows(MEM_Q_W), state, kv_spec, dmat_spec, table_spec, table_spec, table_spec, dec_spec,
                  gate_weight(0), gate_weight(1), gate_weight(2),
                  _layer_resident(w_ret_o.shape, layer), _layer_resident(w_pool.shape, layer),
                  _layer_resident(w_mem_o.shape, layer), _layer_resident(w_out.shape, layer)],
        out_specs=rows(d),
        out_shape=jax.ShapeDtypeStruct((t, d), F32),
        scratch_shapes=[pltpu.VMEM((ts, RET_V_W), BF16), pltpu.VMEM((ts, N_BRANCHES * d), BF16),
                        pltpu.VMEM((RET_HEADS, RET_QK_DIM, RET_V_DIM), F32)],
        compiler_params=_params("arbitrary", "arbitrary"),
        name="mix",
    )(x2, half_h, qk, v, g, p, p, p, inv_count, qm, states_f, kv.reshape((-1,) + kv.shape[2:]), dmat, xi_f, xi_b,
      zeta_b, dec, w_in, w_in, w_in, w_ret_o, w_pool, w_mem_o, w_out)


def _mlp_kernel(x_ref, gain_ref, w1_ref, w2_ref, after_gain_ref, out_ref, *rest, final_norm):
    final_gain_ref = after_gain_ref
    hid_scr = rest[-1]
    block_rows = x_ref.shape[0] if final_norm else MLP_ROWS
    for r0 in range(0, x_ref.shape[0], block_rows):
        rs = slice(r0, r0 + block_rows)
        x = x_ref[rs, :]
        h = (x * gain_ref[...]).astype(BF16)
        for col in range(0, w1_ref.shape[1], COL_CHUNK):
            hid = jnp.maximum(_dot(h, w1_ref[:, col:col + COL_CHUNK]), 0.0)
            hid_scr[rs, col:col + COL_CHUNK] = (hid * hid).astype(BF16)
        acc = _dot(hid_scr[rs, :], w2_ref[...])
        out = x + acc / (jnp.mean(x * x, axis=-1, keepdims=True) + EPS)
        if final_norm:
            out_ref[rs, :] = _rms_norm(out, final_gain_ref[...])
        else:
            out_ref[rs, :] = out
            rest[0][rs, :] = _rms_norm(out, after_gain_ref[...]).astype(BF16)


def _mlp(x2, gains, w1, w2, after_gain, layer, final_norm):
    t, d = x2.shape
    tm = MLP_TILE
    rows = pl.BlockSpec((tm, d), lambda i: (i, 0))
    f32_out = jax.ShapeDtypeStruct((t, d), F32)
    return pl.pallas_call(
        functools.partial(_mlp_kernel, final_norm=final_norm),
        grid=(t // tm,),
        in_specs=[rows, _layer_resident(gains.shape, layer), _layer_resident(w1.shape, layer),
                  _layer_resident(w2.shape, layer), _resident((1, d))],
        out_specs=rows if final_norm else [rows, rows],
        out_shape=f32_out if final_norm else [f32_out, jax.ShapeDtypeStruct((t, d), BF16)],
        scratch_shapes=[pltpu.VMEM((tm, w1.shape[2]), BF16)],
        compiler_params=_params("arbitrary"),
        name="mlp",
    )(x2, gains, w1, w2, after_gain.reshape(1, d))


def _pool_weight_kernel(w_grp_ref, scale_ref, w_o_ref, out_ref):
    for gi in range(len(POOL_WINDOWS)):
        rows = slice(gi * POOL_GROUP, (gi + 1) * POOL_GROUP)
        scaled = w_grp_ref[0, gi] * scale_ref[0, :, rows]
        out_ref[0, rows, :] = jnp.dot(scaled, w_o_ref[0, rows, :], preferred_element_type=F32,
                                      precision=lax.Precision.HIGHEST).astype(BF16)


def _pool_weights(w_pool_grp, pool_scale, w_pool_o):
    depth, groups, group_w, _ = w_pool_grp.shape
    d = w_pool_o.shape[2]
    return pl.pallas_call(
        _pool_weight_kernel,
        grid=(depth,),
        in_specs=[pl.BlockSpec((1, groups, group_w, group_w), lambda l: (l, 0, 0, 0)),
                  pl.BlockSpec((1, 1, POOL_W), lambda l: (l, 0, 0)),
                  pl.BlockSpec((1, POOL_W, d), lambda l: (l, 0, 0))],
        out_specs=pl.BlockSpec((1, POOL_W, d), lambda l: (l, 0, 0)),
        out_shape=jax.ShapeDtypeStruct((depth, POOL_W, d), BF16),
        compiler_params=_params("arbitrary"),
        name="pool_weights",
    )(w_pool_grp, pool_scale.reshape(depth, 1, POOL_W), w_pool_o)


def _rotary_tables(seq):
    inv = ROPE_BASE ** (-np.arange(0, RET_QK_DIM, 2, dtype=np.float64) / RET_QK_DIM)
    ang = np.arange(seq, dtype=np.float64)[:, None] * inv[None, :]
    cos, sin = np.cos(ang), np.sin(ang)
    return (jnp.asarray(np.concatenate([cos, cos], axis=1), F32),
            jnp.asarray(np.concatenate([-sin, sin], axis=1), F32))


def _pool_inv_counts(seq):
    pos = np.arange(seq)
    cols = []
    for w in POOL_WINDOWS:
        count = np.minimum(pos + w // 2, seq) - np.maximum(pos - w // 2, 0)
        cols.append(np.broadcast_to((1.0 / count)[:, None], (seq, POOL_GROUP)))
    return jnp.asarray(np.concatenate(cols, axis=1), F32)


def kernel(x, mem, w_in, ret_decay_logit, w_ret_o, w_pool_grp, pool_scale, w_pool_o, w_mem_kv, w_mem_o,
           w_out, w_ff1, w_ff2, norm1_g, norm2_g, mem_norm_g, final_norm_g):
    batch, seq, d = x.shape
    depth = w_in.shape[0]
    assert seq % MIX_TILE == 0 and seq % ROW_TILE == 0 and MIX_TILE % RET_CHUNK == 0
    assert d % COL_CHUNK == 0 and POOL_HALO % 8 == 0

    cos, sin = _rotary_tables(seq)
    inv_count = _pool_inv_counts(seq)
    dmat, xi_f, xi_b, zeta_f, zeta_b, dec = _decay_tables(ret_decay_logit)
    kv = _mem_kv(mem, mem_norm_g, w_mem_kv.astype(BF16))

    w_pool = _pool_weights(w_pool_grp, pool_scale, w_pool_o)
    w_in_b = w_in.astype(BF16)
    w_ret_o_b = w_ret_o.astype(BF16)
    w_mem_o_b = w_mem_o.astype(BF16)
    w_out_b = w_out.astype(BF16)
    w_ff1_b = w_ff1.astype(BF16)
    w_ff2_b = w_ff2.astype(BF16)
    gains1 = norm1_g.reshape(depth, 1, d)
    gains2 = norm2_g.reshape(depth, 1, d)

    x2 = x.reshape(batch * seq, d)
    proj_in = x2
    for l in range(depth):
        qk, v, g, p, qm, states_f, half_h = _in_proj(proj_in, gains1, w_in_b, cos, sin, zeta_f, dec, l, seq)
        x2 = _mix(x2, half_h, qk, v, g, p, inv_count, qm, states_f, kv, dmat, xi_f, xi_b, zeta_b, dec, w_in_b,
                  w_ret_o_b, w_pool, w_mem_o_b, w_out_b, l, batch, seq)
        if l == depth - 1:
            x2 = _mlp(x2, gains2, w_ff1_b, w_ff2_b, final_norm_g, l, True)
        else:
            x2, proj_in = _mlp(x2, gains2, w_ff1_b, w_ff2_b, norm1_g[l + 1], l, False)
    return x2.reshape(batch, seq, d)
```
